```python
import math, functools
import jax, jax.numpy as jnp
from jax import lax
import numpy as np

D_MODEL = 1024
BATCH = 2
SEQ = 8192
DEPTH = 2
DEC_BATCH = 128
DEC_SEQ = 8
PAST_LEN = 2048
PAGE_SIZE = 128

ATT_WIDTH = D_MODEL // 2
SSM_WIDTH = D_MODEL - ATT_WIDTH
HEAD_DIM = 64
V_DIM = 2 * HEAD_DIM
N_HEADS = ATT_WIDTH // V_DIM
SSM_GROUP = 16
N_SSM_GROUPS = SSM_WIDTH // SSM_GROUP
SSM_STATE = 64
N_EXPERTS = 16
N_EXPERT_GROUPS = 4
EXPERTS_PER_GROUP = N_EXPERTS // N_EXPERT_GROUPS
TOP_K = 2
D_EXPERT = 512
Q_BLOCK = 128
DN_ALPHA = (2 * DEPTH) ** 0.25
DN_BETA = (8 * DEPTH) ** -0.25
LN_EPS = 1e-5
RMS_EPS = 1e-5

kernel_name = "hymba_s5_diffattn_moe_deepnorm_step"


def layer_norm(x, g, b):
    xf = x.astype(jnp.float32)
    mu = jnp.mean(xf, -1, keepdims=True)
    var = jnp.mean(jnp.square(xf - mu), -1, keepdims=True)
    return ((xf - mu) * lax.rsqrt(var + LN_EPS) * g.astype(jnp.float32) + b.astype(jnp.float32)).astype(x.dtype)


def alibi_slopes():
    return 2.0 ** (-8.0 * jnp.arange(1, N_HEADS + 1, dtype=jnp.float32) / N_HEADS)


def diff_attend(q1, q2, k1, k2, v, q_pos, k_pos, lam):
    scale = HEAD_DIM ** -0.5
    dist = (q_pos[:, None] - k_pos[None, :]).astype(jnp.float32)
    bias = jnp.where(dist >= 0, -alibi_slopes()[:, None, None] * dist, -jnp.inf)
    s1 = jnp.einsum('nqhd,nkhd->nhqk', q1, k1, preferred_element_type=jnp.float32) * scale + bias
    s2 = jnp.einsum('nqhd,nkhd->nhqk', q2, k2, preferred_element_type=jnp.float32) * scale + bias
    w = jax.nn.softmax(s1, axis=-1) - lam * jax.nn.softmax(s2, axis=-1)
    return jnp.einsum('nhqk,nkhv->nqhv', w.astype(v.dtype), v)


def attn_prompt(q1, q2, k, v, lam):
    n, s = q1.shape[:2]
    nb = s // Q_BLOCK
    k1, k2 = k[..., :HEAD_DIM], k[..., HEAD_DIM:]
    k_pos = jnp.arange(s)
    to_blocks = lambda q: q.reshape(n, nb, Q_BLOCK, N_HEADS, HEAD_DIM).swapaxes(0, 1)

    def block(args):
        i, qb1, qb2 = args
        q_pos = i * Q_BLOCK + jnp.arange(Q_BLOCK)
        return diff_attend(qb1, qb2, k1, k2, v, q_pos, k_pos, lam)

    out = lax.map(block, (jnp.arange(nb), to_blocks(q1), to_blocks(q2)))
    return out.swapaxes(0, 1).reshape(n, s, N_HEADS, V_DIM)


def attn_sample(q1, q2, k, v, lam, k_pages, v_pages, page_table):
    n, t = q1.shape[:2]
    past = page_table.shape[1] * PAGE_SIZE
    k_past = k_pages[page_table].reshape(n, past, N_HEADS, 2 * HEAD_DIM)
    v_past = v_pages[page_table].reshape(n, past, N_HEADS, V_DIM)
    k_all = jnp.concatenate([k_past, k.astype(k_past.dtype)], axis=1)
    v_all = jnp.concatenate([v_past, v.astype(v_past.dtype)], axis=1)
    q_pos = past + jnp.arange(t)
    k_pos = jnp.arange(past + t)
    return diff_attend(q1, q2, k_all[..., :HEAD_DIM], k_all[..., HEAD_DIM:], v_all, q_pos, k_pos, lam)


def complex_affine_combine(e1, e2):
    a1r, a1i, b1r, b1i = e1
    a2r, a2i, b2r, b2i = e2
    return (a2r * a1r - a2i * a1i,
            a2r * a1i + a2i * a1r,
            a2r * b1r - a2i * b1i + b2r,
            a2r * b1i + a2i * b1r + b2i)


def s5_mixer(u, h0_re, h0_im, lam_re, lam_im, log_dt, b_re, b_im, c_re, c_im, d_skip, w_glu, b_glu):
    n, t, _ = u.shape
    f32 = jnp.float32
    uf = u.astype(f32)
    ug = uf.reshape(n, t, N_SSM_GROUPS, SSM_GROUP)
    lr, li = lam_re.astype(f32), lam_im.astype(f32)
    dt = jnp.exp(log_dt.astype(f32))[:, None]
    mag = jnp.exp(lr * dt)
    a_re, a_im = mag * jnp.cos(li * dt), mag * jnp.sin(li * dt)
    den = lr * lr + li * li
    nr = a_re - 1.0
    coef_re = (nr * lr + a_im * li) / den
    coef_im = (a_im * lr - nr * li) / den
    br, bi = b_re.astype(f32), b_im.astype(f32)
    bb_re = coef_re[..., None] * br - coef_im[..., None] * bi
    bb_im = coef_re[..., None] * bi + coef_im[..., None] * br
    bu_re = jnp.einsum('ntgc,gpc->ntgp', ug, bb_re)
    bu_im = jnp.einsum('ntgc,gpc->ntgp', ug, bb_im)
    shp = bu_re.shape
    elems = (jnp.broadcast_to(a_re, shp), jnp.broadcast_to(a_im, shp), bu_re, bu_im)
    A_re, A_im, H_re, H_im = lax.associative_scan(complex_affine_combine, elems, axis=1)
    h0r = h0_re.astype(f32)[:, None]
    h0i = h0_im.astype(f32)[:, None]
    h_re = H_re + A_re * h0r - A_im * h0i
    h_im = H_im + A_re * h0i + A_im * h0r
    y = (jnp.einsum('ntgp,gcp->ntgc', h_re, c_re.astype(f32))
         - jnp.einsum('ntgp,gcp->ntgc', h_im, c_im.astype(f32)))
    y = y.reshape(n, t, SSM_WIDTH) + d_skip.astype(f32) * uf
    g = jax.nn.gelu(y)
    out = g * jax.nn.sigmoid(g @ w_glu.astype(f32) + b_glu.astype(f32))
    return out.astype(u.dtype), h_re[:, -1], h_im[:, -1]


def route(x2d, router_w, router_bias):
    scores = jax.nn.sigmoid(jnp.dot(x2d, router_w, preferred_element_type=jnp.float32))
    sel = scores + router_bias.astype(jnp.float32)
    grp = sel.reshape(-1, N_EXPERT_GROUPS, EXPERTS_PER_GROUP)
    grp_score = jnp.sum(lax.top_k(grp, TOP_K)[0], axis=-1)
    g_idx = jnp.argmax(grp_score, axis=-1)
    in_grp = (jnp.arange(N_EXPERTS) // EXPERTS_PER_GROUP)[None, :] == g_idx[:, None]
    _, e_idx = lax.top_k(jnp.where(in_grp, sel, -jnp.inf), TOP_K)
    w = jnp.take_along_axis(scores, e_idx, axis=-1)
    w = w / jnp.sum(w, axis=-1, keepdims=True)
    return jnp.sum(jax.nn.one_hot(e_idx, N_EXPERTS, dtype=jnp.float32) * w[..., None], axis=1)


def moe(x, router_w, router_bias, w_gate, w_up, w_down):
    x2 = x.reshape(-1, D_MODEL)
    gates = route(x2, router_w, router_bias)
    out = jnp.zeros(x2.shape, jnp.float32)
    for e in range(N_EXPERTS):
        h = jax.nn.silu(x2 @ w_gate[e]) * (x2 @ w_up[e])
        out = out + gates[:, e:e + 1] * (h @ w_down[e]).astype(jnp.float32)
    return out.astype(x.dtype).reshape(x.shape)


def layer_step(x, attn_fn, h0_re, h0_im, lp, router_w, router_bias, layer_idx):
    n, t, _ = x.shape
    f32 = jnp.float32
    proj = x @ lp['w_in']
    u = proj[..., :SSM_WIDTH]
    q = proj[..., SSM_WIDTH:SSM_WIDTH + ATT_WIDTH].reshape(n, t, N_HEADS, 2, HEAD_DIM)
    k = proj[..., SSM_WIDTH + ATT_WIDTH:SSM_WIDTH + 2 * ATT_WIDTH].reshape(n, t, N_HEADS, 2 * HEAD_DIM)
    v = proj[..., SSM_WIDTH + 2 * ATT_WIDTH:].reshape(n, t, N_HEADS, V_DIM)
    lam_init = 0.8 - 0.6 * math.exp(-0.3 * layer_idx)
    lam = (jnp.exp(jnp.sum(lp['lq1'].astype(f32) * lp['lk1'].astype(f32)))
           - jnp.exp(jnp.sum(lp['lq2'].astype(f32) * lp['lk2'].astype(f32))) + lam_init)
    a = attn_fn(q[..., 0, :], q[..., 1, :], k, v, lam).astype(f32)
    a = a * lax.rsqrt(jnp.mean(a * a, axis=-1, keepdims=True) + RMS_EPS) * lp['subln_g'].astype(f32) * (1.0 - lam_init)
    a = a.reshape(n, t, ATT_WIDTH).astype(x.dtype)
    s, h_re, h_im = s5_mixer(u, h0_re, h0_im, lp['lam_re'], lp['lam_im'], lp['log_dt'], lp['b_re'], lp['b_im'],
                             lp['c_re'], lp['c_im'], lp['d'], lp['w_glu'], lp['b_glu'])
    mix = jnp.concatenate([a, s], axis=-1) @ lp['w_out']
    x = layer_norm(DN_ALPHA * x + mix, lp['ln1_g'], lp['ln1_b'])
    x = layer_norm(DN_ALPHA * x + moe(x, router_w, router_bias, lp['w_gate'], lp['w_up'], lp['w_down']),
                   lp['ln2_g'], lp['ln2_b'])
    return x, k, v, h_re, h_im


def setup_inputs(seed: int = 0) -> dict:
    key = jax.random.key(seed)
    ks = jax.random.split(key, 40)
    f32 = jnp.float32
    nrm = lambda kk, shape, s: s * jax.random.normal(kk, shape, f32)
    n_pages = PAST_LEN // PAGE_SIZE
    n_pool = (DEC_BATCH * n_pages * 5) // 4
    qk_cols = SSM_WIDTH + 2 * ATT_WIDTH
    w_in = jnp.concatenate([nrm(ks[7], (DEPTH, D_MODEL, qk_cols), D_MODEL ** -0.5),
                            nrm(ks[8], (DEPTH, D_MODEL, ATT_WIDTH), DN_BETA * D_MODEL ** -0.5)], axis=-1)
    lam_im = (math.pi * jnp.arange(SSM_STATE, dtype=f32))[None, None, :] + nrm(ks[20], (DEPTH, N_SSM_GROUPS, SSM_STATE), 0.01)
    return {
        "x_prompt": nrm(ks[0], (BATCH, SEQ, D_MODEL), 1.0),
        "x_sample": nrm(ks[1], (DEC_BATCH, DEC_SEQ, D_MODEL), 1.0),
        "cache_k": nrm(ks[2], (DEPTH, n_pool, PAGE_SIZE, N_HEADS, 2 * HEAD_DIM), 1.0),
        "cache_v": nrm(ks[3], (DEPTH, n_pool, PAGE_SIZE, N_HEADS, V_DIM), 1.0),
        "state_ssm_re": nrm(ks[4], (DEPTH, DEC_BATCH, N_SSM_GROUPS, SSM_STATE), 0.5),
        "state_ssm_im": nrm(ks[5], (DEPTH, DEC_BATCH, N_SSM_GROUPS, SSM_STATE), 0.5),
        "page_table": jax.random.permutation(ks[6], n_pool)[:DEC_BATCH * n_pages].reshape(DEC_BATCH, n_pages).astype(jnp.int32),
        "w_in": w_in,
        "w_out": nrm(ks[9], (DEPTH, D_MODEL, D_MODEL), DN_BETA * D_MODEL ** -0.5),
        "ln1_g": 1.0 + nrm(ks[10], (DEPTH, D_MODEL), 0.02),
        "ln1_b": nrm(ks[11], (DEPTH, D_MODEL), 0.02),
        "ln2_g": 1.0 + nrm(ks[12], (DEPTH, D_MODEL), 0.02),
        "ln2_b": nrm(ks[13], (DEPTH, D_MODEL), 0.02),
        "lambda_q1": nrm(ks[14], (DEPTH, HEAD_DIM), 0.1),
        "lambda_k1": nrm(ks[15], (DEPTH, HEAD_DIM), 0.1),
        "lambda_q2": nrm(ks[16], (DEPTH, HEAD_DIM), 0.1),
        "lambda_k2": nrm(ks[17], (DEPTH, HEAD_DIM), 0.1),
        "subln_g": 1.0 + nrm(ks[18], (DEPTH, V_DIM), 0.02),
        "ssm_lambda_re": -0.5 + nrm(ks[19], (DEPTH, N_SSM_GROUPS, SSM_STATE), 0.01),
        "ssm_lambda_im": lam_im,
        "ssm_log_dt": jax.random.uniform(ks[21], (DEPTH, N_SSM_GROUPS), f32, math.log(0.001), math.log(0.1)),
        "ssm_b_re": nrm(ks[22], (DEPTH, N_SSM_GROUPS, SSM_STATE, SSM_GROUP), (2 * SSM_GROUP) ** -0.5),
        "ssm_b_im": nrm(ks[23], (DEPTH, N_SSM_GROUPS, SSM_STATE, SSM_GROUP), (2 * SSM_GROUP) ** -0.5),
        "ssm_c_re": nrm(ks[24], (DEPTH, N_SSM_GROUPS, SSM_GROUP, SSM_STATE), SSM_STATE ** -0.5),
        "ssm_c_im": nrm(ks[25], (DEPTH, N_SSM_GROUPS, SSM_GROUP, SSM_STATE), SSM_STATE ** -0.5),
        "ssm_d": nrm(ks[26], (DEPTH, SSM_WIDTH), 1.0),
        "ssm_w_glu": nrm(ks[27], (DEPTH, SSM_WIDTH, SSM_WIDTH), SSM_WIDTH ** -0.5),
        "ssm_b_glu": nrm(ks[28], (DEPTH, SSM_WIDTH), 0.01),
        "router_w": nrm(ks[29], (D_MODEL, N_EXPERTS), D_MODEL ** -0.5),
        "router_bias": nrm(ks[30], (N_EXPERTS,), 0.01),
        "moe_w_gate": nrm(ks[31], (DEPTH, N_EXPERTS, D_MODEL, D_EXPERT), D_MODEL ** -0.5),
        "moe_w_up": nrm(ks[32], (DEPTH, N_EXPERTS, D_MODEL, D_EXPERT), DN_BETA * D_MODEL ** -0.5),
        "moe_w_down": nrm(ks[33], (DEPTH, N_EXPERTS, D_EXPERT, D_MODEL), DN_BETA * D_EXPERT ** -0.5),
    }


def reference(x_prompt, x_sample, cache_k, cache_v, state_ssm_re, state_ssm_im, page_table,
              w_in, w_out, ln1_g, ln1_b, ln2_g, ln2_b, lambda_q1, lambda_k1, lambda_q2, lambda_k2,
              subln_g, ssm_lambda_re, ssm_lambda_im, ssm_log_dt, ssm_b_re, ssm_b_im, ssm_c_re, ssm_c_im,
              ssm_d, ssm_w_glu, ssm_b_glu, router_w, router_bias, moe_w_gate, moe_w_up, moe_w_down):
    yp, ys = x_prompt, x_sample
    kp, vp, hrp, hip = [], [], [], []
    kss, vss, hrs, his = [], [], [], []
    for l in range(DEPTH):
        lp = dict(w_in=w_in[l], w_out=w_out[l], ln1_g=ln1_g[l], ln1_b=ln1_b[l], ln2_g=ln2_g[l], ln2_b=ln2_b[l],
                  lq1=lambda_q1[l], lk1=lambda_k1[l], lq2=lambda_q2[l], lk2=lambda_k2[l], subln_g=subln_g[l],
                  lam_re=ssm_lambda_re[l], lam_im=ssm_lambda_im[l], log_dt=ssm_log_dt[l],
                  b_re=ssm_b_re[l], b_im=ssm_b_im[l], c_re=ssm_c_re[l], c_im=ssm_c_im[l], d=ssm_d[l],
                  w_glu=ssm_w_glu[l], b_glu=ssm_b_glu[l],
                  w_gate=moe_w_gate[l], w_up=moe_w_up[l], w_down=moe_w_down[l])
        zeros = jnp.zeros((yp.shape[0], N_SSM_GROUPS, SSM_STATE), jnp.float32)
        yp, k_p, v_p, hr_p, hi_p = layer_step(yp, attn_prompt, zeros, zeros, lp, router_w, router_bias, l)
        sample_attn = functools.partial(attn_sample, k_pages=cache_k[l], v_pages=cache_v[l], page_table=page_table)
        ys, k_s, v_s, hr_s, hi_s = layer_step(ys, sample_attn, state_ssm_re[l], state_ssm_im[l], lp,
                                              router_w, router_bias, l)
        kp.append(k_p); vp.append(v_p); hrp.append(hr_p); hip.append(hi_p)
        kss.append(k_s); vss.append(v_s); hrs.append(hr_s); his.append(hi_s)
    return (yp, ys, jnp.stack(kp), jnp.stack(vp), jnp.stack(hrp), jnp.stack(hip),
            jnp.stack(kss), jnp.stack(vss), jnp.stack(hrs), jnp.stack(his))
```

```python
import functools
import math

import jax
import jax.numpy as jnp
from jax import lax
from jax.experimental import pallas as pl
from jax.experimental.pallas import tpu as pltpu

F32 = jnp.float32
BF16 = jnp.bfloat16

D_MODEL = 1024
ATT_WIDTH = 512
SSM_WIDTH = 512
HEAD_DIM = 64
V_DIM = 128
N_HEADS = 4
SSM_GROUP = 16
N_SSM_GROUPS = 32
SSM_STATE = 64
N_STATE = N_SSM_GROUPS * SSM_STATE
N_EXPERTS = 16
EXPERTS_PER_GROUP = 4
N_EXPERT_GROUPS = 4
D_EXPERT = 512
PAGE_SIZE = 128
DEPTH = 2
DN_ALPHA = (2 * DEPTH) ** 0.25
LN_EPS = 1e-5
RMS_EPS = 1e-5
ATTN_SCALE = HEAD_DIM ** -0.5

SUBLANES = 8
LANES = 128
SCAN_ROWS = SUBLANES
PROMPT_SEGMENTS = 4
VMEM_LIMIT = 52 * 1024 * 1024


def _cparams(*sem):
    return pltpu.CompilerParams(dimension_semantics=sem, vmem_limit_bytes=VMEM_LIMIT)


def _layer_norm(z, g, b):
    mu = jnp.mean(z, -1, keepdims=True)
    var = jnp.mean(jnp.square(z - mu), -1, keepdims=True)
    return (z - mu) * lax.rsqrt(var + LN_EPS) * g + b


def _prep_kernel(lre_ref, lim_ref, ldt_ref, bre_ref, bim_ref, lq1_ref, lk1_ref, lq2_ref, lk2_ref,
                 are_ref, aim_ref, bbre_ref, bbim_ref, lam_ref, *, lam_init):
    lr, li = lre_ref[...], lim_ref[...]
    dt = jnp.exp(ldt_ref[...])
    mag = jnp.exp(lr * dt)
    a_re, a_im = mag * jnp.cos(li * dt), mag * jnp.sin(li * dt)
    den = lr * lr + li * li
    nr = a_re - 1.0
    coef_re = (nr * lr + a_im * li) / den
    coef_im = (a_im * lr - nr * li) / den
    br, bi = bre_ref[...], bim_ref[...]
    bbre_ref[...] = coef_re[..., None] * br - coef_im[..., None] * bi
    bbim_ref[...] = coef_re[..., None] * bi + coef_im[..., None] * br
    are_ref[...] = a_re
    aim_ref[...] = a_im
    s1 = jnp.sum(lq1_ref[...] * lk1_ref[...], axis=-1, keepdims=True)
    s2 = jnp.sum(lq2_ref[...] * lk2_ref[...], axis=-1, keepdims=True)
    lam_ref[...] = jnp.exp(s1) - jnp.exp(s2) + lam_init


def _layer_prep(lam_re, lam_im, log_dt, b_re, b_im, lq1, lk1, lq2, lk2, lam_init):
    g, p, c = N_SSM_GROUPS, SSM_STATE, SSM_GROUP
    out_shape = (jax.ShapeDtypeStruct((g, p), F32), jax.ShapeDtypeStruct((g, p), F32),
                 jax.ShapeDtypeStruct((g, p, c), F32), jax.ShapeDtypeStruct((g, p, c), F32),
                 jax.ShapeDtypeStruct((1, 1), F32))
    return pl.pallas_call(
        functools.partial(_prep_kernel, lam_init=lam_init), out_shape=out_shape, name="layer_prep",
    )(lam_re, lam_im, log_dt.reshape(g, 1), b_re, b_im,
      lq1.reshape(1, HEAD_DIM), lk1.reshape(1, HEAD_DIM), lq2.reshape(1, HEAD_DIM), lk2.reshape(1, HEAD_DIM))


def _block_diag_in(bb):
    eye = jnp.eye(N_SSM_GROUPS, dtype=bb.dtype)
    m = jnp.transpose(bb, (0, 2, 1))[:, :, None, :] * eye[:, None, :, None]
    return m.reshape(SSM_WIDTH, N_STATE)


def _block_diag_out(c):
    eye = jnp.eye(N_SSM_GROUPS, dtype=c.dtype)
    m = jnp.transpose(c, (0, 2, 1))[:, :, None, :] * eye[:, None, :, None]
    return m.reshape(N_STATE, SSM_WIDTH)


def _proj_kernel(x_ref, w_ref, u_ref, q_ref, k_ref, v_ref):
    xb = x_ref[...].astype(BF16)
    for i, o_ref in enumerate((u_ref, q_ref, k_ref, v_ref)):
        o_ref[...] = jnp.dot(xb, w_ref[:, i * 512:(i + 1) * 512], preferred_element_type=F32)


def _in_proj(x2d, w_bf16):
    t = x2d.shape[0]
    tm = min(512, t)
    out = jax.ShapeDtypeStruct((t, 512), F32)
    row = pl.BlockSpec((tm, 512), lambda i: (i, 0))
    return pl.pallas_call(
        _proj_kernel, out_shape=(out, out, out, out), grid=(t // tm,),
        in_specs=[pl.BlockSpec((tm, D_MODEL), lambda i: (i, 0)),
                  pl.BlockSpec((D_MODEL, 4 * 512), lambda i: (0, 0))],
        out_specs=(row, row, row, row), compiler_params=_cparams("parallel"), name="in_proj",
    )(x2d, w_bf16)


def _sub_ln(a, g, lam_init):
    return a * lax.rsqrt(jnp.mean(a * a, axis=-1, keepdims=True) + RMS_EPS) * g * (1.0 - lam_init)


def _attn_prompt_kernel(slopes_ref, lam_ref, q_ref, k_ref, v_ref, g_ref, o_ref,
                        m1, l1, acc1, m2, l2, acc2, *, tq, tk, lam_init):
    h, qi, ki = pl.program_id(1), pl.program_id(2), pl.program_id(3)

    @pl.when(ki == 0)
    def _():
        for m, l, acc in ((m1, l1, acc1), (m2, l2, acc2)):
            m[...] = jnp.full(m.shape, -jnp.inf, F32)
            l[...] = jnp.zeros(l.shape, F32)
            acc[...] = jnp.zeros(acc.shape, F32)

    @pl.when(ki <= qi)
    def _():
        qs = q_ref[...] * ATTN_SCALE
        lane = lax.broadcasted_iota(jnp.int32, qs.shape, 1)
        q1 = jnp.where(lane < HEAD_DIM, qs, 0.0).astype(BF16)
        q2 = jnp.where(lane >= HEAD_DIM, qs, 0.0).astype(BF16)
        kb = k_ref[...].astype(BF16)
        vb = v_ref[...].astype(BF16)
        row = lax.broadcasted_iota(jnp.int32, (tq, tk), 0)
        col = lax.broadcasted_iota(jnp.int32, (tq, tk), 1)
        dist = (qi * tq - ki * tk) + row - col
        bias = jnp.where(dist >= 0, -slopes_ref[h] * dist.astype(F32), -jnp.inf)
        for qq, m, l, acc in ((q1, m1, l1, acc1), (q2, m2, l2, acc2)):
            s = lax.dot_general(qq, kb, (((1,), (1,)), ((), ())), preferred_element_type=F32) + bias
            m_prev = m[...]
            m_new = jnp.maximum(m_prev, jnp.max(s, axis=-1, keepdims=True))
            alpha = jnp.exp(m_prev - m_new)
            p = jnp.exp(s - m_new)
            l[...] = alpha * l[...] + jnp.sum(p, axis=-1, keepdims=True)
            acc[...] = alpha * acc[...] + jnp.dot(p.astype(BF16), vb, preferred_element_type=F32)
            m[...] = m_new

    @pl.when(ki == qi)
    def _():
        a = acc1[...] / l1[...] - lam_ref[0, 0] * (acc2[...] / l2[...])
        o_ref[...] = _sub_ln(a, g_ref[...], lam_init)


def _attn_prompt(q, k, v, slopes, lam, subln_g, lam_init):
    b, t, _ = q.shape
    tq = tk = min(512, t)
    nq, nk = t // tq, t // tk
    kernel = functools.partial(_attn_prompt_kernel, tq=tq, tk=tk, lam_init=lam_init)
    smem = pl.BlockSpec(memory_space=pltpu.SMEM)
    kv_spec = pl.BlockSpec((None, tk, V_DIM), lambda b_, h, qi, ki: (b_, jnp.minimum(ki, qi), h))
    q_spec = pl.BlockSpec((None, tq, V_DIM), lambda b_, h, qi, ki: (b_, qi, h))
    return pl.pallas_call(
        kernel, out_shape=jax.ShapeDtypeStruct((b, t, ATT_WIDTH), F32),
        grid=(b, N_HEADS, nq, nk),
        in_specs=[smem, smem, q_spec, kv_spec, kv_spec,
                  pl.BlockSpec((1, V_DIM), lambda b_, h, qi, ki: (0, 0))],
        out_specs=q_spec,
        scratch_shapes=[pltpu.VMEM((tq, 1), F32), pltpu.VMEM((tq, 1), F32), pltpu.VMEM((tq, V_DIM), F32),
                        pltpu.VMEM((tq, 1), F32), pltpu.VMEM((tq, 1), F32), pltpu.VMEM((tq, V_DIM), F32)],
        compiler_params=_cparams("parallel", "parallel", "parallel", "arbitrary"), name="attn_prompt",
    )(slopes, lam, q, k, v, subln_g.reshape(1, V_DIM))


def _attn_sample_kernel(pt_ref, lam_ref, q_ref, knew_ref, vnew_ref, g_ref, cslope_ref, *rest,
                        n_pages, dec_seq, lam_init):
    k_pages = rest[:n_pages]
    v_pages = rest[n_pages:2 * n_pages]
    o_ref, s_scr, kpad, vpad = rest[2 * n_pages:]
    del pt_ref
    past = n_pages * PAGE_SIZE
    n_cols = 2 * N_HEADS * dec_seq

    q = q_ref[...] * ATTN_SCALE
    qt = jnp.concatenate([q] * (2 * N_HEADS), axis=0)
    r_i = lax.broadcasted_iota(jnp.int32, qt.shape, 0)
    l_i = lax.broadcasted_iota(jnp.int32, qt.shape, 1)
    qsel = jnp.where(l_i // HEAD_DIM == r_i // dec_seq, qt, 0.0)
    qpad = jnp.concatenate([qsel, jnp.zeros((LANES - n_cols, ATT_WIDTH), F32)], axis=0)

    kpad[...] = jnp.zeros(kpad.shape, F32)
    vpad[...] = jnp.zeros(vpad.shape, F32)
    kpad[0:dec_seq, :] = knew_ref[...]
    vpad[0:dec_seq, :] = vnew_ref[...]

    row = lax.broadcasted_iota(jnp.int32, (PAGE_SIZE, LANES), 0)
    col = lax.broadcasted_iota(jnp.int32, (PAGE_SIZE, LANES), 1)
    qoff = col % dec_seq
    cslope = cslope_ref[...]
    dn = (((1,), (1,)), ((), ()))
    for p in range(n_pages):
        s = lax.dot_general(k_pages[p][...], qpad, dn, preferred_element_type=F32)
        dist = (past - p * PAGE_SIZE) + qoff - row
        s_scr[p * PAGE_SIZE:(p + 1) * PAGE_SIZE, :] = s - cslope * dist.astype(F32)
    s = lax.dot_general(kpad[...], qpad, dn, preferred_element_type=F32)
    dist = qoff - row
    ok = jnp.logical_and(row < dec_seq, dist >= 0)
    s_scr[past:past + PAGE_SIZE, :] = jnp.where(ok, s - cslope * dist.astype(F32), -jnp.inf)

    s_all = s_scr[...]
    m = jnp.max(s_all, axis=0, keepdims=True)
    p_all = jnp.exp(s_all - m)
    l = jnp.sum(p_all, axis=0, keepdims=True)
    c_i = lax.broadcasted_iota(jnp.int32, (1, LANES), 1)
    sign = jnp.where((c_i // dec_seq) % 2 == 0, 1.0, -lam_ref[0, 0])
    s_scr[...] = p_all * (sign / l)

    out = jnp.zeros((LANES, ATT_WIDTH), F32)
    for p in range(n_pages + 1):
        pt = s_scr[p * PAGE_SIZE:(p + 1) * PAGE_SIZE, :].T
        vp = v_pages[p][...] if p < n_pages else vpad[...]
        out = out + jnp.dot(pt, vp, preferred_element_type=F32)
    for h in range(N_HEADS):
        r0 = 2 * h * dec_seq
        a = (out[r0:r0 + dec_seq, h * V_DIM:(h + 1) * V_DIM]
             + out[r0 + dec_seq:r0 + 2 * dec_seq, h * V_DIM:(h + 1) * V_DIM])
        o_ref[:, h * V_DIM:(h + 1) * V_DIM] = _sub_ln(a, g_ref[...], lam_init)


def _attn_sample(q, k, v, k_pages, v_pages, page_table, cslope, lam, subln_g, lam_init):
    n, dec_seq, _ = q.shape
    n_pages = page_table.shape[1]
    assert 2 * N_HEADS * dec_seq <= LANES and dec_seq % SUBLANES == 0
    kernel = functools.partial(_attn_sample_kernel, n_pages=n_pages, dec_seq=dec_seq, lam_init=lam_init)
    tok = pl.BlockSpec((None, dec_seq, ATT_WIDTH), lambda i, pt: (i, 0, 0))

    def page_spec(p):
        return pl.BlockSpec((None, PAGE_SIZE, ATT_WIDTH), lambda i, pt: (pt[i * n_pages + p], 0, 0))

    grid_spec = pltpu.PrefetchScalarGridSpec(
        num_scalar_prefetch=1, grid=(n,),
        in_specs=[pl.BlockSpec(memory_space=pltpu.SMEM), tok, tok, tok,
                  pl.BlockSpec((1, V_DIM), lambda i, pt: (0, 0)),
                  pl.BlockSpec((1, LANES), lambda i, pt: (0, 0))]
                 + [page_spec(p) for p in range(n_pages)] * 2,
        out_specs=tok,
        scratch_shapes=[pltpu.VMEM(((n_pages + 1) * PAGE_SIZE, LANES), F32),
                        pltpu.VMEM((PAGE_SIZE, ATT_WIDTH), F32), pltpu.VMEM((PAGE_SIZE, ATT_WIDTH), F32)])
    return pl.pallas_call(
        kernel, out_shape=jax.ShapeDtypeStruct((n, dec_seq, ATT_WIDTH), F32), grid_spec=grid_spec,
        compiler_params=_cparams("arbitrary"), name="attn_sample",
    )(page_table.reshape(-1), lam, q, k, v, subln_g.reshape(1, V_DIM), cslope,
      *([k_pages] * n_pages), *([v_pages] * n_pages))


def _s5_scan_chunk(bu_scr, hs_scr, hre_scr, him_scr, are_ref, aim_ref, steps, lane_chunk):
    for c in range(N_STATE // lane_chunk):
        lo = c * lane_chunk
        are = jnp.broadcast_to(are_ref[:, lo:lo + lane_chunk], (SCAN_ROWS, lane_chunk))
        aim = jnp.broadcast_to(aim_ref[:, lo:lo + lane_chunk], (SCAN_ROWS, lane_chunk))

        def body(j, carry, lo=lo, are=are, aim=aim):
            hre, him = carry
            r0 = pl.multiple_of(j * SCAN_ROWS, SCAN_ROWS)
            bre = bu_scr[pl.ds(r0, SCAN_ROWS), lo:lo + lane_chunk]
            bim = bu_scr[pl.ds(r0, SCAN_ROWS), N_STATE + lo:N_STATE + lo + lane_chunk]
            nre = are * hre - aim * him + bre
            nim = are * him + aim * hre + bim
            if hs_scr is not None:
                hs_scr[pl.ds(r0, SCAN_ROWS), lo:lo + lane_chunk] = nre
                hs_scr[pl.ds(r0, SCAN_ROWS), N_STATE + lo:N_STATE + lo + lane_chunk] = nim
            return nre, nim

        hre, him = lax.fori_loop(0, steps, body,
                                 (hre_scr[:, lo:lo + lane_chunk], him_scr[:, lo:lo + lane_chunk]),
                                 unroll=min(4, steps))
        hre_scr[:, lo:lo + lane_chunk] = hre
        him_scr[:, lo:lo + lane_chunk] = him


def _s5_kernel(u_ref, h0re_ref, h0im_ref, are_ref, aim_ref, bbd_ref, *rest, steps, with_y):
    if with_y:
        cbd_ref, d_ref, wglu_ref, bglu_ref, s_ref, hre_ref, him_ref, hre_scr, him_scr, bu_scr, hs_scr = rest
    else:
        hre_ref, him_ref, hre_scr, him_scr, bu_scr = rest
        hs_scr = None
    c = pl.program_id(1)

    @pl.when(c == 0)
    def _():
        hre_scr[...] = h0re_ref[...]
        him_scr[...] = h0im_ref[...]

    u = u_ref[...]
    bu_scr[...] = jnp.dot(u.astype(BF16), bbd_ref[...], preferred_element_type=F32)
    _s5_scan_chunk(bu_scr, hs_scr, hre_scr, him_scr, are_ref, aim_ref, steps, lane_chunk=1024)
    if with_y:
        y = jnp.dot(hs_scr[...].astype(BF16), cbd_ref[...], preferred_element_type=F32) + d_ref[...] * u
        g = jax.nn.gelu(y)
        gate = jnp.dot(g.astype(BF16), wglu_ref[...], preferred_element_type=F32) + bglu_ref[...]
        s_ref[...] = g * jax.nn.sigmoid(gate)

    @pl.when(c == pl.num_programs(1) - 1)
    def _():
        hre_ref[...] = hre_scr[...]
        him_ref[...] = him_scr[...]


def _s5_call(u3, h0re, h0im, a_re, a_im, bbd, tail, steps_per_chunk, with_y):
    ng, rows, _ = u3.shape
    j_total = rows // SCAN_ROWS
    steps = min(steps_per_chunk, j_total)
    nc = j_total // steps
    blk = steps * SCAN_ROWS
    const = lambda shape: pl.BlockSpec(shape, lambda g, c: (0,) * len(shape))
    st_spec = pl.BlockSpec((None, SCAN_ROWS, N_STATE), lambda g, c: (g, 0, 0))
    u_spec = pl.BlockSpec((None, blk, SSM_WIDTH), lambda g, c: (g, c, 0))
    st_shape = jax.ShapeDtypeStruct((ng, SCAN_ROWS, N_STATE), F32)
    in_specs = [u_spec, st_spec, st_spec, const((1, N_STATE)), const((1, N_STATE)),
                const((SSM_WIDTH, 2 * N_STATE))]
    scratch = [pltpu.VMEM((SCAN_ROWS, N_STATE), F32), pltpu.VMEM((SCAN_ROWS, N_STATE), F32),
               pltpu.VMEM((blk, 2 * N_STATE), F32)]
    if with_y:
        in_specs += [const((2 * N_STATE, SSM_WIDTH)), const((1, SSM_WIDTH)),
                     const((SSM_WIDTH, SSM_WIDTH)), const((1, SSM_WIDTH))]
        out_shape = (jax.ShapeDtypeStruct((ng, rows, SSM_WIDTH), F32), st_shape, st_shape)
        out_specs = (u_spec, st_spec, st_spec)
        scratch.append(pltpu.VMEM((blk, 2 * N_STATE), F32))
    else:
        out_shape = (st_shape, st_shape)
        out_specs = (st_spec, st_spec)
    return pl.pallas_call(
        functools.partial(_s5_kernel, steps=steps, with_y=with_y),
        out_shape=out_shape, grid=(ng, nc), in_specs=in_specs, out_specs=out_specs,
        scratch_shapes=scratch, compiler_params=_cparams("parallel", "arbitrary"),
        name="s5_scan_glu" if with_y else "s5_scan_state",
    )(u3, h0re, h0im, a_re, a_im, bbd, *tail)


def _s5_seg_init_kernel(fre_ref, fim_ref, are_ref, aim_ref, ire_ref, iim_ref, *, seg_len, n_seq, n_seg):
    pr, pi = are_ref[...], aim_ref[...]
    ar, ai = None, None
    e = seg_len
    while e:
        if e & 1:
            ar, ai = (pr, pi) if ar is None else (ar * pr - ai * pi, ar * pi + ai * pr)
        e >>= 1
        if e:
            pr, pi = pr * pr - pi * pi, 2.0 * pr * pi
    for n in range(n_seq):
        hr = jnp.zeros((1, N_STATE), F32)
        hi = jnp.zeros((1, N_STATE), F32)
        for s in range(n_seg):
            r = n * n_seg + s
            ire_ref[r:r + 1, :] = hr
            iim_ref[r:r + 1, :] = hi
            fr, fi = fre_ref[r:r + 1, :], fim_ref[r:r + 1, :]
            hr, hi = ar * hr - ai * hi + fr, ar * hi + ai * hr + fi


def _s5_seg_init(f_re, f_im, a_re, a_im, seg_len, n_seq, n_seg):
    shape = jax.ShapeDtypeStruct(f_re.shape, F32)
    return pl.pallas_call(
        functools.partial(_s5_seg_init_kernel, seg_len=seg_len, n_seq=n_seq, n_seg=n_seg),
        out_shape=(shape, shape), name="s5_seg_init",
    )(f_re, f_im, a_re, a_im)


def _s5_prompt(u, a_re, a_im, bbd, tail):
    b, t, _ = u.shape
    n_seg = SCAN_ROWS // b
    assert b * n_seg == SCAN_ROWS and t % n_seg == 0
    j = t // n_seg
    u3 = u.reshape(b, n_seg, j, SSM_WIDTH).transpose(2, 0, 1, 3).reshape(1, j * SCAN_ROWS, SSM_WIDTH)
    zeros = jnp.zeros((1, SCAN_ROWS, N_STATE), F32)
    f_re, f_im = _s5_call(u3, zeros, zeros, a_re, a_im, bbd, (), 64, with_y=False)
    i_re, i_im = _s5_seg_init(f_re[0], f_im[0], a_re, a_im, j, b, n_seg)
    s3, h_re, h_im = _s5_call(u3, i_re[None], i_im[None], a_re, a_im, bbd, tail, 64, with_y=True)
    s = s3.reshape(j, b, n_seg, SSM_WIDTH).transpose(1, 2, 0, 3).reshape(b, t, SSM_WIDTH)
    last = h_re.reshape(b, n_seg, N_STATE)[:, -1], h_im.reshape(b, n_seg, N_STATE)[:, -1]
    return s, last[0], last[1]


def _s5_sample(u, h0_re, h0_im, a_re, a_im, bbd, tail):
    n, j, _ = u.shape
    ng = n // SCAN_ROWS
    u3 = u.reshape(ng, SCAN_ROWS, j, SSM_WIDTH).transpose(0, 2, 1, 3).reshape(ng, j * SCAN_ROWS, SSM_WIDTH)
    s3, h_re, h_im = _s5_call(u3, h0_re.reshape(ng, SCAN_ROWS, N_STATE), h0_im.reshape(ng, SCAN_ROWS, N_STATE),
                              a_re, a_im, bbd, tail, j, with_y=True)
    s = s3.reshape(ng, j, SCAN_ROWS, SSM_WIDTH).transpose(0, 2, 1, 3).reshape(n, j, SSM_WIDTH)
    return s, h_re.reshape(n, N_STATE), h_im.reshape(n, N_STATE)


def _top2_sum(a, b, c, d):
    m1, n1 = jnp.maximum(a, b), jnp.minimum(a, b)
    m2, n2 = jnp.maximum(c, d), jnp.minimum(c, d)
    return jnp.maximum(m1, m2) + jnp.maximum(jnp.minimum(m1, m2), jnp.maximum(n1, n2))


def _route_rows(logit_rows, rbias_ref):
    scores = [jax.nn.sigmoid(r) for r in logit_rows]
    sel = [scores[e] + rbias_ref[e] for e in range(N_EXPERTS)]
    gs = [_top2_sum(*sel[EXPERTS_PER_GROUP * g:EXPERTS_PER_GROUP * (g + 1)]) for g in range(N_EXPERT_GROUPS)]
    best_v, best_g = gs[0], jnp.zeros(gs[0].shape, jnp.int32)
    for g in range(1, N_EXPERT_GROUPS):
        upd = gs[g] > best_v
        best_g = jnp.where(upd, g, best_g)
        best_v = jnp.where(upd, gs[g], best_v)
    picked = []
    for e in range(N_EXPERTS):
        g = e // EXPERTS_PER_GROUP
        rank = jnp.zeros(sel[e].shape, jnp.int32)
        for o in range(EXPERTS_PER_GROUP * g, EXPERTS_PER_GROUP * (g + 1)):
            if o != e:
                ahead = (sel[o] >= sel[e]) if o < e else (sel[o] > sel[e])
                rank = rank + ahead.astype(jnp.int32)
        on = jnp.logical_and(best_g == g, rank < 2)
        picked.append(jnp.where(on, scores[e], 0.0))
    den = picked[0]
    for e in range(1, N_EXPERTS):
        den = den + picked[e]
    return [w / den for w in picked]


def _outproj_kernel(rbias_ref, a_ref, s_ref, x_ref, wo_ref, g_ref, b_ref, rwh_ref, rwl_ref,
                    x1_ref, gates_ref):
    mix = (jnp.dot(a_ref[...].astype(BF16), wo_ref[0:ATT_WIDTH, :], preferred_element_type=F32)
           + jnp.dot(s_ref[...].astype(BF16), wo_ref[ATT_WIDTH:, :], preferred_element_type=F32))
    x1 = _layer_norm(DN_ALPHA * x_ref[...] + mix, g_ref[...], b_ref[...])
    x1_ref[...] = x1
    xh = x1.astype(BF16)
    xl = (x1 - xh.astype(F32)).astype(BF16)
    logits = (jnp.dot(xh, rwh_ref[...], preferred_element_type=F32)
              + jnp.dot(xl, rwh_ref[...], preferred_element_type=F32)
              + jnp.dot(xh, rwl_ref[...], preferred_element_type=F32))
    lt = logits.T
    gate_rows = _route_rows([lt[e:e + 1, :] for e in range(N_EXPERTS)], rbias_ref)
    tm = lt.shape[1]
    r_i = lax.broadcasted_iota(jnp.int32, (N_EXPERTS, tm), 0)
    gt = jnp.zeros((N_EXPERTS, tm), F32)
    for e in range(N_EXPERTS):
        gt = jnp.where(r_i == e, jnp.broadcast_to(gate_rows[e], (N_EXPERTS, tm)), gt)
    gt = jnp.concatenate([gt, jnp.zeros((LANES - N_EXPERTS, tm), F32)], axis=0)
    gates_ref[...] = gt.T


def _out_proj(a2d, s2d, x2d, wo_bf16, ln_g, ln_b, rw_hi, rw_lo, router_bias):
    t = x2d.shape[0]
    tm = min(512, t)
    row = lambda w: pl.BlockSpec((tm, w), lambda i: (i, 0))
    const = lambda shape: pl.BlockSpec(shape, lambda i: (0, 0))
    return pl.pallas_call(
        _outproj_kernel,
        out_shape=(jax.ShapeDtypeStruct((t, D_MODEL), F32), jax.ShapeDtypeStruct((t, LANES), F32)),
        grid=(t // tm,),
        in_specs=[pl.BlockSpec(memory_space=pltpu.SMEM), row(ATT_WIDTH), row(SSM_WIDTH), row(D_MODEL),
                  const((D_MODEL, D_MODEL)), const((1, D_MODEL)), const((1, D_MODEL)),
                  const((D_MODEL, LANES)), const((D_MODEL, LANES))],
        out_specs=(row(D_MODEL), row(LANES)), compiler_params=_cparams("parallel"), name="out_proj_router",
    )(router_bias, a2d, s2d, x2d, wo_bf16, ln_g.reshape(1, -1), ln_b.reshape(1, -1), rw_hi, rw_lo)


def _moe_kernel(x_ref, gates_ref, wg_ref, wu_ref, wd_ref, g_ref, b_ref, o_ref, xb_scr, acc_scr):
    e = pl.program_id(1)

    @pl.when(e == 0)
    def _():
        xb_scr[...] = x_ref[...].astype(BF16)
        acc_scr[...] = jnp.zeros(acc_scr.shape, F32)

    xb = xb_scr[...]
    hg = jnp.dot(xb, wg_ref[...], preferred_element_type=F32)
    hu = jnp.dot(xb, wu_ref[...], preferred_element_type=F32)
    h = jax.nn.silu(hg) * hu
    y = jnp.dot(h.astype(BF16), wd_ref[...], preferred_element_type=F32)
    gates = gates_ref[...]
    lane = lax.broadcasted_iota(jnp.int32, gates.shape, 1)
    gate = jnp.sum(jnp.where(lane == e, gates, 0.0), axis=-1, keepdims=True)
    acc_scr[...] = acc_scr[...] + gate * y

    @pl.when(e == pl.num_programs(1) - 1)
    def _():
        o_ref[...] = _layer_norm(DN_ALPHA * x_ref[...] + acc_scr[...], g_ref[...], b_ref[...])


def _moe(x2d, gates, wg, wu, wd, ln_g, ln_b):
    t = x2d.shape[0]
    tm = min(512, t)
    row = lambda w: pl.BlockSpec((tm, w), lambda i, e: (i, 0))
    const = lambda shape: pl.BlockSpec(shape, lambda i, e: (0, 0))
    return pl.pallas_call(
        _moe_kernel, out_shape=jax.ShapeDtypeStruct((t, D_MODEL), F32), grid=(t // tm, N_EXPERTS),
        in_specs=[row(D_MODEL), row(LANES),
                  pl.BlockSpec((None, D_MODEL, D_EXPERT), lambda i, e: (e, 0, 0)),
                  pl.BlockSpec((None, D_MODEL, D_EXPERT), lambda i, e: (e, 0, 0)),
                  pl.BlockSpec((None, D_EXPERT, D_MODEL), lambda i, e: (e, 0, 0)),
                  const((1, D_MODEL)), const((1, D_MODEL))],
        out_specs=row(D_MODEL),
        scratch_shapes=[pltpu.VMEM((tm, D_MODEL), BF16), pltpu.VMEM((tm, D_MODEL), F32)],
        compiler_params=_cparams("parallel", "arbitrary"), name="moe_ln",
    )(x2d, gates, wg, wu, wd, ln_g.reshape(1, -1), ln_b.reshape(1, -1))


def kernel(x_prompt, x_sample, cache_k, cache_v, state_ssm_re, state_ssm_im, page_table, w_in, w_out, ln1_g, ln1_b, ln2_g, ln2_b, lambda_q1, lambda_k1, lambda_q2, lambda_k2, subln_g, ssm_lambda_re, ssm_lambda_im, ssm_log_dt, ssm_b_re, ssm_b_im, ssm_c_re, ssm_c_im, ssm_d, ssm_w_glu, ssm_b_glu, router_w, router_bias, moe_w_gate, moe_w_up, moe_w_down):
    b, t, _ = x_prompt.shape
    n, dec_seq, _ = x_sample.shape
    depth = w_in.shape[0]
    n_pool = cache_k.shape[1]

    slopes = 2.0 ** (-8.0 * jnp.arange(1, N_HEADS + 1, dtype=F32) / N_HEADS)
    col = jnp.arange(LANES)
    cslope = jnp.where(col < 2 * N_HEADS * dec_seq, slopes[jnp.minimum(col // (2 * dec_seq), N_HEADS - 1)], 0.0)
    cslope = cslope.reshape(1, LANES).astype(F32)
    rw_pad = jnp.zeros((D_MODEL, LANES), F32).at[:, :N_EXPERTS].set(router_w)
    rw_hi = rw_pad.astype(BF16)
    rw_lo = (rw_pad - rw_hi.astype(F32)).astype(BF16)

    yp = x_prompt.reshape(b * t, D_MODEL)
    ys = x_sample.reshape(n * dec_seq, D_MODEL)
    outs = [[] for _ in range(8)]
    for l in range(depth):
        lam_init = 0.8 - 0.6 * math.exp(-0.3 * l)
        a_re, a_im, bb_re, bb_im, lam = _layer_prep(
            ssm_lambda_re[l], ssm_lambda_im[l], ssm_log_dt[l], ssm_b_re[l], ssm_b_im[l],
            lambda_q1[l], lambda_k1[l], lambda_q2[l], lambda_k2[l], lam_init)
        a_re, a_im = a_re.reshape(1, N_STATE), a_im.reshape(1, N_STATE)
        bbd = jnp.concatenate([_block_diag_in(bb_re), _block_diag_in(bb_im)], axis=1).astype(BF16)
        cbd = jnp.concatenate([_block_diag_out(ssm_c_re[l]), -_block_diag_out(ssm_c_im[l])], axis=0).astype(BF16)
        tail = (cbd, ssm_d[l].reshape(1, -1), ssm_w_glu[l].astype(BF16), ssm_b_glu[l].reshape(1, -1))
        w_in_b, w_out_b = w_in[l].astype(BF16), w_out[l].astype(BF16)
        wg, wu, wd = moe_w_gate[l].astype(BF16), moe_w_up[l].astype(BF16), moe_w_down[l].astype(BF16)

        def finish(x2d, a2d, s2d):
            x1, gates = _out_proj(a2d, s2d, x2d, w_out_b, ln1_g[l], ln1_b[l], rw_hi, rw_lo, router_bias)
            return _moe(x1, gates, wg, wu, wd, ln2_g[l], ln2_b[l])

        u, q, k, v = _in_proj(yp, w_in_b)
        a = _attn_prompt(q.reshape(b, t, -1), k.reshape(b, t, -1), v.reshape(b, t, -1),
                         slopes, lam, subln_g[l], lam_init)
        s, hr, hi = _s5_prompt(u.reshape(b, t, -1), a_re, a_im, bbd, tail)
        yp = finish(yp, a.reshape(b * t, -1), s.reshape(b * t, -1))
        outs[0].append(k.reshape(b, t, N_HEADS, V_DIM))
        outs[1].append(v.reshape(b, t, N_HEADS, V_DIM))
        outs[2].append(hr.reshape(b, N_SSM_GROUPS, SSM_STATE))
        outs[3].append(hi.reshape(b, N_SSM_GROUPS, SSM_STATE))

        u, q, k, v = _in_proj(ys, w_in_b)
        a = _attn_sample(q.reshape(n, dec_seq, -1), k.reshape(n, dec_seq, -1), v.reshape(n, dec_seq, -1),
                         cache_k[l].reshape(n_pool, PAGE_SIZE, -1), cache_v[l].reshape(n_pool, PAGE_SIZE, -1),
                         page_table, cslope, lam, subln_g[l], lam_init)
        s, hr, hi = _s5_sample(u.reshape(n, dec_seq, -1), state_ssm_re[l].reshape(n, N_STATE),
                               state_ssm_im[l].reshape(n, N_STATE), a_re, a_im, bbd, tail)
        ys = finish(ys, a.reshape(n * dec_seq, -1), s.reshape(n * dec_seq, -1))
        outs[4].append(k.reshape(n, dec_seq, N_HEADS, V_DIM))
        outs[5].append(v.reshape(n, dec_seq, N_HEADS, V_DIM))
        outs[6].append(hr.reshape(n, N_SSM_GROUPS, SSM_STATE))
        outs[7].append(hi.reshape(n, N_SSM_GROUPS, SSM_STATE))

    return (yp.reshape(b, t, D_MODEL), ys.reshape(n, dec_seq, D_MODEL)) + tuple(jnp.stack(o) for o in outs)
```

```python
import functools
import math

import jax
import jax.numpy as jnp
from jax import lax
from jax.experimental import pallas as pl
from jax.experimental.pallas import tpu as pltpu

F32 = jnp.float32
BF16 = jnp.bfloat16

D_MODEL = 1024
ATT_WIDTH = 512
SSM_WIDTH = 512
HEAD_DIM = 64
V_DIM = 128
N_HEADS = 4
SSM_GROUP = 16
N_SSM_GROUPS = 32
SSM_STATE = 64
N_STATE = N_SSM_GROUPS * SSM_STATE
N_EXPERTS = 16
EXPERTS_PER_GROUP = 4
N_EXPERT_GROUPS = 4
D_EXPERT = 512
PAGE_SIZE = 128
DEPTH = 2
DN_ALPHA = (2 * DEPTH) ** 0.25
LN_EPS = 1e-5
RMS_EPS = 1e-5
ATTN_SCALE = HEAD_DIM ** -0.5

SUBLANES = 8
LANES = 128
SCAN_ROWS = SUBLANES
PROMPT_SEGMENTS = 4
VMEM_LIMIT = 52 * 1024 * 1024


def _cparams(*sem):
    return pltpu.CompilerParams(dimension_semantics=sem, vmem_limit_bytes=VMEM_LIMIT)


def _layer_norm(z, g, b):
    mu = jnp.mean(z, -1, keepdims=True)
    var = jnp.mean(jnp.square(z - mu), -1, keepdims=True)
    return (z - mu) * lax.rsqrt(var + LN_EPS) * g + b


def _prep_kernel(lre_ref, lim_ref, ldt_ref, bre_ref, bim_ref, lq1_ref, lk1_ref, lq2_ref, lk2_ref,
                 are_ref, aim_ref, bbre_ref, bbim_ref, lam_ref, *, lam_init):
    lr, li = lre_ref[...], lim_ref[...]
    dt = jnp.exp(ldt_ref[...])
    mag = jnp.exp(lr * dt)
    a_re, a_im = mag * jnp.cos(li * dt), mag * jnp.sin(li * dt)
    den = lr * lr + li * li
    nr = a_re - 1.0
    coef_re = (nr * lr + a_im * li) / den
    coef_im = (a_im * lr - nr * li) / den
    br, bi = bre_ref[...], bim_ref[...]
    bbre_ref[...] = coef_re[..., None] * br - coef_im[..., None] * bi
    bbim_ref[...] = coef_re[..., None] * bi + coef_im[..., None] * br
    are_ref[...] = a_re
    aim_ref[...] = a_im
    s1 = jnp.sum(lq1_ref[...] * lk1_ref[...], axis=-1, keepdims=True)
    s2 = jnp.sum(lq2_ref[...] * lk2_ref[...], axis=-1, keepdims=True)
    lam_ref[...] = jnp.exp(s1) - jnp.exp(s2) + lam_init


def _layer_prep(lam_re, lam_im, log_dt, b_re, b_im, lq1, lk1, lq2, lk2, lam_init):
    g, p, c = N_SSM_GROUPS, SSM_STATE, SSM_GROUP
    out_shape = (jax.ShapeDtypeStruct((g, p), F32), jax.ShapeDtypeStruct((g, p), F32),
                 jax.ShapeDtypeStruct((g, p, c), F32), jax.ShapeDtypeStruct((g, p, c), F32),
                 jax.ShapeDtypeStruct((1, 1), F32))
    return pl.pallas_call(
        functools.partial(_prep_kernel, lam_init=lam_init), out_shape=out_shape, name="layer_prep",
    )(lam_re, lam_im, log_dt.reshape(g, 1), b_re, b_im,
      lq1.reshape(1, HEAD_DIM), lk1.reshape(1, HEAD_DIM), lq2.reshape(1, HEAD_DIM), lk2.reshape(1, HEAD_DIM))


def _block_diag_in(bb):
    eye = jnp.eye(N_SSM_GROUPS, dtype=bb.dtype)
    m = jnp.transpose(bb, (0, 2, 1))[:, :, None, :] * eye[:, None, :, None]
    return m.reshape(SSM_WIDTH, N_STATE)


def _block_diag_out(c):
    eye = jnp.eye(N_SSM_GROUPS, dtype=c.dtype)
    m = jnp.transpose(c, (0, 2, 1))[:, :, None, :] * eye[:, None, :, None]
    return m.reshape(N_STATE, SSM_WIDTH)


def _proj_kernel(x_ref, w_ref, *rest):
    xb = x_ref[...].astype(BF16)
    if len(rest) == 4:
        u_ref, q_ref, k_ref, v_ref = rest
    else:
        wvt_ref, u_ref, q_ref, k_ref, v_ref, kb_ref, vt_ref = rest
    for i, o_ref in enumerate((u_ref, q_ref, k_ref, v_ref)):
        y = jnp.dot(xb, w_ref[:, i * 512:(i + 1) * 512], preferred_element_type=F32)
        o_ref[...] = y
        if len(rest) > 4 and o_ref is k_ref:
            kb_ref[...] = y.astype(BF16)
    if len(rest) > 4:
        vt = lax.dot_general(wvt_ref[...], xb, (((1,), (1,)), ((), ())), preferred_element_type=F32)
        vt_ref[...] = vt.astype(BF16)


def _in_proj(x2d, w_bf16, wv_t=None, seq_len=None):
    t = x2d.shape[0]
    tm = min(512, t)
    out = jax.ShapeDtypeStruct((t, 512), F32)
    row = pl.BlockSpec((tm, 512), lambda i: (i, 0))
    in_specs = [pl.BlockSpec((tm, D_MODEL), lambda i: (i, 0)),
                pl.BlockSpec((D_MODEL, 4 * 512), lambda i: (0, 0))]
    out_shape, out_specs, args = (out,) * 4, (row,) * 4, (x2d, w_bf16)
    if wv_t is not None:
        nt = seq_len // tm
        in_specs.append(pl.BlockSpec((512, D_MODEL), lambda i: (0, 0)))
        out_shape += (jax.ShapeDtypeStruct((t, 512), BF16),
                      jax.ShapeDtypeStruct((t // seq_len, 512, seq_len), BF16))
        out_specs += (row, pl.BlockSpec((None, 512, tm), lambda i: (i // nt, 0, i % nt)))
        args += (wv_t,)
    return pl.pallas_call(
        _proj_kernel, out_shape=out_shape, grid=(t // tm,), in_specs=in_specs, out_specs=out_specs,
        compiler_params=_cparams("parallel"), name="in_proj",
    )(*args)


def _sub_ln(a, g, lam_init):
    return a * lax.rsqrt(jnp.mean(a * a, axis=-1, keepdims=True) + RMS_EPS) * g * (1.0 - lam_init)


ALIBI_SPLIT = 64
ATTN_KEY_BLOCK = 512
ATTN_QUERY_BLOCK = 1024


def _attn_prompt_kernel(qi_tab, ki_tab, slopes_ref, lam_ref, q_ref, kb_ref, vt_ref, kfeat_ref, vfeat_ref,
                        g_ref, o_ref, qs_scr, m_scr, acc_scr, *, tq, tk, lam_init):
    h, step = pl.program_id(1), pl.program_id(2)
    qi, ki = qi_tab[step], ki_tab[step]
    slope = slopes_ref[h]

    @pl.when(ki == 0)
    def _():
        qs = q_ref[...] * ATTN_SCALE
        lane = lax.broadcasted_iota(jnp.int32, qs.shape, 1)
        qfeat = jnp.where(lane == 0, ALIBI_SPLIT * slope, jnp.where(lane == 1, slope, 0.0)).astype(BF16)
        qs_scr[0:tq, 0:LANES] = jnp.where(lane < HEAD_DIM, qs, 0.0).astype(BF16)
        qs_scr[tq:2 * tq, 0:LANES] = jnp.where(lane >= HEAD_DIM, qs, 0.0).astype(BF16)
        qs_scr[0:tq, LANES:2 * LANES] = qfeat
        qs_scr[tq:2 * tq, LANES:2 * LANES] = qfeat
        m_scr[...] = jnp.full(m_scr.shape, -jnp.inf, F32)
        acc_scr[...] = jnp.zeros(acc_scr.shape, F32)

    def update(diag):
        kp = jnp.concatenate([kb_ref[...], kfeat_ref[...]], axis=1)
        vpt = jnp.concatenate([vt_ref[...], vfeat_ref[...]], axis=0)
        d = slope * (ki * tk - qi * tq).astype(F32)
        chunks = [c0 for c0 in range(0, 2 * tq, tk) if (c0 % tq) // tk >= diag]

        def scores(c0):
            return lax.dot_general(kp, qs_scr[c0:c0 + tk, :], (((1,), (1,)), ((), ())),
                                   preferred_element_type=F32)

        s_next = scores(chunks[0])
        for i, c0 in enumerate(chunks):
            s = s_next
            if i + 1 < len(chunks):
                s_next = scores(chunks[i + 1])
            if (c0 % tq) // tk == diag:
                key = lax.broadcasted_iota(jnp.int32, s.shape, 0)
                qry = lax.broadcasted_iota(jnp.int32, s.shape, 1)
                s = jnp.where(qry >= key, s, -jnp.inf)
            m_prev = m_scr[:, c0:c0 + tk]
            m_new = jnp.maximum(m_prev, jnp.max(s, axis=0, keepdims=True) + d)
            alpha = jnp.exp(m_prev - m_new)
            p = jnp.exp(s - (m_new - d))
            pv = jnp.dot(vpt, p.astype(BF16), preferred_element_type=F32)
            acc_scr[:, c0:c0 + tk] = alpha * acc_scr[:, c0:c0 + tk] + pv
            m_scr[:, c0:c0 + tk] = m_new

    rel = ki - qi * (tq // tk)

    @pl.when(rel < 0)
    def _():
        update(-1)

    for j in range(tq // tk):
        @pl.when(rel == j)
        def _(j=j):
            update(j)

    @pl.when(rel == tq // tk - 1)
    def _():
        o1 = acc_scr[0:V_DIM, 0:tq] / acc_scr[V_DIM:V_DIM + 1, 0:tq]
        o2 = acc_scr[0:V_DIM, tq:2 * tq] / acc_scr[V_DIM:V_DIM + 1, tq:2 * tq]
        a = (o1 - lam_ref[0, 0] * o2).T
        o_ref[...] = _sub_ln(a, g_ref[...], lam_init)


def _attn_prompt(q, kb, vt, slopes, lam, subln_g, lam_init):
    b, t, _ = q.shape
    tk = min(ATTN_KEY_BLOCK, t)
    tq = min(ATTN_QUERY_BLOCK, t)
    nq, r = t // tq, tq // tk
    pairs = [(qi, ki) for qi in range(nq) for ki in range(r * (qi + 1))]
    qi_tab = jnp.asarray([p[0] for p in pairs], jnp.int32)
    ki_tab = jnp.asarray([p[1] for p in pairs], jnp.int32)
    c = jnp.arange(tk)[:, None]
    lane = jnp.arange(LANES)[None, :]
    kfeat = jnp.where(lane == 0, c // ALIBI_SPLIT, jnp.where(lane == 1, c % ALIBI_SPLIT, 0)).astype(BF16)
    vfeat = jnp.broadcast_to(jnp.where(lane.T == 0, 1.0, 0.0), (LANES, tk)).astype(BF16)
    kernel = functools.partial(_attn_prompt_kernel, tq=tq, tk=tk, lam_init=lam_init)
    smem = pl.BlockSpec(memory_space=pltpu.SMEM)
    q_spec = pl.BlockSpec((None, tq, V_DIM), lambda b_, h, s, qt, kt: (b_, qt[s], h))
    k_spec = pl.BlockSpec((None, tk, V_DIM), lambda b_, h, s, qt, kt: (b_, kt[s], h))
    vt_spec = pl.BlockSpec((None, V_DIM, tk), lambda b_, h, s, qt, kt: (b_, h, kt[s]))
    const = lambda shape: pl.BlockSpec(shape, lambda b_, h, s, qt, kt: (0, 0))
    grid_spec = pltpu.PrefetchScalarGridSpec(
        num_scalar_prefetch=2, grid=(b, N_HEADS, len(pairs)),
        in_specs=[smem, smem, q_spec, k_spec, vt_spec, const((tk, LANES)), const((LANES, tk)),
                  const((1, V_DIM))],
        out_specs=q_spec,
        scratch_shapes=[pltpu.VMEM((2 * tq, 2 * LANES), BF16), pltpu.VMEM((1, 2 * tq), F32),
                        pltpu.VMEM((2 * LANES, 2 * tq), F32)])
    return pl.pallas_call(
        kernel, out_shape=jax.ShapeDtypeStruct((b, t, ATT_WIDTH), F32), grid_spec=grid_spec,
        compiler_params=_cparams("parallel", "parallel", "arbitrary"), name="attn_prompt",
    )(qi_tab, ki_tab, slopes, lam, q, kb, vt, kfeat, vfeat, subln_g.reshape(1, V_DIM))


def _attn_sample_kernel(pt_ref, lam_ref, q_ref, knew_ref, vnew_ref, g_ref, cslope_ref, *rest,
                        n_pages, dec_seq, lam_init):
    del pt_ref
    k_pages = rest[:n_pages]
    v_pages = rest[n_pages:2 * n_pages]
    o_ref, s_scr = rest[2 * n_pages:]
    past = n_pages * PAGE_SIZE
    cols_per_head = 2 * dec_seq
    head_rows = lambda ref, h: ref[pl.ds(h, PAGE_SIZE, stride=N_HEADS), :]

    q = q_ref[...] * ATTN_SCALE
    lane = lax.broadcasted_iota(jnp.int32, (dec_seq, V_DIM), 1)
    qw = []
    for h in range(N_HEADS):
        qh = q[:, h * V_DIM:(h + 1) * V_DIM]
        blocks = [jnp.where(lane < HEAD_DIM, qh, 0.0), jnp.where(lane >= HEAD_DIM, qh, 0.0)]
        if h:
            blocks.insert(0, jnp.zeros((h * cols_per_head, V_DIM), F32))
        blocks.append(jnp.zeros((LANES - (h + 1) * cols_per_head, V_DIM), F32))
        qw.append(jnp.concatenate(blocks, axis=0))
    dn = (((1,), (1,)), ((), ()))

    s_past = None
    for h0 in range(0, N_HEADS, 2):
        keys = jnp.concatenate(
            [jnp.concatenate([head_rows(k_pages[p], h0), head_rows(k_pages[p], h0 + 1)], axis=1)
             for p in range(n_pages)], axis=0)
        w = jnp.concatenate([qw[h0], qw[h0 + 1]], axis=1)
        part = lax.dot_general(keys, w, dn, preferred_element_type=F32)
        s_past = part if s_past is None else s_past + part
    row = lax.broadcasted_iota(jnp.int32, (past, LANES), 0)
    col = lax.broadcasted_iota(jnp.int32, (past, LANES), 1)
    cslope = cslope_ref[...]
    dist = past + col % dec_seq - row
    s_scr[0:past, :] = s_past - cslope * dist.astype(F32)

    knew = knew_ref[...]
    s_new = None
    for h in range(N_HEADS):
        part = lax.dot_general(knew[:, h * V_DIM:(h + 1) * V_DIM], qw[h], dn, preferred_element_type=F32)
        s_new = part if s_new is None else s_new + part
    row = lax.broadcasted_iota(jnp.int32, (dec_seq, LANES), 0)
    col = lax.broadcasted_iota(jnp.int32, (dec_seq, LANES), 1)
    dist = col % dec_seq - row
    s_scr[past:past + dec_seq, :] = jnp.where(dist >= 0, s_new - cslope * dist.astype(F32), -jnp.inf)
    s_scr[past + dec_seq:past + PAGE_SIZE, :] = jnp.full((PAGE_SIZE - dec_seq, LANES), -jnp.inf, F32)

    s_all = s_scr[...]
    m = jnp.max(s_all, axis=0, keepdims=True)
    p_all = jnp.exp(s_all - m)
    l = jnp.sum(p_all, axis=0, keepdims=True)
    c_i = lax.broadcasted_iota(jnp.int32, (1, LANES), 1)
    sign = jnp.where((c_i // dec_seq) % 2 == 0, 1.0, -lam_ref[0, 0])
    s_scr[...] = p_all * (sign / l)

    outs = [jnp.zeros((cols_per_head, V_DIM), F32) for _ in range(N_HEADS)]
    vnew = vnew_ref[...]
    vzero = jnp.zeros((PAGE_SIZE - dec_seq, V_DIM), F32)
    for p in range(n_pages + 1):
        pt = s_scr[p * PAGE_SIZE:(p + 1) * PAGE_SIZE, :].T
        for h in range(N_HEADS):
            if p < n_pages:
                vh = head_rows(v_pages[p], h)
            else:
                vh = jnp.concatenate([vnew[:, h * V_DIM:(h + 1) * V_DIM], vzero], axis=0)
            outs[h] = outs[h] + jnp.dot(pt[h * cols_per_head:(h + 1) * cols_per_head, :], vh,
                                        preferred_element_type=F32)
    for h in range(N_HEADS):
        a = outs[h][0:dec_seq, :] + outs[h][dec_seq:2 * dec_seq, :]
        o_ref[:, h * V_DIM:(h + 1) * V_DIM] = _sub_ln(a, g_ref[...], lam_init)


def _attn_sample(q, k, v, k_pages, v_pages, page_base, page_table, cslope, lam, subln_g, lam_init):
    n, dec_seq, _ = q.shape
    n_pages = page_table.shape[1]
    assert 2 * N_HEADS * dec_seq <= LANES and dec_seq % SUBLANES == 0 and N_HEADS % 2 == 0
    kernel = functools.partial(_attn_sample_kernel, n_pages=n_pages, dec_seq=dec_seq, lam_init=lam_init)
    tok = pl.BlockSpec((None, dec_seq, ATT_WIDTH), lambda i, pt: (i, 0, 0))

    def page_spec(p):
        return pl.BlockSpec((None, PAGE_SIZE * N_HEADS, V_DIM),
                            lambda i, pt: (page_base + pt[i * n_pages + p], 0, 0))

    grid_spec = pltpu.PrefetchScalarGridSpec(
        num_scalar_prefetch=1, grid=(n,),
        in_specs=[pl.BlockSpec(memory_space=pltpu.SMEM), tok, tok, tok,
                  pl.BlockSpec((1, V_DIM), lambda i, pt: (0, 0)),
                  pl.BlockSpec((1, LANES), lambda i, pt: (0, 0))]
                 + [page_spec(p) for p in range(n_pages)] * 2,
        out_specs=tok,
        scratch_shapes=[pltpu.VMEM(((n_pages + 1) * PAGE_SIZE, LANES), F32)])
    return pl.pallas_call(
        kernel, out_shape=jax.ShapeDtypeStruct((n, dec_seq, ATT_WIDTH), F32), grid_spec=grid_spec,
        compiler_params=_cparams("arbitrary"), name="attn_sample",
    )(page_table.reshape(-1), lam, q, k, v, subln_g.reshape(1, V_DIM), cslope,
      *([k_pages] * n_pages), *([v_pages] * n_pages))


def _s5_scan_chunk(bu_scr, hs_scr, hre_scr, him_scr, are_ref, aim_ref, steps, lane_chunk):
    for c in range(N_STATE // lane_chunk):
        lo = c * lane_chunk
        are = jnp.broadcast_to(are_ref[:, lo:lo + lane_chunk], (SCAN_ROWS, lane_chunk))
        aim = jnp.broadcast_to(aim_ref[:, lo:lo + lane_chunk], (SCAN_ROWS, lane_chunk))

        def body(j, carry, lo=lo, are=are, aim=aim):
            hre, him = carry
            r0 = pl.multiple_of(j * SCAN_ROWS, SCAN_ROWS)
            bre = bu_scr[pl.ds(r0, SCAN_ROWS), lo:lo + lane_chunk]
            bim = bu_scr[pl.ds(r0, SCAN_ROWS), N_STATE + lo:N_STATE + lo + lane_chunk]
            nre = are * hre - aim * him + bre
            nim = are * him + aim * hre + bim
            if hs_scr is not None:
                hs_scr[pl.ds(r0, SCAN_ROWS), lo:lo + lane_chunk] = nre
                hs_scr[pl.ds(r0, SCAN_ROWS), N_STATE + lo:N_STATE + lo + lane_chunk] = nim
            return nre, nim

        hre, him = lax.fori_loop(0, steps, body,
                                 (hre_scr[:, lo:lo + lane_chunk], him_scr[:, lo:lo + lane_chunk]),
                                 unroll=min(4, steps))
        hre_scr[:, lo:lo + lane_chunk] = hre
        him_scr[:, lo:lo + lane_chunk] = him


def _s5_kernel(u_ref, h0re_ref, h0im_ref, are_ref, aim_ref, bbd_ref, *rest, steps, with_y):
    if with_y:
        cbd_ref, d_ref, wglu_ref, bglu_ref, s_ref, hre_ref, him_ref, hre_scr, him_scr, bu_scr, hs_scr = rest
    else:
        hre_ref, him_ref, hre_scr, him_scr, bu_scr = rest
        hs_scr = None
    c = pl.program_id(1)

    @pl.when(c == 0)
    def _():
        hre_scr[...] = h0re_ref[...]
        him_scr[...] = h0im_ref[...]

    u = u_ref[...]
    bu_scr[...] = jnp.dot(u.astype(BF16), bbd_ref[...], preferred_element_type=F32)
    _s5_scan_chunk(bu_scr, hs_scr, hre_scr, him_scr, are_ref, aim_ref, steps, lane_chunk=1024)
    if with_y:
        y = jnp.dot(hs_scr[...].astype(BF16), cbd_ref[...], preferred_element_type=F32) + d_ref[...] * u
        g = jax.nn.gelu(y)
        gate = jnp.dot(g.astype(BF16), wglu_ref[...], preferred_element_type=F32) + bglu_ref[...]
        s_ref[...] = g * jax.nn.sigmoid(gate)

    @pl.when(c == pl.num_programs(1) - 1)
    def _():
        hre_ref[...] = hre_scr[...]
        him_ref[...] = him_scr[...]


def _s5_call(u3, h0re, h0im, a_re, a_im, bbd, tail, steps_per_chunk, with_y):
    ng, rows, _ = u3.shape
    j_total = rows // SCAN_ROWS
    steps = min(steps_per_chunk, j_total)
    nc = j_total // steps
    blk = steps * SCAN_ROWS
    const = lambda shape: pl.BlockSpec(shape, lambda g, c: (0,) * len(shape))
    st_spec = pl.BlockSpec((None, SCAN_ROWS, N_STATE), lambda g, c: (g, 0, 0))
    u_spec = pl.BlockSpec((None, blk, SSM_WIDTH), lambda g, c: (g, c, 0))
    st_shape = jax.ShapeDtypeStruct((ng, SCAN_ROWS, N_STATE), F32)
    in_specs = [u_spec, st_spec, st_spec, const((1, N_STATE)), const((1, N_STATE)),
                const((SSM_WIDTH, 2 * N_STATE))]
    scratch = [pltpu.VMEM((SCAN_ROWS, N_STATE), F32), pltpu.VMEM((SCAN_ROWS, N_STATE), F32),
               pltpu.VMEM((blk, 2 * N_STATE), F32)]
    if with_y:
        in_specs += [const((2 * N_STATE, SSM_WIDTH)), const((1, SSM_WIDTH)),
                     const((SSM_WIDTH, SSM_WIDTH)), const((1, SSM_WIDTH))]
        out_shape = (jax.ShapeDtypeStruct((ng, rows, SSM_WIDTH), F32), st_shape, st_shape)
        out_specs = (u_spec, st_spec, st_spec)
        scratch.append(pltpu.VMEM((blk, 2 * N_STATE), F32))
    else:
        out_shape = (st_shape, st_shape)
        out_specs = (st_spec, st_spec)
    return pl.pallas_call(
        functools.partial(_s5_kernel, steps=steps, with_y=with_y),
        out_shape=out_shape, grid=(ng, nc), in_specs=in_specs, out_specs=out_specs,
        scratch_shapes=scratch, compiler_params=_cparams("parallel", "arbitrary"),
        name="s5_scan_glu" if with_y else "s5_scan_state",
    )(u3, h0re, h0im, a_re, a_im, bbd, *tail)


def _s5_seg_init_kernel(fre_ref, fim_ref, are_ref, aim_ref, ire_ref, iim_ref, *, seg_len, n_seq, n_seg):
    pr, pi = are_ref[...], aim_ref[...]
    ar, ai = None, None
    e = seg_len
    while e:
        if e & 1:
            ar, ai = (pr, pi) if ar is None else (ar * pr - ai * pi, ar * pi + ai * pr)
        e >>= 1
        if e:
            pr, pi = pr * pr - pi * pi, 2.0 * pr * pi
    for n in range(n_seq):
        hr = jnp.zeros((1, N_STATE), F32)
        hi = jnp.zeros((1, N_STATE), F32)
        for s in range(n_seg):
            r = n * n_seg + s
            ire_ref[r:r + 1, :] = hr
            iim_ref[r:r + 1, :] = hi
            fr, fi = fre_ref[r:r + 1, :], fim_ref[r:r + 1, :]
            hr, hi = ar * hr - ai * hi + fr, ar * hi + ai * hr + fi


def _s5_seg_init(f_re, f_im, a_re, a_im, seg_len, n_seq, n_seg):
    shape = jax.ShapeDtypeStruct(f_re.shape, F32)
    return pl.pallas_call(
        functools.partial(_s5_seg_init_kernel, seg_len=seg_len, n_seq=n_seq, n_seg=n_seg),
        out_shape=(shape, shape), name="s5_seg_init",
    )(f_re, f_im, a_re, a_im)


def _s5_prompt(u, a_re, a_im, bbd, tail):
    b, t, _ = u.shape
    n_seg = SCAN_ROWS // b
    assert b * n_seg == SCAN_ROWS and t % n_seg == 0
    j = t // n_seg
    u3 = u.reshape(b, n_seg, j, SSM_WIDTH).transpose(2, 0, 1, 3).reshape(1, j * SCAN_ROWS, SSM_WIDTH)
    zeros = jnp.zeros((1, SCAN_ROWS, N_STATE), F32)
    f_re, f_im = _s5_call(u3, zeros, zeros, a_re, a_im, bbd, (), 64, with_y=False)
    i_re, i_im = _s5_seg_init(f_re[0], f_im[0], a_re, a_im, j, b, n_seg)
    s3, h_re, h_im = _s5_call(u3, i_re[None], i_im[None], a_re, a_im, bbd, tail, 64, with_y=True)
    s = s3.reshape(j, b, n_seg, SSM_WIDTH).transpose(1, 2, 0, 3).reshape(b, t, SSM_WIDTH)
    last = h_re.reshape(b, n_seg, N_STATE)[:, -1], h_im.reshape(b, n_seg, N_STATE)[:, -1]
    return s, last[0], last[1]


def _s5_sample(u, h0_re, h0_im, a_re, a_im, bbd, tail):
    n, j, _ = u.shape
    ng = n // SCAN_ROWS
    u3 = u.reshape(ng, SCAN_ROWS, j, SSM_WIDTH).transpose(0, 2, 1, 3).reshape(ng, j * SCAN_ROWS, SSM_WIDTH)
    s3, h_re, h_im = _s5_call(u3, h0_re.reshape(ng, SCAN_ROWS, N_STATE), h0_im.reshape(ng, SCAN_ROWS, N_STATE),
                              a_re, a_im, bbd, tail, j, with_y=True)
    s = s3.reshape(ng, j, SCAN_ROWS, SSM_WIDTH).transpose(0, 2, 1, 3).reshape(n, j, SSM_WIDTH)
    return s, h_re.reshape(n, N_STATE), h_im.reshape(n, N_STATE)


def _top2_sum(a, b, c, d):
    m1, n1 = jnp.maximum(a, b), jnp.minimum(a, b)
    m2, n2 = jnp.maximum(c, d), jnp.minimum(c, d)
    return jnp.maximum(m1, m2) + jnp.maximum(jnp.minimum(m1, m2), jnp.maximum(n1, n2))


def _route_rows(logit_rows, rbias_ref):
    scores = [jax.nn.sigmoid(r) for r in logit_rows]
    sel = [scores[e] + rbias_ref[e] for e in range(N_EXPERTS)]
    gs = [_top2_sum(*sel[EXPERTS_PER_GROUP * g:EXPERTS_PER_GROUP * (g + 1)]) for g in range(N_EXPERT_GROUPS)]
    best_v, best_g = gs[0], jnp.zeros(gs[0].shape, jnp.int32)
    for g in range(1, N_EXPERT_GROUPS):
        upd = gs[g] > best_v
        best_g = jnp.where(upd, g, best_g)
        best_v = jnp.where(upd, gs[g], best_v)
    picked = []
    for e in range(N_EXPERTS):
        g = e // EXPERTS_PER_GROUP
        rank = jnp.zeros(sel[e].shape, jnp.int32)
        for o in range(EXPERTS_PER_GROUP * g, EXPERTS_PER_GROUP * (g + 1)):
            if o != e:
                ahead = (sel[o] >= sel[e]) if o < e else (sel[o] > sel[e])
                rank = rank + ahead.astype(jnp.int32)
        on = jnp.logical_and(best_g == g, rank < 2)
        picked.append(jnp.where(on, scores[e], 0.0))
    den = picked[0]
    for e in range(1, N_EXPERTS):
        den = den + picked[e]
    return [w / den for w in picked]


def _outproj_kernel(rbias_ref, a_ref, s_ref, x_ref, wo_ref, g_ref, b_ref, rwh_ref, rwl_ref,
                    x1_ref, gates_ref):
    mix = (jnp.dot(a_ref[...].astype(BF16), wo_ref[0:ATT_WIDTH, :], preferred_element_type=F32)
           + jnp.dot(s_ref[...].astype(BF16), wo_ref[ATT_WIDTH:, :], preferred_element_type=F32))
    x1 = _layer_norm(DN_ALPHA * x_ref[...] + mix, g_ref[...], b_ref[...])
    x1_ref[...] = x1
    xh = x1.astype(BF16)
    xl = (x1 - xh.astype(F32)).astype(BF16)
    logits = (jnp.dot(xh, rwh_ref[...], preferred_element_type=F32)
              + jnp.dot(xl, rwh_ref[...], preferred_element_type=F32)
              + jnp.dot(xh, rwl_ref[...], preferred_element_type=F32))
    lt = logits.T
    gate_rows = _route_rows([lt[e:e + 1, :] for e in range(N_EXPERTS)], rbias_ref)
    tm = lt.shape[1]
    r_i = lax.broadcasted_iota(jnp.int32, (N_EXPERTS, tm), 0)
    gt = jnp.zeros((N_EXPERTS, tm), F32)
    for e in range(N_EXPERTS):
        gt = jnp.where(r_i == e, jnp.broadcast_to(gate_rows[e], (N_EXPERTS, tm)), gt)
    gt = jnp.concatenate([gt, jnp.zeros((LANES - N_EXPERTS, tm), F32)], axis=0)
    gates_ref[...] = gt.T


def _out_proj(a2d, s2d, x2d, wo_bf16, ln_g, ln_b, rw_hi, rw_lo, router_bias):
    t = x2d.shape[0]
    tm = min(512, t)
    row = lambda w: pl.BlockSpec((tm, w), lambda i: (i, 0))
    const = lambda shape: pl.BlockSpec(shape, lambda i: (0, 0))
    return pl.pallas_call(
        _outproj_kernel,
        out_shape=(jax.ShapeDtypeStruct((t, D_MODEL), F32), jax.ShapeDtypeStruct((t, LANES), F32)),
        grid=(t // tm,),
        in_specs=[pl.BlockSpec(memory_space=pltpu.SMEM), row(ATT_WIDTH), row(SSM_WIDTH), row(D_MODEL),
                  const((D_MODEL, D_MODEL)), const((1, D_MODEL)), const((1, D_MODEL)),
                  const((D_MODEL, LANES)), const((D_MODEL, LANES))],
        out_specs=(row(D_MODEL), row(LANES)), compiler_params=_cparams("parallel"), name="out_proj_router",
    )(router_bias, a2d, s2d, x2d, wo_bf16, ln_g.reshape(1, -1), ln_b.reshape(1, -1), rw_hi, rw_lo)


def _moe_kernel(x_ref, gates_ref, wg_ref, wu_ref, wd_ref, g_ref, b_ref, o_ref, xb_scr, acc_scr):
    e = pl.program_id(1)

    @pl.when(e == 0)
    def _():
        xb_scr[...] = x_ref[...].astype(BF16)
        acc_scr[...] = jnp.zeros(acc_scr.shape, F32)

    xb = xb_scr[...]
    hg = jnp.dot(xb, wg_ref[...], preferred_element_type=F32)
    hu = jnp.dot(xb, wu_ref[...], preferred_element_type=F32)
    h = jax.nn.silu(hg) * hu
    y = jnp.dot(h.astype(BF16), wd_ref[...], preferred_element_type=F32)
    gates = gates_ref[...]
    lane = lax.broadcasted_iota(jnp.int32, gates.shape, 1)
    gate = jnp.sum(jnp.where(lane == e, gates, 0.0), axis=-1, keepdims=True)
    acc_scr[...] = acc_scr[...] + gate * y

    @pl.when(e == pl.num_programs(1) - 1)
    def _():
        o_ref[...] = _layer_norm(DN_ALPHA * x_ref[...] + acc_scr[...], g_ref[...], b_ref[...])


def _moe(x2d, gates, wg, wu, wd, ln_g, ln_b):
    t = x2d.shape[0]
    tm = min(512, t)
    row = lambda w: pl.BlockSpec((tm, w), lambda i, e: (i, 0))
    const = lambda shape: pl.BlockSpec(shape, lambda i, e: (0, 0))
    return pl.pallas_call(
        _moe_kernel, out_shape=jax.ShapeDtypeStruct((t, D_MODEL), F32), grid=(t // tm, N_EXPERTS),
        in_specs=[row(D_MODEL), row(LANES),
                  pl.BlockSpec((None, D_MODEL, D_EXPERT), lambda i, e: (e, 0, 0)),
                  pl.BlockSpec((None, D_MODEL, D_EXPERT), lambda i, e: (e, 0, 0)),
                  pl.BlockSpec((None, D_EXPERT, D_MODEL), lambda i, e: (e, 0, 0)),
                  const((1, D_MODEL)), const((1, D_MODEL))],
        out_specs=row(D_MODEL),
        scratch_shapes=[pltpu.VMEM((tm, D_MODEL), BF16), pltpu.VMEM((tm, D_MODEL), F32)],
        compiler_params=_cparams("parallel", "arbitrary"), name="moe_ln",
    )(x2d, gates, wg, wu, wd, ln_g.reshape(1, -1), ln_b.reshape(1, -1))


def kernel(x_prompt, x_sample, cache_k, cache_v, state_ssm_re, state_ssm_im, page_table, w_in, w_out, ln1_g, ln1_b, ln2_g, ln2_b, lambda_q1, lambda_k1, lambda_q2, lambda_k2, subln_g, ssm_lambda_re, ssm_lambda_im, ssm_log_dt, ssm_b_re, ssm_b_im, ssm_c_re, ssm_c_im, ssm_d, ssm_w_glu, ssm_b_glu, router_w, router_bias, moe_w_gate, moe_w_up, moe_w_down):
    b, t, _ = x_prompt.shape
    n, dec_seq, _ = x_sample.shape
    depth = w_in.shape[0]
    n_pool = cache_k.shape[1]

    slopes = 2.0 ** (-8.0 * jnp.arange(1, N_HEADS + 1, dtype=F32) / N_HEADS)
    col = jnp.arange(LANES)
    cslope = jnp.where(col < 2 * N_HEADS * dec_seq, slopes[jnp.minimum(col // (2 * dec_seq), N_HEADS - 1)], 0.0)
    cslope = cslope.reshape(1, LANES).astype(F32)
    rw_pad = jnp.zeros((D_MODEL, LANES), F32).at[:, :N_EXPERTS].set(router_w)
    rw_hi = rw_pad.astype(BF16)
    rw_lo = (rw_pad - rw_hi.astype(F32)).astype(BF16)

    pages_k = cache_k.reshape(depth * n_pool, PAGE_SIZE * N_HEADS, V_DIM)
    pages_v = cache_v.reshape(depth * n_pool, PAGE_SIZE * N_HEADS, V_DIM)

    yp = x_prompt.reshape(b * t, D_MODEL)
    ys = x_sample.reshape(n * dec_seq, D_MODEL)
    outs = [[] for _ in range(8)]
    for l in range(depth):
        lam_init = 0.8 - 0.6 * math.exp(-0.3 * l)
        a_re, a_im, bb_re, bb_im, lam = _layer_prep(
            ssm_lambda_re[l], ssm_lambda_im[l], ssm_log_dt[l], ssm_b_re[l], ssm_b_im[l],
            lambda_q1[l], lambda_k1[l], lambda_q2[l], lambda_k2[l], lam_init)
        a_re, a_im = a_re.reshape(1, N_STATE), a_im.reshape(1, N_STATE)
        bbd = jnp.concatenate([_block_diag_in(bb_re), _block_diag_in(bb_im)], axis=1).astype(BF16)
        cbd = jnp.concatenate([_block_diag_out(ssm_c_re[l]), -_block_diag_out(ssm_c_im[l])], axis=0).astype(BF16)
        tail = (cbd, ssm_d[l].reshape(1, -1), ssm_w_glu[l].astype(BF16), ssm_b_glu[l].reshape(1, -1))
        w_in_b, w_out_b = w_in[l].astype(BF16), w_out[l].astype(BF16)
        wg, wu, wd = moe_w_gate[l].astype(BF16), moe_w_up[l].astype(BF16), moe_w_down[l].astype(BF16)

        def finish(x2d, a2d, s2d):
            x1, gates = _out_proj(a2d, s2d, x2d, w_out_b, ln1_g[l], ln1_b[l], rw_hi, rw_lo, router_bias)
            return _moe(x1, gates, wg, wu, wd, ln2_g[l], ln2_b[l])

        u, q, k, v, kb, vt = _in_proj(yp, w_in_b, w_in_b[:, 3 * 512:].T, t)
        a = _attn_prompt(q.reshape(b, t, -1), kb.reshape(b, t, -1), vt, slopes, lam, subln_g[l], lam_init)
        s, hr, hi = _s5_prompt(u.reshape(b, t, -1), a_re, a_im, bbd, tail)
        yp = finish(yp, a.reshape(b * t, -1), s.reshape(b * t, -1))
        outs[0].append(k.reshape(b, t, N_HEADS, V_DIM))
        outs[1].append(v.reshape(b, t, N_HEADS, V_DIM))
        outs[2].append(hr.reshape(b, N_SSM_GROUPS, SSM_STATE))
        outs[3].append(hi.reshape(b, N_SSM_GROUPS, SSM_STATE))

        u, q, k, v = _in_proj(ys, w_in_b)
        a = _attn_sample(q.reshape(n, dec_seq, -1), k.reshape(n, dec_seq, -1), v.reshape(n, dec_seq, -1),
                         pages_k, pages_v, l * n_pool, page_table, cslope, lam, subln_g[l], lam_init)
        s, hr, hi = _s5_sample(u.reshape(n, dec_seq, -1), state_ssm_re[l].reshape(n, N_STATE),
                               state_ssm_im[l].reshape(n, N_STATE), a_re, a_im, bbd, tail)
        ys = finish(ys, a.reshape(n * dec_seq, -1), s.reshape(n * dec_seq, -1))
        outs[4].append(k.reshape(n, dec_seq, N_HEADS, V_DIM))
        outs[5].append(v.reshape(n, dec_seq, N_HEADS, V_DIM))
        outs[6].append(hr.reshape(n, N_SSM_GROUPS, SSM_STATE))
        outs[7].append(hi.reshape(n, N_SSM_GROUPS, SSM_STATE))

    return (yp.reshape(b, t, D_MODEL), ys.reshape(n, dec_seq, D_MODEL)) + tuple(jnp.stack(o) for o in outs)
```

```python
import functools
import math

import jax
import jax.numpy as jnp
from jax import lax
from jax.experimental import pallas as pl
from jax.experimental.pallas import tpu as pltpu

F32 = jnp.float32
BF16 = jnp.bfloat16

D_MODEL = 1024
ATT_WIDTH = 512
SSM_WIDTH = 512
HEAD_DIM = 64
V_DIM = 128
N_HEADS = 4
SSM_GROUP = 16
N_SSM_GROUPS = 32
SSM_STATE = 64
N_STATE = N_SSM_GROUPS * SSM_STATE
N_EXPERTS = 16
EXPERTS_PER_GROUP = 4
N_EXPERT_GROUPS = 4
D_EXPERT = 512
PAGE_SIZE = 128
DEPTH = 2
DN_ALPHA = (2 * DEPTH) ** 0.25
LN_EPS = 1e-5
RMS_EPS = 1e-5
ATTN_SCALE = HEAD_DIM ** -0.5

SUBLANES = 8
LANES = 128
SCAN_ROWS = SUBLANES
MOE_TOKEN_TILE = 1024
VMEM_LIMIT = 52 * 1024 * 1024


def _cparams(*sem):
    return pltpu.CompilerParams(dimension_semantics=sem, vmem_limit_bytes=VMEM_LIMIT)


def _layer_norm(z, g, b):
    mu = jnp.mean(z, -1, keepdims=True)
    var = jnp.mean(jnp.square(z - mu), -1, keepdims=True)
    return (z - mu) * lax.rsqrt(var + LN_EPS) * g + b


def _prep_kernel(lre_ref, lim_ref, ldt_ref, bre_ref, bim_ref, lq1_ref, lk1_ref, lq2_ref, lk2_ref,
                 are_ref, aim_ref, bbre_ref, bbim_ref, lam_ref, *, lam_init):
    lr, li = lre_ref[...], lim_ref[...]
    dt = jnp.exp(ldt_ref[...])
    mag = jnp.exp(lr * dt)
    a_re, a_im = mag * jnp.cos(li * dt), mag * jnp.sin(li * dt)
    den = lr * lr + li * li
    nr = a_re - 1.0
    coef_re = (nr * lr + a_im * li) / den
    coef_im = (a_im * lr - nr * li) / den
    br, bi = bre_ref[...], bim_ref[...]
    bbre_ref[...] = coef_re[..., None] * br - coef_im[..., None] * bi
    bbim_ref[...] = coef_re[..., None] * bi + coef_im[..., None] * br
    are_ref[...] = a_re
    aim_ref[...] = a_im
    s1 = jnp.sum(lq1_ref[...] * lk1_ref[...], axis=-1, keepdims=True)
    s2 = jnp.sum(lq2_ref[...] * lk2_ref[...], axis=-1, keepdims=True)
    lam_ref[...] = jnp.exp(s1) - jnp.exp(s2) + lam_init


def _layer_prep(lam_re, lam_im, log_dt, b_re, b_im, lq1, lk1, lq2, lk2, lam_init):
    g, p, c = N_SSM_GROUPS, SSM_STATE, SSM_GROUP
    out_shape = (jax.ShapeDtypeStruct((g, p), F32), jax.ShapeDtypeStruct((g, p), F32),
                 jax.ShapeDtypeStruct((g, p, c), F32), jax.ShapeDtypeStruct((g, p, c), F32),
                 jax.ShapeDtypeStruct((1, 1), F32))
    return pl.pallas_call(
        functools.partial(_prep_kernel, lam_init=lam_init), out_shape=out_shape, name="layer_prep",
    )(lam_re, lam_im, log_dt.reshape(g, 1), b_re, b_im,
      lq1.reshape(1, HEAD_DIM), lk1.reshape(1, HEAD_DIM), lq2.reshape(1, HEAD_DIM), lk2.reshape(1, HEAD_DIM))


def _block_diag_in(bb):
    eye = jnp.eye(N_SSM_GROUPS, dtype=bb.dtype)
    m = jnp.transpose(bb, (0, 2, 1))[:, :, None, :] * eye[:, None, :, None]
    return m.reshape(SSM_WIDTH, N_STATE)


def _block_diag_out(c):
    eye = jnp.eye(N_SSM_GROUPS, dtype=c.dtype)
    m = jnp.transpose(c, (0, 2, 1))[:, :, None, :] * eye[:, None, :, None]
    return m.reshape(N_STATE, SSM_WIDTH)


def _proj_sample_kernel(x_ref, w_ref, u_ref, q_ref, k_ref, v_ref):
    xb = x_ref[...].astype(BF16)
    for i, o_ref in enumerate((u_ref, q_ref, k_ref, v_ref)):
        o_ref[...] = jnp.dot(xb, w_ref[:, i * 512:(i + 1) * 512], preferred_element_type=F32)


def _in_proj_sample(x2d, w_bf16):
    t = x2d.shape[0]
    tm = min(512, t)
    out = jax.ShapeDtypeStruct((t, 512), F32)
    row = pl.BlockSpec((tm, 512), lambda i: (i, 0))
    return pl.pallas_call(
        _proj_sample_kernel, out_shape=(out,) * 4, grid=(t // tm,),
        in_specs=[pl.BlockSpec((tm, D_MODEL), lambda i: (i, 0)),
                  pl.BlockSpec((D_MODEL, 4 * 512), lambda i: (0, 0))],
        out_specs=(row,) * 4, compiler_params=_cparams("parallel"), name="in_proj_sample",
    )(x2d, w_bf16)


def _proj_prompt_kernel(x_ref, w_ref, wvt_ref, kbuf_in, vbuf_in, u_ref, q_ref, kb_ref, vt_ref, k_ref, v_ref):
    del kbuf_in, vbuf_in
    xb = x_ref[...].astype(BF16)
    tm = xb.shape[0]
    col = lambda i: jnp.dot(xb, w_ref[:, i * 512:(i + 1) * 512], preferred_element_type=F32)
    u_ref[...] = col(0)
    q_ref[...] = col(1)
    k, v = col(2), col(3)
    kb_ref[...] = k.astype(BF16)
    for h in range(N_HEADS):
        k_ref[pl.ds(h, tm, stride=N_HEADS), :] = k[:, h * V_DIM:(h + 1) * V_DIM]
        v_ref[pl.ds(h, tm, stride=N_HEADS), :] = v[:, h * V_DIM:(h + 1) * V_DIM]
    vt = lax.dot_general(wvt_ref[...], xb, (((1,), (1,)), ((), ())), preferred_element_type=F32)
    vt_ref[...] = vt.astype(BF16)


def _in_proj_prompt(x2d, w_bf16, wv_t, kbuf, vbuf, layer, n_seq):
    t = x2d.shape[0]
    seq_len = t // n_seq
    tm = min(512, seq_len)
    assert seq_len % tm == 0
    nt, nl = seq_len // tm, t // tm
    row = pl.BlockSpec((tm, 512), lambda i: (i, 0))
    cache_row = pl.BlockSpec((tm * N_HEADS, V_DIM), lambda i: (layer * nl + i, 0))
    anyspec = pl.BlockSpec(memory_space=pl.ANY)
    out_shape = (jax.ShapeDtypeStruct((t, 512), F32), jax.ShapeDtypeStruct((t, 512), F32),
                 jax.ShapeDtypeStruct((t, 512), BF16), jax.ShapeDtypeStruct((n_seq, 512, seq_len), BF16),
                 jax.ShapeDtypeStruct(kbuf.shape, F32), jax.ShapeDtypeStruct(vbuf.shape, F32))
    out_specs = (row, row, row,
                 pl.BlockSpec((None, 512, tm), lambda i: (i // nt, 0, i % nt)), cache_row, cache_row)
    return pl.pallas_call(
        _proj_prompt_kernel, out_shape=out_shape, grid=(nl,),
        in_specs=[pl.BlockSpec((tm, D_MODEL), lambda i: (i, 0)),
                  pl.BlockSpec((D_MODEL, 4 * 512), lambda i: (0, 0)),
                  pl.BlockSpec((512, D_MODEL), lambda i: (0, 0)), anyspec, anyspec],
        out_specs=out_specs, input_output_aliases={3: 4, 4: 5},
        compiler_params=_cparams("parallel"), name="in_proj_prompt",
    )(x2d, w_bf16, wv_t, kbuf, vbuf)


def _sub_ln(a, g, lam_init):
    return a * lax.rsqrt(jnp.mean(a * a, axis=-1, keepdims=True) + RMS_EPS) * g * (1.0 - lam_init)


ALIBI_SPLIT = 64
ATTN_KEY_BLOCK = 512
ATTN_QUERY_BLOCK = 1024


def _attn_prompt_kernel(qi_tab, ki_tab, slopes_ref, lam_ref, q_ref, kb_ref, vt_ref, kfeat_ref, vfeat_ref,
                        g_ref, o_ref, qs_scr, m_scr, acc_scr, *, tq, tk, lam_init):
    h, step = pl.program_id(1), pl.program_id(2)
    qi, ki = qi_tab[step], ki_tab[step]
    slope = slopes_ref[h]

    @pl.when(ki == 0)
    def _():
        qs = q_ref[...] * ATTN_SCALE
        lane = lax.broadcasted_iota(jnp.int32, qs.shape, 1)
        qfeat = jnp.where(lane == 0, ALIBI_SPLIT * slope, jnp.where(lane == 1, slope, 0.0)).astype(BF16)
        qs_scr[0:tq, 0:LANES] = jnp.where(lane < HEAD_DIM, qs, 0.0).astype(BF16)
        qs_scr[tq:2 * tq, 0:LANES] = jnp.where(lane >= HEAD_DIM, qs, 0.0).astype(BF16)
        qs_scr[0:tq, LANES:2 * LANES] = qfeat
        qs_scr[tq:2 * tq, LANES:2 * LANES] = qfeat
        m_scr[...] = jnp.full(m_scr.shape, -jnp.inf, F32)
        acc_scr[...] = jnp.zeros(acc_scr.shape, F32)

    def update(diag):
        kp = jnp.concatenate([kb_ref[...], kfeat_ref[...]], axis=1)
        vpt = jnp.concatenate([vt_ref[...], vfeat_ref[...]], axis=0)
        d = slope * (ki * tk - qi * tq).astype(F32)
        chunks = [c0 for c0 in range(0, 2 * tq, tk) if (c0 % tq) // tk >= diag]

        def scores(c0):
            return lax.dot_general(kp, qs_scr[c0:c0 + tk, :], (((1,), (1,)), ((), ())),
                                   preferred_element_type=F32)

        s_next = scores(chunks[0])
        for i, c0 in enumerate(chunks):
            s = s_next
            if i + 1 < len(chunks):
                s_next = scores(chunks[i + 1])
            if (c0 % tq) // tk == diag:
                key = lax.broadcasted_iota(jnp.int32, s.shape, 0)
                qry = lax.broadcasted_iota(jnp.int32, s.shape, 1)
                s = jnp.where(qry >= key, s, -jnp.inf)
            m_prev = m_scr[:, c0:c0 + tk]
            m_new = jnp.maximum(m_prev, jnp.max(s, axis=0, keepdims=True) + d)
            alpha = jnp.exp(m_prev - m_new)
            p = jnp.exp(s - (m_new - d))
            pv = jnp.dot(vpt, p.astype(BF16), preferred_element_type=F32)
            acc_scr[:, c0:c0 + tk] = alpha * acc_scr[:, c0:c0 + tk] + pv
            m_scr[:, c0:c0 + tk] = m_new

    rel = ki - qi * (tq // tk)

    @pl.when(rel < 0)
    def _():
        update(-1)

    for j in range(tq // tk):
        @pl.when(rel == j)
        def _(j=j):
            update(j)

    @pl.when(rel == tq // tk - 1)
    def _():
        o1 = acc_scr[0:V_DIM, 0:tq] / acc_scr[V_DIM:V_DIM + 1, 0:tq]
        o2 = acc_scr[0:V_DIM, tq:2 * tq] / acc_scr[V_DIM:V_DIM + 1, tq:2 * tq]
        a = (o1 - lam_ref[0, 0] * o2).T
        o_ref[...] = _sub_ln(a, g_ref[...], lam_init)


def _attn_prompt(q, kb, vt, slopes, lam, subln_g, lam_init):
    b, t, _ = q.shape
    tk = min(ATTN_KEY_BLOCK, t)
    tq = min(ATTN_QUERY_BLOCK, t)
    nq, r = t // tq, tq // tk
    pairs = [(qi, ki) for qi in range(nq) for ki in range(r * (qi + 1))]
    qi_tab = jnp.asarray([p[0] for p in pairs], jnp.int32)
    ki_tab = jnp.asarray([p[1] for p in pairs], jnp.int32)
    c = jnp.arange(tk)[:, None]
    lane = jnp.arange(LANES)[None, :]
    kfeat = jnp.where(lane == 0, c // ALIBI_SPLIT, jnp.where(lane == 1, c % ALIBI_SPLIT, 0)).astype(BF16)
    vfeat = jnp.broadcast_to(jnp.where(lane.T == 0, 1.0, 0.0), (LANES, tk)).astype(BF16)
    kernel = functools.partial(_attn_prompt_kernel, tq=tq, tk=tk, lam_init=lam_init)
    smem = pl.BlockSpec(memory_space=pltpu.SMEM)
    q_spec = pl.BlockSpec((None, tq, V_DIM), lambda b_, h, s, qt, kt: (b_, qt[s], h))
    k_spec = pl.BlockSpec((None, tk, V_DIM), lambda b_, h, s, qt, kt: (b_, kt[s], h))
    vt_spec = pl.BlockSpec((None, V_DIM, tk), lambda b_, h, s, qt, kt: (b_, h, kt[s]))
    const = lambda shape: pl.BlockSpec(shape, lambda b_, h, s, qt, kt: (0, 0))
    grid_spec = pltpu.PrefetchScalarGridSpec(
        num_scalar_prefetch=2, grid=(b, N_HEADS, len(pairs)),
        in_specs=[smem, smem, q_spec, k_spec, vt_spec, const((tk, LANES)), const((LANES, tk)),
                  const((1, V_DIM))],
        out_specs=q_spec,
        scratch_shapes=[pltpu.VMEM((2 * tq, 2 * LANES), BF16), pltpu.VMEM((1, 2 * tq), F32),
                        pltpu.VMEM((2 * LANES, 2 * tq), F32)])
    return pl.pallas_call(
        kernel, out_shape=jax.ShapeDtypeStruct((b, t, ATT_WIDTH), F32), grid_spec=grid_spec,
        compiler_params=_cparams("parallel", "parallel", "arbitrary"), name="attn_prompt",
    )(qi_tab, ki_tab, slopes, lam, q, kb, vt, kfeat, vfeat, subln_g.reshape(1, V_DIM))


def _attn_sample_kernel(pt_ref, lam_ref, q_ref, knew_ref, vnew_ref, g_ref, cslope_ref, *rest,
                        n_pages, dec_seq, lam_init):
    del pt_ref
    k_pages = rest[:n_pages]
    v_pages = rest[n_pages:2 * n_pages]
    o_ref, s_scr = rest[2 * n_pages:]
    past = n_pages * PAGE_SIZE
    cols_per_head = 2 * dec_seq
    head_rows = lambda ref, h: ref[pl.ds(h, PAGE_SIZE, stride=N_HEADS), :]

    q = q_ref[...] * ATTN_SCALE
    lane = lax.broadcasted_iota(jnp.int32, (dec_seq, V_DIM), 1)
    qw = []
    for h in range(N_HEADS):
        qh = q[:, h * V_DIM:(h + 1) * V_DIM]
        blocks = [jnp.where(lane < HEAD_DIM, qh, 0.0), jnp.where(lane >= HEAD_DIM, qh, 0.0)]
        if h:
            blocks.insert(0, jnp.zeros((h * cols_per_head, V_DIM), F32))
        blocks.append(jnp.zeros((LANES - (h + 1) * cols_per_head, V_DIM), F32))
        qw.append(jnp.concatenate(blocks, axis=0))
    dn = (((1,), (1,)), ((), ()))

    s_past = None
    for h0 in range(0, N_HEADS, 2):
        keys = jnp.concatenate(
            [jnp.concatenate([head_rows(k_pages[p], h0), head_rows(k_pages[p], h0 + 1)], axis=1)
             for p in range(n_pages)], axis=0)
        w = jnp.concatenate([qw[h0], qw[h0 + 1]], axis=1)
        part = lax.dot_general(keys, w, dn, preferred_element_type=F32)
        s_past = part if s_past is None else s_past + part
    row = lax.broadcasted_iota(jnp.int32, (past, LANES), 0)
    col = lax.broadcasted_iota(jnp.int32, (past, LANES), 1)
    cslope = cslope_ref[...]
    dist = past + col % dec_seq - row
    s_scr[0:past, :] = s_past - cslope * dist.astype(F32)

    knew = knew_ref[...]
    s_new = None
    for h in range(N_HEADS):
        part = lax.dot_general(knew[:, h * V_DIM:(h + 1) * V_DIM], qw[h], dn, preferred_element_type=F32)
        s_new = part if s_new is None else s_new + part
    row = lax.broadcasted_iota(jnp.int32, (dec_seq, LANES), 0)
    col = lax.broadcasted_iota(jnp.int32, (dec_seq, LANES), 1)
    dist = col % dec_seq - row
    s_scr[past:past + dec_seq, :] = jnp.where(dist >= 0, s_new - cslope * dist.astype(F32), -jnp.inf)
    s_scr[past + dec_seq:past + PAGE_SIZE, :] = jnp.full((PAGE_SIZE - dec_seq, LANES), -jnp.inf, F32)

    s_all = s_scr[...]
    m = jnp.max(s_all, axis=0, keepdims=True)
    p_all = jnp.exp(s_all - m)
    l = jnp.sum(p_all, axis=0, keepdims=True)
    c_i = lax.broadcasted_iota(jnp.int32, (1, LANES), 1)
    sign = jnp.where((c_i // dec_seq) % 2 == 0, 1.0, -lam_ref[0, 0])
    s_scr[...] = p_all * (sign / l)

    outs = [jnp.zeros((cols_per_head, V_DIM), F32) for _ in range(N_HEADS)]
    vnew = vnew_ref[...]
    vzero = jnp.zeros((PAGE_SIZE - dec_seq, V_DIM), F32)
    for p in range(n_pages + 1):
        pt = s_scr[p * PAGE_SIZE:(p + 1) * PAGE_SIZE, :].T
        for h in range(N_HEADS):
            if p < n_pages:
                vh = head_rows(v_pages[p], h)
            else:
                vh = jnp.concatenate([vnew[:, h * V_DIM:(h + 1) * V_DIM], vzero], axis=0)
            outs[h] = outs[h] + jnp.dot(pt[h * cols_per_head:(h + 1) * cols_per_head, :], vh,
                                        preferred_element_type=F32)
    for h in range(N_HEADS):
        a = outs[h][0:dec_seq, :] + outs[h][dec_seq:2 * dec_seq, :]
        o_ref[:, h * V_DIM:(h + 1) * V_DIM] = _sub_ln(a, g_ref[...], lam_init)


def _attn_sample(q, k, v, k_pages, v_pages, page_base, page_table, cslope, lam, subln_g, lam_init):
    n, dec_seq, _ = q.shape
    n_pages = page_table.shape[1]
    assert 2 * N_HEADS * dec_seq <= LANES and dec_seq % SUBLANES == 0 and N_HEADS % 2 == 0
    kernel = functools.partial(_attn_sample_kernel, n_pages=n_pages, dec_seq=dec_seq, lam_init=lam_init)
    tok = pl.BlockSpec((None, dec_seq, ATT_WIDTH), lambda i, pt: (i, 0, 0))

    def page_spec(p):
        return pl.BlockSpec((None, PAGE_SIZE * N_HEADS, V_DIM),
                            lambda i, pt: (page_base + pt[i * n_pages + p], 0, 0))

    grid_spec = pltpu.PrefetchScalarGridSpec(
        num_scalar_prefetch=1, grid=(n,),
        in_specs=[pl.BlockSpec(memory_space=pltpu.SMEM), tok, tok, tok,
                  pl.BlockSpec((1, V_DIM), lambda i, pt: (0, 0)),
                  pl.BlockSpec((1, LANES), lambda i, pt: (0, 0))]
                 + [page_spec(p) for p in range(n_pages)] * 2,
        out_specs=tok,
        scratch_shapes=[pltpu.VMEM(((n_pages + 1) * PAGE_SIZE, LANES), F32)])
    return pl.pallas_call(
        kernel, out_shape=jax.ShapeDtypeStruct((n, dec_seq, ATT_WIDTH), F32), grid_spec=grid_spec,
        compiler_params=_cparams("arbitrary"), name="attn_sample",
    )(page_table.reshape(-1), lam, q, k, v, subln_g.reshape(1, V_DIM), cslope,
      *([k_pages] * n_pages), *([v_pages] * n_pages))


def _s5_scan_chunk(bu_scr, hs_scr, hre_scr, him_scr, are_ref, aim_ref, steps, lane_chunk):
    for c in range(N_STATE // lane_chunk):
        lo = c * lane_chunk
        are = jnp.broadcast_to(are_ref[:, lo:lo + lane_chunk], (SCAN_ROWS, lane_chunk))
        aim = jnp.broadcast_to(aim_ref[:, lo:lo + lane_chunk], (SCAN_ROWS, lane_chunk))

        def body(j, carry, lo=lo, are=are, aim=aim):
            hre, him = carry
            r0 = pl.multiple_of(j * SCAN_ROWS, SCAN_ROWS)
            bre = bu_scr[pl.ds(r0, SCAN_ROWS), lo:lo + lane_chunk]
            bim = bu_scr[pl.ds(r0, SCAN_ROWS), N_STATE + lo:N_STATE + lo + lane_chunk]
            nre = are * hre - aim * him + bre
            nim = are * him + aim * hre + bim
            if hs_scr is not None:
                hs_scr[pl.ds(r0, SCAN_ROWS), lo:lo + lane_chunk] = nre
                hs_scr[pl.ds(r0, SCAN_ROWS), N_STATE + lo:N_STATE + lo + lane_chunk] = nim
            return nre, nim

        hre, him = lax.fori_loop(0, steps, body,
                                 (hre_scr[:, lo:lo + lane_chunk], him_scr[:, lo:lo + lane_chunk]),
                                 unroll=min(4, steps))
        hre_scr[:, lo:lo + lane_chunk] = hre
        him_scr[:, lo:lo + lane_chunk] = him


def _s5_kernel(u_ref, h0re_ref, h0im_ref, are_ref, aim_ref, bbd_ref, *rest, steps, with_y):
    if with_y:
        cbd_ref, d_ref, wglu_ref, bglu_ref, s_ref, hre_ref, him_ref, hre_scr, him_scr, bu_scr, hs_scr = rest
    else:
        hre_ref, him_ref, hre_scr, him_scr, bu_scr = rest
        hs_scr = None
    c = pl.program_id(1)

    @pl.when(c == 0)
    def _():
        hre_scr[...] = h0re_ref[...]
        him_scr[...] = h0im_ref[...]

    u = u_ref[...]
    ub = u.astype(BF16)
    hu, hst = SSM_WIDTH // 2, N_STATE // 2
    for half in range(2):
        for part in range(2):
            cols = slice(part * N_STATE + half * hst, part * N_STATE + (half + 1) * hst)
            bu_scr[:, cols] = jnp.dot(ub[:, half * hu:(half + 1) * hu], bbd_ref[half * hu:(half + 1) * hu, cols],
                                      preferred_element_type=F32)
    _s5_scan_chunk(bu_scr, hs_scr, hre_scr, him_scr, are_ref, aim_ref, steps, lane_chunk=1024)
    if with_y:
        y_halves = []
        for half in range(2):
            ycols = slice(half * hu, (half + 1) * hu)
            acc = None
            for part in range(2):
                rows = slice(part * N_STATE + half * hst, part * N_STATE + (half + 1) * hst)
                term = jnp.dot(hs_scr[:, rows].astype(BF16), cbd_ref[rows, ycols], preferred_element_type=F32)
                acc = term if acc is None else acc + term
            y_halves.append(acc)
        y = jnp.concatenate(y_halves, axis=1) + d_ref[...] * u
        g = jax.nn.gelu(y)
        gate = jnp.dot(g.astype(BF16), wglu_ref[...], preferred_element_type=F32) + bglu_ref[...]
        s_ref[...] = g * jax.nn.sigmoid(gate)

    @pl.when(c == pl.num_programs(1) - 1)
    def _():
        hre_ref[...] = hre_scr[...]
        him_ref[...] = him_scr[...]


def _s5_call(u3, h0re, h0im, a_re, a_im, bbd, tail, steps_per_chunk, with_y):
    ng, rows, _ = u3.shape
    j_total = rows // SCAN_ROWS
    steps = min(steps_per_chunk, j_total)
    nc = j_total // steps
    blk = steps * SCAN_ROWS
    const = lambda shape: pl.BlockSpec(shape, lambda g, c: (0,) * len(shape))
    st_spec = pl.BlockSpec((None, SCAN_ROWS, N_STATE), lambda g, c: (g, 0, 0))
    u_spec = pl.BlockSpec((None, blk, SSM_WIDTH), lambda g, c: (g, c, 0))
    st_shape = jax.ShapeDtypeStruct((ng, SCAN_ROWS, N_STATE), F32)
    in_specs = [u_spec, st_spec, st_spec, const((1, N_STATE)), const((1, N_STATE)),
                const((SSM_WIDTH, 2 * N_STATE))]
    scratch = [pltpu.VMEM((SCAN_ROWS, N_STATE), F32), pltpu.VMEM((SCAN_ROWS, N_STATE), F32),
               pltpu.VMEM((blk, 2 * N_STATE), F32)]
    if with_y:
        in_specs += [const((2 * N_STATE, SSM_WIDTH)), const((1, SSM_WIDTH)),
                     const((SSM_WIDTH, SSM_WIDTH)), const((1, SSM_WIDTH))]
        out_shape = (jax.ShapeDtypeStruct((ng, rows, SSM_WIDTH), F32), st_shape, st_shape)
        out_specs = (u_spec, st_spec, st_spec)
        scratch.append(pltpu.VMEM((blk, 2 * N_STATE), F32))
    else:
        out_shape = (st_shape, st_shape)
        out_specs = (st_spec, st_spec)
    return pl.pallas_call(
        functools.partial(_s5_kernel, steps=steps, with_y=with_y),
        out_shape=out_shape, grid=(ng, nc), in_specs=in_specs, out_specs=out_specs,
        scratch_shapes=scratch, compiler_params=_cparams("parallel", "arbitrary"),
        name="s5_scan_glu" if with_y else "s5_scan_state",
    )(u3, h0re, h0im, a_re, a_im, bbd, *tail)


def _s5_seg_init_kernel(fre_ref, fim_ref, are_ref, aim_ref, ire_ref, iim_ref, *, seg_len, n_seq, n_seg):
    pr, pi = are_ref[...], aim_ref[...]
    ar, ai = None, None
    e = seg_len
    while e:
        if e & 1:
            ar, ai = (pr, pi) if ar is None else (ar * pr - ai * pi, ar * pi + ai * pr)
        e >>= 1
        if e:
            pr, pi = pr * pr - pi * pi, 2.0 * pr * pi
    for n in range(n_seq):
        hr = jnp.zeros((1, N_STATE), F32)
        hi = jnp.zeros((1, N_STATE), F32)
        for s in range(n_seg):
            r = n * n_seg + s
            ire_ref[r:r + 1, :] = hr
            iim_ref[r:r + 1, :] = hi
            fr, fi = fre_ref[r:r + 1, :], fim_ref[r:r + 1, :]
            hr, hi = ar * hr - ai * hi + fr, ar * hi + ai * hr + fi


def _s5_seg_init(f_re, f_im, a_re, a_im, seg_len, n_seq, n_seg):
    shape = jax.ShapeDtypeStruct(f_re.shape, F32)
    return pl.pallas_call(
        functools.partial(_s5_seg_init_kernel, seg_len=seg_len, n_seq=n_seq, n_seg=n_seg),
        out_shape=(shape, shape), name="s5_seg_init",
    )(f_re, f_im, a_re, a_im)


def _s5_prompt(u, a_re, a_im, bbd, tail):
    b, t, _ = u.shape
    n_seg = SCAN_ROWS // b
    assert b * n_seg == SCAN_ROWS and t % n_seg == 0
    j = t // n_seg
    u3 = u.reshape(b, n_seg, j, SSM_WIDTH).transpose(2, 0, 1, 3).reshape(1, j * SCAN_ROWS, SSM_WIDTH)
    zeros = jnp.zeros((1, SCAN_ROWS, N_STATE), F32)
    f_re, f_im = _s5_call(u3, zeros, zeros, a_re, a_im, bbd, (), 64, with_y=False)
    i_re, i_im = _s5_seg_init(f_re[0], f_im[0], a_re, a_im, j, b, n_seg)
    s3, h_re, h_im = _s5_call(u3, i_re[None], i_im[None], a_re, a_im, bbd, tail, 64, with_y=True)
    s = s3.reshape(j, b, n_seg, SSM_WIDTH).transpose(1, 2, 0, 3).reshape(b, t, SSM_WIDTH)
    last = h_re.reshape(b, n_seg, N_STATE)[:, -1], h_im.reshape(b, n_seg, N_STATE)[:, -1]
    return s, last[0], last[1]


def _s5_sample(u, h0_re, h0_im, a_re, a_im, bbd, tail):
    n, j, _ = u.shape
    ng = n // SCAN_ROWS
    u3 = u.reshape(ng, SCAN_ROWS, j, SSM_WIDTH).transpose(0, 2, 1, 3).reshape(ng, j * SCAN_ROWS, SSM_WIDTH)
    s3, h_re, h_im = _s5_call(u3, h0_re.reshape(ng, SCAN_ROWS, N_STATE), h0_im.reshape(ng, SCAN_ROWS, N_STATE),
                              a_re, a_im, bbd, tail, j, with_y=True)
    s = s3.reshape(ng, j, SCAN_ROWS, SSM_WIDTH).transpose(0, 2, 1, 3).reshape(n, j, SSM_WIDTH)
    return s, h_re.reshape(n, N_STATE), h_im.reshape(n, N_STATE)


def _top2_sum(a, b, c, d):
    m1, n1 = jnp.maximum(a, b), jnp.minimum(a, b)
    m2, n2 = jnp.maximum(c, d), jnp.minimum(c, d)
    return jnp.maximum(m1, m2) + jnp.maximum(jnp.minimum(m1, m2), jnp.maximum(n1, n2))


def _route_rows(logit_rows, rbias_ref):
    scores = [jax.nn.sigmoid(r) for r in logit_rows]
    sel = [scores[e] + rbias_ref[e] for e in range(N_EXPERTS)]
    gs = [_top2_sum(*sel[EXPERTS_PER_GROUP * g:EXPERTS_PER_GROUP * (g + 1)]) for g in range(N_EXPERT_GROUPS)]
    best_v, best_g = gs[0], jnp.zeros(gs[0].shape, jnp.int32)
    for g in range(1, N_EXPERT_GROUPS):
        upd = gs[g] > best_v
        best_g = jnp.where(upd, g, best_g)
        best_v = jnp.where(upd, gs[g], best_v)
    picked = []
    for e in range(N_EXPERTS):
        g = e // EXPERTS_PER_GROUP
        rank = jnp.zeros(sel[e].shape, jnp.int32)
        for o in range(EXPERTS_PER_GROUP * g, EXPERTS_PER_GROUP * (g + 1)):
            if o != e:
                ahead = (sel[o] >= sel[e]) if o < e else (sel[o] > sel[e])
                rank = rank + ahead.astype(jnp.int32)
        on = jnp.logical_and(best_g == g, rank < 2)
        picked.append(jnp.where(on, scores[e], 0.0))
    den = picked[0]
    for e in range(1, N_EXPERTS):
        den = den + picked[e]
    return [w / den for w in picked]


def _outproj_kernel(rbias_ref, a_ref, s_ref, x_ref, wo_ref, g_ref, b_ref, rwh_ref, rwl_ref,
                    x1_ref, gates_ref):
    mix = (jnp.dot(a_ref[...].astype(BF16), wo_ref[0:ATT_WIDTH, :], preferred_element_type=F32)
           + jnp.dot(s_ref[...].astype(BF16), wo_ref[ATT_WIDTH:, :], preferred_element_type=F32))
    x1 = _layer_norm(DN_ALPHA * x_ref[...] + mix, g_ref[...], b_ref[...])
    x1_ref[...] = x1
    xh = x1.astype(BF16)
    xl = (x1 - xh.astype(F32)).astype(BF16)
    logits = (jnp.dot(xh, rwh_ref[...], preferred_element_type=F32)
              + jnp.dot(xl, rwh_ref[...], preferred_element_type=F32)
              + jnp.dot(xh, rwl_ref[...], preferred_element_type=F32))
    lt = logits.T
    gate_rows = _route_rows([lt[e:e + 1, :] for e in range(N_EXPERTS)], rbias_ref)
    tm = lt.shape[1]
    r_i = lax.broadcasted_iota(jnp.int32, (N_EXPERTS, tm), 0)
    gt = jnp.zeros((N_EXPERTS, tm), F32)
    for e in range(N_EXPERTS):
        gt = jnp.where(r_i == e, jnp.broadcast_to(gate_rows[e], (N_EXPERTS, tm)), gt)
    gt = jnp.concatenate([gt, jnp.zeros((LANES - N_EXPERTS, tm), F32)], axis=0)
    gates_ref[...] = gt.T


def _out_proj(a2d, s, x2d, wo_bf16, ln_g, ln_b, rw_hi, rw_lo, router_bias):
    t = x2d.shape[0]
    tm = min(512, t)
    row = lambda w: pl.BlockSpec((tm, w), lambda i: (i, 0))
    const = lambda shape: pl.BlockSpec(shape, lambda i: (0, 0))
    s_spec = row(SSM_WIDTH)
    return pl.pallas_call(
        _outproj_kernel,
        out_shape=(jax.ShapeDtypeStruct((t, D_MODEL), F32), jax.ShapeDtypeStruct((t, LANES), F32)),
        grid=(t // tm,),
        in_specs=[pl.BlockSpec(memory_space=pltpu.SMEM), row(ATT_WIDTH), s_spec, row(D_MODEL),
                  const((D_MODEL, D_MODEL)), const((1, D_MODEL)), const((1, D_MODEL)),
                  const((D_MODEL, LANES)), const((D_MODEL, LANES))],
        out_specs=(row(D_MODEL), row(LANES)), compiler_params=_cparams("parallel"), name="out_proj_router",
    )(router_bias, a2d, s, x2d, wo_bf16, ln_g.reshape(1, -1), ln_b.reshape(1, -1), rw_hi, rw_lo)


def _moe_kernel(x_ref, gates_ref, wg_ref, wu_ref, wd_ref, g_ref, b_ref, o_ref, xb_scr, acc_scr):
    e = pl.program_id(1)

    @pl.when(e == 0)
    def _():
        xb_scr[...] = x_ref[...].astype(BF16)
        acc_scr[...] = jnp.zeros(acc_scr.shape, F32)

    xb = xb_scr[...]
    hg = jnp.dot(xb, wg_ref[...], preferred_element_type=F32)
    hu = jnp.dot(xb, wu_ref[...], preferred_element_type=F32)
    h = jax.nn.silu(hg) * hu
    y = jnp.dot(h.astype(BF16), wd_ref[...], preferred_element_type=F32)
    gates = gates_ref[...]
    lane = lax.broadcasted_iota(jnp.int32, gates.shape, 1)
    gate = jnp.sum(jnp.where(lane == e, gates, 0.0), axis=-1, keepdims=True)
    acc_scr[...] = acc_scr[...] + gate * y

    @pl.when(e == pl.num_programs(1) - 1)
    def _():
        o_ref[...] = _layer_norm(DN_ALPHA * x_ref[...] + acc_scr[...], g_ref[...], b_ref[...])


def _moe(x2d, gates, wg, wu, wd, ln_g, ln_b):
    t = x2d.shape[0]
    tm = min(MOE_TOKEN_TILE, t)
    row = lambda w: pl.BlockSpec((tm, w), lambda i, e: (i, 0))
    const = lambda shape: pl.BlockSpec(shape, lambda i, e: (0, 0))
    return pl.pallas_call(
        _moe_kernel, out_shape=jax.ShapeDtypeStruct((t, D_MODEL), F32), grid=(t // tm, N_EXPERTS),
        in_specs=[row(D_MODEL), row(LANES),
                  pl.BlockSpec((None, D_MODEL, D_EXPERT), lambda i, e: (e, 0, 0)),
                  pl.BlockSpec((None, D_MODEL, D_EXPERT), lambda i, e: (e, 0, 0)),
                  pl.BlockSpec((None, D_EXPERT, D_MODEL), lambda i, e: (e, 0, 0)),
                  const((1, D_MODEL)), const((1, D_MODEL))],
        out_specs=row(D_MODEL),
        scratch_shapes=[pltpu.VMEM((tm, D_MODEL), BF16), pltpu.VMEM((tm, D_MODEL), F32)],
        compiler_params=_cparams("parallel", "arbitrary"), name="moe_ln",
    )(x2d, gates, wg, wu, wd, ln_g.reshape(1, -1), ln_b.reshape(1, -1))


def kernel(x_prompt, x_sample, cache_k, cache_v, state_ssm_re, state_ssm_im, page_table, w_in, w_out, ln1_g, ln1_b, ln2_g, ln2_b, lambda_q1, lambda_k1, lambda_q2, lambda_k2, subln_g, ssm_lambda_re, ssm_lambda_im, ssm_log_dt, ssm_b_re, ssm_b_im, ssm_c_re, ssm_c_im, ssm_d, ssm_w_glu, ssm_b_glu, router_w, router_bias, moe_w_gate, moe_w_up, moe_w_down):
    b, t, _ = x_prompt.shape
    n, dec_seq, _ = x_sample.shape
    depth = w_in.shape[0]
    n_pool = cache_k.shape[1]

    slopes = 2.0 ** (-8.0 * jnp.arange(1, N_HEADS + 1, dtype=F32) / N_HEADS)
    col = jnp.arange(LANES)
    cslope = jnp.where(col < 2 * N_HEADS * dec_seq, slopes[jnp.minimum(col // (2 * dec_seq), N_HEADS - 1)], 0.0)
    cslope = cslope.reshape(1, LANES).astype(F32)
    rw_pad = jnp.zeros((D_MODEL, LANES), F32).at[:, :N_EXPERTS].set(router_w)
    rw_hi = rw_pad.astype(BF16)
    rw_lo = (rw_pad - rw_hi.astype(F32)).astype(BF16)

    pages_k = cache_k.reshape(depth * n_pool, PAGE_SIZE * N_HEADS, V_DIM)
    pages_v = cache_v.reshape(depth * n_pool, PAGE_SIZE * N_HEADS, V_DIM)

    yp = x_prompt.reshape(b * t, D_MODEL)
    ys = x_sample.reshape(n * dec_seq, D_MODEL)
    outs = [[] for _ in range(8)]
    kbuf = jnp.zeros((depth * b * t * N_HEADS, V_DIM), F32)
    vbuf = jnp.zeros((depth * b * t * N_HEADS, V_DIM), F32)
    for l in range(depth):
        lam_init = 0.8 - 0.6 * math.exp(-0.3 * l)
        a_re, a_im, bb_re, bb_im, lam = _layer_prep(
            ssm_lambda_re[l], ssm_lambda_im[l], ssm_log_dt[l], ssm_b_re[l], ssm_b_im[l],
            lambda_q1[l], lambda_k1[l], lambda_q2[l], lambda_k2[l], lam_init)
        a_re, a_im = a_re.reshape(1, N_STATE), a_im.reshape(1, N_STATE)
        bbd = jnp.concatenate([_block_diag_in(bb_re), _block_diag_in(bb_im)], axis=1).astype(BF16)
        cbd = jnp.concatenate([_block_diag_out(ssm_c_re[l]), -_block_diag_out(ssm_c_im[l])], axis=0).astype(BF16)
        tail = (cbd, ssm_d[l].reshape(1, -1), ssm_w_glu[l].astype(BF16), ssm_b_glu[l].reshape(1, -1))
        w_in_b, w_out_b = w_in[l].astype(BF16), w_out[l].astype(BF16)
        wg, wu, wd = moe_w_gate[l].astype(BF16), moe_w_up[l].astype(BF16), moe_w_down[l].astype(BF16)

        def finish(x2d, a2d, s2d):
            x1, gates = _out_proj(a2d, s2d, x2d, w_out_b, ln1_g[l], ln1_b[l], rw_hi, rw_lo, router_bias)
            return _moe(x1, gates, wg, wu, wd, ln2_g[l], ln2_b[l])

        u, q, kb, vt, kbuf, vbuf = _in_proj_prompt(yp, w_in_b, w_in_b[:, 3 * 512:].T, kbuf, vbuf, l, b)
        a = _attn_prompt(q.reshape(b, t, -1), kb.reshape(b, t, -1), vt, slopes, lam, subln_g[l], lam_init)
        s, hr, hi = _s5_prompt(u.reshape(b, t, -1), a_re, a_im, bbd, tail)
        yp = finish(yp, a.reshape(b * t, -1), s.reshape(b * t, -1))
        outs[2].append(hr.reshape(b, N_SSM_GROUPS, SSM_STATE))
        outs[3].append(hi.reshape(b, N_SSM_GROUPS, SSM_STATE))

        u, q, k, v = _in_proj_sample(ys, w_in_b)
        a = _attn_sample(q.reshape(n, dec_seq, -1), k.reshape(n, dec_seq, -1), v.reshape(n, dec_seq, -1),
                         pages_k, pages_v, l * n_pool, page_table, cslope, lam, subln_g[l], lam_init)
        s, hr, hi = _s5_sample(u.reshape(n, dec_seq, -1), state_ssm_re[l].reshape(n, N_STATE),
                               state_ssm_im[l].reshape(n, N_STATE), a_re, a_im, bbd, tail)
        ys = finish(ys, a.reshape(n * dec_seq, -1), s.reshape(n * dec_seq, -1))
        outs[4].append(k.reshape(n, dec_seq, N_HEADS, V_DIM))
        outs[5].append(v.reshape(n, dec_seq, N_HEADS, V_DIM))
        outs[6].append(hr.reshape(n, N_SSM_GROUPS, SSM_STATE))
        outs[7].append(hi.reshape(n, N_SSM_GROUPS, SSM_STATE))

    outs[0] = kbuf.reshape(depth, b, t, N_HEADS, V_DIM)
    outs[1] = vbuf.reshape(depth, b, t, N_HEADS, V_DIM)
    outs = [o if isinstance(o, jax.Array) else jnp.stack(o) for o in outs]
    return (yp.reshape(b, t, D_MODEL), ys.reshape(n, dec_seq, D_MODEL)) + tuple(outs)
```

```python
import functools
import math

import jax
import jax.numpy as jnp
from jax import lax
from jax.experimental import pallas as pl
from jax.experimental.pallas import tpu as pltpu

F32 = jnp.float32
BF16 = jnp.bfloat16

D_MODEL = 1024
ATT_WIDTH = 512
SSM_WIDTH = 512
HEAD_DIM = 64
V_DIM = 128
N_HEADS = 4
SSM_GROUP = 16
N_SSM_GROUPS = 32
SSM_STATE = 64
N_STATE = N_SSM_GROUPS * SSM_STATE
N_EXPERTS = 16
EXPERTS_PER_GROUP = 4
N_EXPERT_GROUPS = 4
D_EXPERT = 512
PAGE_SIZE = 128
DEPTH = 2
DN_ALPHA = (2 * DEPTH) ** 0.25
LN_EPS = 1e-5
RMS_EPS = 1e-5
ATTN_SCALE = HEAD_DIM ** -0.5

SUBLANES = 8
BF16_SUBLANES = 16
LANES = 128
SCAN_ROWS = SUBLANES
MOE_TOKEN_TILE = 1024
VMEM_LIMIT = 52 * 1024 * 1024


def _cparams(*sem):
    return pltpu.CompilerParams(dimension_semantics=sem, vmem_limit_bytes=VMEM_LIMIT)


def _layer_norm(z, g, b):
    mu = jnp.mean(z, -1, keepdims=True)
    var = jnp.mean(jnp.square(z - mu), -1, keepdims=True)
    return (z - mu) * lax.rsqrt(var + LN_EPS) * g + b


def _prep_kernel(lre_ref, lim_ref, ldt_ref, bre_ref, bim_ref, lq1_ref, lk1_ref, lq2_ref, lk2_ref,
                 are_ref, aim_ref, bbre_ref, bbim_ref, lam_ref, *, lam_init):
    lr, li = lre_ref[...], lim_ref[...]
    dt = jnp.exp(ldt_ref[...])
    mag = jnp.exp(lr * dt)
    a_re, a_im = mag * jnp.cos(li * dt), mag * jnp.sin(li * dt)
    den = lr * lr + li * li
    nr = a_re - 1.0
    coef_re = (nr * lr + a_im * li) / den
    coef_im = (a_im * lr - nr * li) / den
    br, bi = bre_ref[...], bim_ref[...]
    bbre_ref[...] = coef_re[..., None] * br - coef_im[..., None] * bi
    bbim_ref[...] = coef_re[..., None] * bi + coef_im[..., None] * br
    are_ref[...] = a_re
    aim_ref[...] = a_im
    s1 = jnp.sum(lq1_ref[...] * lk1_ref[...], axis=-1, keepdims=True)
    s2 = jnp.sum(lq2_ref[...] * lk2_ref[...], axis=-1, keepdims=True)
    lam_ref[...] = jnp.exp(s1) - jnp.exp(s2) + lam_init


def _layer_prep(lam_re, lam_im, log_dt, b_re, b_im, lq1, lk1, lq2, lk2, lam_init):
    g, p, c = N_SSM_GROUPS, SSM_STATE, SSM_GROUP
    out_shape = (jax.ShapeDtypeStruct((g, p), F32), jax.ShapeDtypeStruct((g, p), F32),
                 jax.ShapeDtypeStruct((g, p, c), F32), jax.ShapeDtypeStruct((g, p, c), F32),
                 jax.ShapeDtypeStruct((1, 1), F32))
    return pl.pallas_call(
        functools.partial(_prep_kernel, lam_init=lam_init), out_shape=out_shape, name="layer_prep",
    )(lam_re, lam_im, log_dt.reshape(g, 1), b_re, b_im,
      lq1.reshape(1, HEAD_DIM), lk1.reshape(1, HEAD_DIM), lq2.reshape(1, HEAD_DIM), lk2.reshape(1, HEAD_DIM))


def _block_diag_in(bb):
    eye = jnp.eye(N_SSM_GROUPS, dtype=bb.dtype)
    m = jnp.transpose(bb, (0, 2, 1))[:, :, None, :] * eye[:, None, :, None]
    return m.reshape(SSM_WIDTH, N_STATE)


def _block_diag_out(c):
    eye = jnp.eye(N_SSM_GROUPS, dtype=c.dtype)
    m = jnp.transpose(c, (0, 2, 1))[:, :, None, :] * eye[:, None, :, None]
    return m.reshape(N_STATE, SSM_WIDTH)


def _proj_sample_kernel(x_ref, w_ref, u_ref, q_ref, k_ref, v_ref):
    xb = x_ref[...].astype(BF16)
    for i, o_ref in enumerate((u_ref, q_ref, k_ref, v_ref)):
        o_ref[...] = jnp.dot(xb, w_ref[:, i * 512:(i + 1) * 512], preferred_element_type=F32)


def _in_proj_sample(x2d, w_bf16):
    t = x2d.shape[0]
    tm = min(512, t)
    out = jax.ShapeDtypeStruct((t, 512), F32)
    row = pl.BlockSpec((tm, 512), lambda i: (i, 0))
    return pl.pallas_call(
        _proj_sample_kernel, out_shape=(out,) * 4, grid=(t // tm,),
        in_specs=[pl.BlockSpec((tm, D_MODEL), lambda i: (i, 0)),
                  pl.BlockSpec((D_MODEL, 4 * 512), lambda i: (0, 0))],
        out_specs=(row,) * 4, compiler_params=_cparams("parallel"), name="in_proj_sample",
    )(x2d, w_bf16)


def _proj_prompt_kernel(x_ref, w_ref, wvt_ref, kbuf_in, vbuf_in, u_ref, q_ref, kb_ref, vt_ref, k_ref, v_ref):
    del kbuf_in, vbuf_in
    xb = x_ref[...].astype(BF16)
    tm = xb.shape[0]
    col = lambda i: jnp.dot(xb, w_ref[:, i * 512:(i + 1) * 512], preferred_element_type=F32)
    u_ref[...] = col(0)
    q_ref[...] = col(1)
    k, v = col(2), col(3)
    kb_ref[...] = k.astype(BF16)
    for h in range(N_HEADS):
        k_ref[pl.ds(h, tm, stride=N_HEADS), :] = k[:, h * V_DIM:(h + 1) * V_DIM]
        v_ref[pl.ds(h, tm, stride=N_HEADS), :] = v[:, h * V_DIM:(h + 1) * V_DIM]
    vt = lax.dot_general(wvt_ref[...], xb, (((1,), (1,)), ((), ())), preferred_element_type=F32)
    vt_ref[...] = vt.astype(BF16)


def _in_proj_prompt(x2d, w_bf16, wv_t, kbuf, vbuf, layer, n_seq):
    t = x2d.shape[0]
    seq_len = t // n_seq
    tm = min(512, seq_len)
    assert seq_len % tm == 0
    nt, nl = seq_len // tm, t // tm
    row = pl.BlockSpec((tm, 512), lambda i: (i, 0))
    cache_row = pl.BlockSpec((tm * N_HEADS, V_DIM), lambda i: (layer * nl + i, 0))
    anyspec = pl.BlockSpec(memory_space=pl.ANY)
    out_shape = (jax.ShapeDtypeStruct((t, 512), F32), jax.ShapeDtypeStruct((t, 512), F32),
                 jax.ShapeDtypeStruct((t, 512), BF16), jax.ShapeDtypeStruct((n_seq, 512, seq_len), BF16),
                 jax.ShapeDtypeStruct(kbuf.shape, F32), jax.ShapeDtypeStruct(vbuf.shape, F32))
    out_specs = (row, row, row,
                 pl.BlockSpec((None, 512, tm), lambda i: (i // nt, 0, i % nt)), cache_row, cache_row)
    return pl.pallas_call(
        _proj_prompt_kernel, out_shape=out_shape, grid=(nl,),
        in_specs=[pl.BlockSpec((tm, D_MODEL), lambda i: (i, 0)),
                  pl.BlockSpec((D_MODEL, 4 * 512), lambda i: (0, 0)),
                  pl.BlockSpec((512, D_MODEL), lambda i: (0, 0)), anyspec, anyspec],
        out_specs=out_specs, input_output_aliases={3: 4, 4: 5},
        compiler_params=_cparams("parallel"), name="in_proj_prompt",
    )(x2d, w_bf16, wv_t, kbuf, vbuf)


def _sub_ln(a, g, lam_init):
    return a * lax.rsqrt(jnp.mean(a * a, axis=-1, keepdims=True) + RMS_EPS) * g * (1.0 - lam_init)


ALIBI_SPLIT = 64
ATTN_KEY_BLOCK = 512
ATTN_QUERY_BLOCK = 1024


def _attn_prompt_kernel(qi_tab, ki_tab, slopes_ref, lam_ref, q_ref, kb_ref, vt_ref, kfeat_ref, vfeat_ref,
                        g_ref, o_ref, qs_scr, m_scr, acc_scr, *, tq, tk, lam_init):
    h, step = pl.program_id(1), pl.program_id(2)
    qi, ki = qi_tab[step], ki_tab[step]
    slope = slopes_ref[h]

    @pl.when(ki == 0)
    def _():
        qs = q_ref[...] * ATTN_SCALE
        lane = lax.broadcasted_iota(jnp.int32, qs.shape, 1)
        qfeat = jnp.where(lane == 0, ALIBI_SPLIT * slope, jnp.where(lane == 1, slope, 0.0)).astype(BF16)
        qs_scr[0:tq, 0:LANES] = jnp.where(lane < HEAD_DIM, qs, 0.0).astype(BF16)
        qs_scr[tq:2 * tq, 0:LANES] = jnp.where(lane >= HEAD_DIM, qs, 0.0).astype(BF16)
        qs_scr[0:tq, LANES:2 * LANES] = qfeat
        qs_scr[tq:2 * tq, LANES:2 * LANES] = qfeat
        m_scr[...] = jnp.full(m_scr.shape, -jnp.inf, F32)
        acc_scr[...] = jnp.zeros(acc_scr.shape, F32)

    def update(diag):
        kp = jnp.concatenate([kb_ref[...], kfeat_ref[...]], axis=1)
        vpt = jnp.concatenate([vt_ref[...], vfeat_ref[...]], axis=0)
        d = slope * (ki * tk - qi * tq).astype(F32)
        chunks = [c0 for c0 in range(0, 2 * tq, tk) if (c0 % tq) // tk >= diag]

        def scores(c0):
            return lax.dot_general(kp, qs_scr[c0:c0 + tk, :], (((1,), (1,)), ((), ())),
                                   preferred_element_type=F32)

        def softmax_part(s, c0):
            if (c0 % tq) // tk == diag:
                key = lax.broadcasted_iota(jnp.int32, s.shape, 0)
                qry = lax.broadcasted_iota(jnp.int32, s.shape, 1)
                s = jnp.where(qry >= key, s, -jnp.inf)
            m_prev = m_scr[:, c0:c0 + tk]
            m_new = jnp.maximum(m_prev, jnp.max(s, axis=0, keepdims=True) + d)
            m_scr[:, c0:c0 + tk] = m_new
            return jnp.exp(s - (m_new - d)).astype(BF16), jnp.exp(m_prev - m_new)

        def accumulate(c0, p, alpha):
            pv = jnp.dot(vpt, p, preferred_element_type=F32)
            acc_scr[:, c0:c0 + tk] = alpha * acc_scr[:, c0:c0 + tk] + pv

        n = len(chunks)
        s_q = [scores(chunks[0])] + ([scores(chunks[1])] if n > 1 else [])
        pending = None
        for i, c0 in enumerate(chunks):
            s = s_q.pop(0)
            if i + 2 < n:
                s_q.append(scores(chunks[i + 2]))
            if pending is not None:
                accumulate(*pending)
            pending = (c0,) + softmax_part(s, c0)
        accumulate(*pending)

    rel = ki - qi * (tq // tk)

    @pl.when(rel < 0)
    def _():
        update(-1)

    for j in range(tq // tk):
        @pl.when(rel == j)
        def _(j=j):
            update(j)

    @pl.when(rel == tq // tk - 1)
    def _():
        o1 = acc_scr[0:V_DIM, 0:tq] / acc_scr[V_DIM:V_DIM + 1, 0:tq]
        o2 = acc_scr[0:V_DIM, tq:2 * tq] / acc_scr[V_DIM:V_DIM + 1, tq:2 * tq]
        a = (o1 - lam_ref[0, 0] * o2).T
        o_ref[...] = _sub_ln(a, g_ref[...], lam_init)


def _attn_prompt(q, kb, vt, slopes, lam, subln_g, lam_init):
    b, t, _ = q.shape
    tk = min(ATTN_KEY_BLOCK, t)
    tq = min(ATTN_QUERY_BLOCK, t)
    nq, r = t // tq, tq // tk
    pairs = [(qi, ki) for qi in range(nq) for ki in range(r * (qi + 1))]
    qi_tab = jnp.asarray([p[0] for p in pairs], jnp.int32)
    ki_tab = jnp.asarray([p[1] for p in pairs], jnp.int32)
    c = jnp.arange(tk)[:, None]
    lane = jnp.arange(LANES)[None, :]
    kfeat = jnp.where(lane == 0, c // ALIBI_SPLIT, jnp.where(lane == 1, c % ALIBI_SPLIT, 0)).astype(BF16)
    ones_row = jnp.arange(BF16_SUBLANES)[:, None] == 0
    vfeat = jnp.broadcast_to(jnp.where(ones_row, 1.0, 0.0), (BF16_SUBLANES, tk)).astype(BF16)
    kernel = functools.partial(_attn_prompt_kernel, tq=tq, tk=tk, lam_init=lam_init)
    smem = pl.BlockSpec(memory_space=pltpu.SMEM)
    q_spec = pl.BlockSpec((None, tq, V_DIM), lambda b_, h, s, qt, kt: (b_, qt[s], h))
    k_spec = pl.BlockSpec((None, tk, V_DIM), lambda b_, h, s, qt, kt: (b_, kt[s], h))
    vt_spec = pl.BlockSpec((None, V_DIM, tk), lambda b_, h, s, qt, kt: (b_, h, kt[s]))
    const = lambda shape: pl.BlockSpec(shape, lambda b_, h, s, qt, kt: (0, 0))
    grid_spec = pltpu.PrefetchScalarGridSpec(
        num_scalar_prefetch=2, grid=(b, N_HEADS, len(pairs)),
        in_specs=[smem, smem, q_spec, k_spec, vt_spec, const((tk, LANES)), const((BF16_SUBLANES, tk)),
                  const((1, V_DIM))],
        out_specs=q_spec,
        scratch_shapes=[pltpu.VMEM((2 * tq, 2 * LANES), BF16), pltpu.VMEM((1, 2 * tq), F32),
                        pltpu.VMEM((V_DIM + BF16_SUBLANES, 2 * tq), F32)])
    return pl.pallas_call(
        kernel, out_shape=jax.ShapeDtypeStruct((b, t, ATT_WIDTH), F32), grid_spec=grid_spec,
        compiler_params=_cparams("parallel", "parallel", "arbitrary"), name="attn_prompt",
    )(qi_tab, ki_tab, slopes, lam, q, kb, vt, kfeat, vfeat, subln_g.reshape(1, V_DIM))


def _attn_sample_kernel(pt_ref, lam_ref, q_ref, knew_ref, vnew_ref, g_ref, cslope_ref, *rest,
                        n_pages, dec_seq, lam_init):
    del pt_ref
    k_pages = rest[:n_pages]
    v_pages = rest[n_pages:2 * n_pages]
    o_ref, s_scr = rest[2 * n_pages:]
    past = n_pages * PAGE_SIZE
    cols_per_head = 2 * dec_seq
    head_rows = lambda ref, h: ref[pl.ds(h, PAGE_SIZE, stride=N_HEADS), :]

    q = q_ref[...] * ATTN_SCALE
    lane = lax.broadcasted_iota(jnp.int32, (dec_seq, V_DIM), 1)
    qw = []
    for h in range(N_HEADS):
        qh = q[:, h * V_DIM:(h + 1) * V_DIM]
        blocks = [jnp.where(lane < HEAD_DIM, qh, 0.0), jnp.where(lane >= HEAD_DIM, qh, 0.0)]
        if h:
            blocks.insert(0, jnp.zeros((h * cols_per_head, V_DIM), F32))
        blocks.append(jnp.zeros((LANES - (h + 1) * cols_per_head, V_DIM), F32))
        qw.append(jnp.concatenate(blocks, axis=0))
    dn = (((1,), (1,)), ((), ()))

    s_past = None
    for h0 in range(0, N_HEADS, 2):
        keys = jnp.concatenate(
            [jnp.concatenate([head_rows(k_pages[p], h0), head_rows(k_pages[p], h0 + 1)], axis=1)
             for p in range(n_pages)], axis=0)
        w = jnp.concatenate([qw[h0], qw[h0 + 1]], axis=1)
        part = lax.dot_general(keys, w, dn, preferred_element_type=F32)
        s_past = part if s_past is None else s_past + part
    row = lax.broadcasted_iota(jnp.int32, (past, LANES), 0)
    col = lax.broadcasted_iota(jnp.int32, (past, LANES), 1)
    cslope = cslope_ref[...]
    dist = past + col % dec_seq - row
    s_scr[0:past, :] = s_past - cslope * dist.astype(F32)

    knew = knew_ref[...]
    s_new = None
    for h in range(N_HEADS):
        part = lax.dot_general(knew[:, h * V_DIM:(h + 1) * V_DIM], qw[h], dn, preferred_element_type=F32)
        s_new = part if s_new is None else s_new + part
    row = lax.broadcasted_iota(jnp.int32, (dec_seq, LANES), 0)
    col = lax.broadcasted_iota(jnp.int32, (dec_seq, LANES), 1)
    dist = col % dec_seq - row
    s_scr[past:past + dec_seq, :] = jnp.where(dist >= 0, s_new - cslope * dist.astype(F32), -jnp.inf)
    s_scr[past + dec_seq:past + PAGE_SIZE, :] = jnp.full((PAGE_SIZE - dec_seq, LANES), -jnp.inf, F32)

    s_all = s_scr[...]
    m = jnp.max(s_all, axis=0, keepdims=True)
    p_all = jnp.exp(s_all - m)
    l = jnp.sum(p_all, axis=0, keepdims=True)
    c_i = lax.broadcasted_iota(jnp.int32, (1, LANES), 1)
    sign = jnp.where((c_i // dec_seq) % 2 == 0, 1.0, -lam_ref[0, 0])
    s_scr[...] = p_all * (sign / l)

    outs = [jnp.zeros((cols_per_head, V_DIM), F32) for _ in range(N_HEADS)]
    vnew = vnew_ref[...]
    vzero = jnp.zeros((PAGE_SIZE - dec_seq, V_DIM), F32)
    for p in range(n_pages + 1):
        pt = s_scr[p * PAGE_SIZE:(p + 1) * PAGE_SIZE, :].T
        for h in range(N_HEADS):
            if p < n_pages:
                vh = head_rows(v_pages[p], h)
            else:
                vh = jnp.concatenate([vnew[:, h * V_DIM:(h + 1) * V_DIM], vzero], axis=0)
            outs[h] = outs[h] + jnp.dot(pt[h * cols_per_head:(h + 1) * cols_per_head, :], vh,
                                        preferred_element_type=F32)
    for h in range(N_HEADS):
        a = outs[h][0:dec_seq, :] + outs[h][dec_seq:2 * dec_seq, :]
        o_ref[:, h * V_DIM:(h + 1) * V_DIM] = _sub_ln(a, g_ref[...], lam_init)


def _attn_sample(q, k, v, k_pages, v_pages, page_base, page_table, cslope, lam, subln_g, lam_init):
    n, dec_seq, _ = q.shape
    n_pages = page_table.shape[1]
    assert 2 * N_HEADS * dec_seq <= LANES and dec_seq % SUBLANES == 0 and N_HEADS % 2 == 0
    kernel = functools.partial(_attn_sample_kernel, n_pages=n_pages, dec_seq=dec_seq, lam_init=lam_init)
    tok = pl.BlockSpec((None, dec_seq, ATT_WIDTH), lambda i, pt: (i, 0, 0))

    def page_spec(p):
        return pl.BlockSpec((None, PAGE_SIZE * N_HEADS, V_DIM),
                            lambda i, pt: (page_base + pt[i * n_pages + p], 0, 0))

    grid_spec = pltpu.PrefetchScalarGridSpec(
        num_scalar_prefetch=1, grid=(n,),
        in_specs=[pl.BlockSpec(memory_space=pltpu.SMEM), tok, tok, tok,
                  pl.BlockSpec((1, V_DIM), lambda i, pt: (0, 0)),
                  pl.BlockSpec((1, LANES), lambda i, pt: (0, 0))]
                 + [page_spec(p) for p in range(n_pages)] * 2,
        out_specs=tok,
        scratch_shapes=[pltpu.VMEM(((n_pages + 1) * PAGE_SIZE, LANES), F32)])
    return pl.pallas_call(
        kernel, out_shape=jax.ShapeDtypeStruct((n, dec_seq, ATT_WIDTH), F32), grid_spec=grid_spec,
        compiler_params=_cparams("arbitrary"), name="attn_sample",
    )(page_table.reshape(-1), lam, q, k, v, subln_g.reshape(1, V_DIM), cslope,
      *([k_pages] * n_pages), *([v_pages] * n_pages))


def _s5_scan_chunk(bu_scr, hs_scr, hre_scr, him_scr, are_ref, aim_ref, steps, lane_chunk):
    for c in range(N_STATE // lane_chunk):
        lo = c * lane_chunk
        are = jnp.broadcast_to(are_ref[:, lo:lo + lane_chunk], (SCAN_ROWS, lane_chunk))
        aim = jnp.broadcast_to(aim_ref[:, lo:lo + lane_chunk], (SCAN_ROWS, lane_chunk))

        def body(j, carry, lo=lo, are=are, aim=aim):
            hre, him = carry
            r0 = pl.multiple_of(j * SCAN_ROWS, SCAN_ROWS)
            bre = bu_scr[pl.ds(r0, SCAN_ROWS), lo:lo + lane_chunk]
            bim = bu_scr[pl.ds(r0, SCAN_ROWS), N_STATE + lo:N_STATE + lo + lane_chunk]
            nre = are * hre - aim * him + bre
            nim = are * him + aim * hre + bim
            if hs_scr is not None:
                hs_scr[pl.ds(r0, SCAN_ROWS), lo:lo + lane_chunk] = nre
                hs_scr[pl.ds(r0, SCAN_ROWS), N_STATE + lo:N_STATE + lo + lane_chunk] = nim
            return nre, nim

        hre, him = lax.fori_loop(0, steps, body,
                                 (hre_scr[:, lo:lo + lane_chunk], him_scr[:, lo:lo + lane_chunk]),
                                 unroll=min(4, steps))
        hre_scr[:, lo:lo + lane_chunk] = hre
        him_scr[:, lo:lo + lane_chunk] = him


def _s5_kernel(u_ref, h0re_ref, h0im_ref, are_ref, aim_ref, bbd_ref, *rest, steps, with_y):
    if with_y:
        cbd_ref, d_ref, wglu_ref, bglu_ref, s_ref, hre_ref, him_ref, hre_scr, him_scr, bu_scr, hs_scr = rest
    else:
        hre_ref, him_ref, hre_scr, him_scr, bu_scr = rest
        hs_scr = None
    c = pl.program_id(1)

    @pl.when(c == 0)
    def _():
        hre_scr[...] = h0re_ref[...]
        him_scr[...] = h0im_ref[...]

    u = u_ref[...]
    ub = u.astype(BF16)
    hu, hst = SSM_WIDTH // 2, N_STATE // 2
    for half in range(2):
        for part in range(2):
            cols = slice(part * N_STATE + half * hst, part * N_STATE + (half + 1) * hst)
            bu_scr[:, cols] = jnp.dot(ub[:, half * hu:(half + 1) * hu], bbd_ref[half * hu:(half + 1) * hu, cols],
                                      preferred_element_type=F32)
    _s5_scan_chunk(bu_scr, hs_scr, hre_scr, him_scr, are_ref, aim_ref, steps, lane_chunk=1024)
    if with_y:
        y_halves = []
        for half in range(2):
            ycols = slice(half * hu, (half + 1) * hu)
            acc = None
            for part in range(2):
                rows = slice(part * N_STATE + half * hst, part * N_STATE + (half + 1) * hst)
                term = jnp.dot(hs_scr[:, rows].astype(BF16), cbd_ref[rows, ycols], preferred_element_type=F32)
                acc = term if acc is None else acc + term
            y_halves.append(acc)
        y = jnp.concatenate(y_halves, axis=1) + d_ref[...] * u
        g = jax.nn.gelu(y)
        gate = jnp.dot(g.astype(BF16), wglu_ref[...], preferred_element_type=F32) + bglu_ref[...]
        s_ref[...] = g * jax.nn.sigmoid(gate)

    @pl.when(c == pl.num_programs(1) - 1)
    def _():
        hre_ref[...] = hre_scr[...]
        him_ref[...] = him_scr[...]


def _s5_call(u3, h0re, h0im, a_re, a_im, bbd, tail, steps_per_chunk, with_y):
    ng, rows, _ = u3.shape
    j_total = rows // SCAN_ROWS
    steps = min(steps_per_chunk, j_total)
    nc = j_total // steps
    blk = steps * SCAN_ROWS
    const = lambda shape: pl.BlockSpec(shape, lambda g, c: (0,) * len(shape))
    st_spec = pl.BlockSpec((None, SCAN_ROWS, N_STATE), lambda g, c: (g, 0, 0))
    u_spec = pl.BlockSpec((None, blk, SSM_WIDTH), lambda g, c: (g, c, 0))
    st_shape = jax.ShapeDtypeStruct((ng, SCAN_ROWS, N_STATE), F32)
    in_specs = [u_spec, st_spec, st_spec, const((1, N_STATE)), const((1, N_STATE)),
                const((SSM_WIDTH, 2 * N_STATE))]
    scratch = [pltpu.VMEM((SCAN_ROWS, N_STATE), F32), pltpu.VMEM((SCAN_ROWS, N_STATE), F32),
               pltpu.VMEM((blk, 2 * N_STATE), F32)]
    if with_y:
        in_specs += [const((2 * N_STATE, SSM_WIDTH)), const((1, SSM_WIDTH)),
                     const((SSM_WIDTH, SSM_WIDTH)), const((1, SSM_WIDTH))]
        out_shape = (jax.ShapeDtypeStruct((ng, rows, SSM_WIDTH), F32), st_shape, st_shape)
        out_specs = (u_spec, st_spec, st_spec)
        scratch.append(pltpu.VMEM((blk, 2 * N_STATE), F32))
    else:
        out_shape = (st_shape, st_shape)
        out_specs = (st_spec, st_spec)
    return pl.pallas_call(
        functools.partial(_s5_kernel, steps=steps, with_y=with_y),
        out_shape=out_shape, grid=(ng, nc), in_specs=in_specs, out_specs=out_specs,
        scratch_shapes=scratch, compiler_params=_cparams("parallel", "arbitrary"),
        name="s5_scan_glu" if with_y else "s5_scan_state",
    )(u3, h0re, h0im, a_re, a_im, bbd, *tail)


def _s5_seg_init_kernel(fre_ref, fim_ref, are_ref, aim_ref, ire_ref, iim_ref, *, seg_len, n_seq, n_seg):
    pr, pi = are_ref[...], aim_ref[...]
    ar, ai = None, None
    e = seg_len
    while e:
        if e & 1:
            ar, ai = (pr, pi) if ar is None else (ar * pr - ai * pi, ar * pi + ai * pr)
        e >>= 1
        if e:
            pr, pi = pr * pr - pi * pi, 2.0 * pr * pi
    for n in range(n_seq):
        hr = jnp.zeros((1, N_STATE), F32)
        hi = jnp.zeros((1, N_STATE), F32)
        for s in range(n_seg):
            r = n * n_seg + s
            ire_ref[r:r + 1, :] = hr
            iim_ref[r:r + 1, :] = hi
            fr, fi = fre_ref[r:r + 1, :], fim_ref[r:r + 1, :]
            hr, hi = ar * hr - ai * hi + fr, ar * hi + ai * hr + fi


def _s5_seg_init(f_re, f_im, a_re, a_im, seg_len, n_seq, n_seg):
    shape = jax.ShapeDtypeStruct(f_re.shape, F32)
    return pl.pallas_call(
        functools.partial(_s5_seg_init_kernel, seg_len=seg_len, n_seq=n_seq, n_seg=n_seg),
        out_shape=(shape, shape), name="s5_seg_init",
    )(f_re, f_im, a_re, a_im)


def _s5_prompt(u, a_re, a_im, bbd, tail):
    b, t, _ = u.shape
    n_seg = SCAN_ROWS // b
    assert b * n_seg == SCAN_ROWS and t % n_seg == 0
    j = t // n_seg
    u3 = u.reshape(b, n_seg, j, SSM_WIDTH).transpose(2, 0, 1, 3).reshape(1, j * SCAN_ROWS, SSM_WIDTH)
    zeros = jnp.zeros((1, SCAN_ROWS, N_STATE), F32)
    f_re, f_im = _s5_call(u3, zeros, zeros, a_re, a_im, bbd, (), 64, with_y=False)
    i_re, i_im = _s5_seg_init(f_re[0], f_im[0], a_re, a_im, j, b, n_seg)
    s3, h_re, h_im = _s5_call(u3, i_re[None], i_im[None], a_re, a_im, bbd, tail, 64, with_y=True)
    s = s3.reshape(j, b, n_seg, SSM_WIDTH).transpose(1, 2, 0, 3).reshape(b, t, SSM_WIDTH)
    last = h_re.reshape(b, n_seg, N_STATE)[:, -1], h_im.reshape(b, n_seg, N_STATE)[:, -1]
    return s, last[0], last[1]


def _s5_sample(u, h0_re, h0_im, a_re, a_im, bbd, tail):
    n, j, _ = u.shape
    ng = n // SCAN_ROWS
    u3 = u.reshape(ng, SCAN_ROWS, j, SSM_WIDTH).transpose(0, 2, 1, 3).reshape(ng, j * SCAN_ROWS, SSM_WIDTH)
    s3, h_re, h_im = _s5_call(u3, h0_re.reshape(ng, SCAN_ROWS, N_STATE), h0_im.reshape(ng, SCAN_ROWS, N_STATE),
                              a_re, a_im, bbd, tail, j, with_y=True)
    s = s3.reshape(ng, j, SCAN_ROWS, SSM_WIDTH).transpose(0, 2, 1, 3).reshape(n, j, SSM_WIDTH)
    return s, h_re.reshape(n, N_STATE), h_im.reshape(n, N_STATE)


def _top2_sum(a, b, c, d):
    m1, n1 = jnp.maximum(a, b), jnp.minimum(a, b)
    m2, n2 = jnp.maximum(c, d), jnp.minimum(c, d)
    return jnp.maximum(m1, m2) + jnp.maximum(jnp.minimum(m1, m2), jnp.maximum(n1, n2))


def _route_rows(logit_rows, rbias_ref):
    scores = [jax.nn.sigmoid(r) for r in logit_rows]
    sel = [scores[e] + rbias_ref[e] for e in range(N_EXPERTS)]
    gs = [_top2_sum(*sel[EXPERTS_PER_GROUP * g:EXPERTS_PER_GROUP * (g + 1)]) for g in range(N_EXPERT_GROUPS)]
    best_v, best_g = gs[0], jnp.zeros(gs[0].shape, jnp.int32)
    for g in range(1, N_EXPERT_GROUPS):
        upd = gs[g] > best_v
        best_g = jnp.where(upd, g, best_g)
        best_v = jnp.where(upd, gs[g], best_v)
    picked = []
    for e in range(N_EXPERTS):
        g = e // EXPERTS_PER_GROUP
        rank = jnp.zeros(sel[e].shape, jnp.int32)
        for o in range(EXPERTS_PER_GROUP * g, EXPERTS_PER_GROUP * (g + 1)):
            if o != e:
                ahead = (sel[o] >= sel[e]) if o < e else (sel[o] > sel[e])
                rank = rank + ahead.astype(jnp.int32)
        on = jnp.logical_and(best_g == g, rank < 2)
        picked.append(jnp.where(on, scores[e], 0.0))
    den = picked[0]
    for e in range(1, N_EXPERTS):
        den = den + picked[e]
    return [w / den for w in picked]


def _outproj_kernel(rbias_ref, a_ref, s_ref, x_ref, wo_ref, g_ref, b_ref, rw_ref, x1_ref, gates_ref):
    mix = (jnp.dot(a_ref[...].astype(BF16), wo_ref[0:ATT_WIDTH, :], preferred_element_type=F32)
           + jnp.dot(s_ref[...].astype(BF16), wo_ref[ATT_WIDTH:, :], preferred_element_type=F32))
    x1 = _layer_norm(DN_ALPHA * x_ref[...] + mix, g_ref[...], b_ref[...])
    x1_ref[...] = x1
    xh = x1.astype(BF16)
    xl = (x1 - xh.astype(F32)).astype(BF16)
    rwh, rwl = rw_ref[0:D_MODEL, :], rw_ref[D_MODEL:, :]
    logits = (jnp.dot(xh, rwh, preferred_element_type=F32)
              + jnp.dot(xl, rwh, preferred_element_type=F32)
              + jnp.dot(xh, rwl, preferred_element_type=F32))
    lt = logits.T
    gate_rows = _route_rows([lt[e:e + 1, :] for e in range(N_EXPERTS)], rbias_ref)
    tm = lt.shape[1]
    r_i = lax.broadcasted_iota(jnp.int32, (N_EXPERTS, tm), 0)
    gt = jnp.zeros((N_EXPERTS, tm), F32)
    for e in range(N_EXPERTS):
        gt = jnp.where(r_i == e, jnp.broadcast_to(gate_rows[e], (N_EXPERTS, tm)), gt)
    gt = jnp.concatenate([gt, jnp.zeros((LANES - N_EXPERTS, tm), F32)], axis=0)
    gates_ref[...] = gt.T


def _out_proj(a2d, s, x2d, wo_bf16, ln_g, ln_b, rw_cat, router_bias):
    t = x2d.shape[0]
    tm = min(512, t)
    row = lambda w: pl.BlockSpec((tm, w), lambda i: (i, 0))
    const = lambda shape: pl.BlockSpec(shape, lambda i: (0, 0))
    s_spec = row(SSM_WIDTH)
    return pl.pallas_call(
        _outproj_kernel,
        out_shape=(jax.ShapeDtypeStruct((t, D_MODEL), F32), jax.ShapeDtypeStruct((t, LANES), F32)),
        grid=(t // tm,),
        in_specs=[pl.BlockSpec(memory_space=pltpu.SMEM), row(ATT_WIDTH), s_spec, row(D_MODEL),
                  const((D_MODEL, D_MODEL)), const((1, D_MODEL)), const((1, D_MODEL)),
                  const((2 * D_MODEL, LANES))],
        out_specs=(row(D_MODEL), row(LANES)), compiler_params=_cparams("parallel"), name="out_proj_router",
    )(router_bias, a2d, s, x2d, wo_bf16, ln_g.reshape(1, -1), ln_b.reshape(1, -1), rw_cat)


def _moe_kernel(x_ref, gates_ref, wg_ref, wu_ref, wd_ref, g_ref, b_ref, o_ref, xb_scr, acc_scr):
    e = pl.program_id(1)

    @pl.when(e == 0)
    def _():
        xb_scr[...] = x_ref[...].astype(BF16)
        acc_scr[...] = jnp.zeros(acc_scr.shape, F32)

    xb = xb_scr[...]
    hg = jnp.dot(xb, wg_ref[...].astype(BF16), preferred_element_type=F32)
    hu = jnp.dot(xb, wu_ref[...].astype(BF16), preferred_element_type=F32)
    h = jax.nn.silu(hg) * hu
    y = jnp.dot(h.astype(BF16), wd_ref[...].astype(BF16), preferred_element_type=F32)
    gates = gates_ref[...]
    lane = lax.broadcasted_iota(jnp.int32, gates.shape, 1)
    gate = jnp.sum(jnp.where(lane == e, gates, 0.0), axis=-1, keepdims=True)
    acc_scr[...] = acc_scr[...] + gate * y

    @pl.when(e == pl.num_programs(1) - 1)
    def _():
        o_ref[...] = _layer_norm(DN_ALPHA * x_ref[...] + acc_scr[...], g_ref[...], b_ref[...])


def _moe(x2d, gates, wg, wu, wd, layer, ln_g, ln_b):
    t = x2d.shape[0]
    tm = min(MOE_TOKEN_TILE, t)
    row = lambda w: pl.BlockSpec((tm, w), lambda i, e: (i, 0))
    const = lambda shape: pl.BlockSpec(shape, lambda i, e: (0, 0))
    return pl.pallas_call(
        _moe_kernel, out_shape=jax.ShapeDtypeStruct((t, D_MODEL), F32), grid=(t // tm, N_EXPERTS),
        in_specs=[row(D_MODEL), row(LANES),
                  pl.BlockSpec((None, None, D_MODEL, D_EXPERT), lambda i, e: (layer, e, 0, 0)),
                  pl.BlockSpec((None, None, D_MODEL, D_EXPERT), lambda i, e: (layer, e, 0, 0)),
                  pl.BlockSpec((None, None, D_EXPERT, D_MODEL), lambda i, e: (layer, e, 0, 0)),
                  const((1, D_MODEL)), const((1, D_MODEL))],
        out_specs=row(D_MODEL),
        scratch_shapes=[pltpu.VMEM((tm, D_MODEL), BF16), pltpu.VMEM((tm, D_MODEL), F32)],
        compiler_params=_cparams("parallel", "arbitrary"), name="moe_ln",
    )(x2d, gates, wg, wu, wd, ln_g.reshape(1, -1), ln_b.reshape(1, -1))


def kernel(x_prompt, x_sample, cache_k, cache_v, state_ssm_re, state_ssm_im, page_table, w_in, w_out, ln1_g, ln1_b, ln2_g, ln2_b, lambda_q1, lambda_k1, lambda_q2, lambda_k2, subln_g, ssm_lambda_re, ssm_lambda_im, ssm_log_dt, ssm_b_re, ssm_b_im, ssm_c_re, ssm_c_im, ssm_d, ssm_w_glu, ssm_b_glu, router_w, router_bias, moe_w_gate, moe_w_up, moe_w_down):
    b, t, _ = x_prompt.shape
    n, dec_seq, _ = x_sample.shape
    depth = w_in.shape[0]
    n_pool = cache_k.shape[1]

    slopes = 2.0 ** (-8.0 * jnp.arange(1, N_HEADS + 1, dtype=F32) / N_HEADS)
    col = jnp.arange(LANES)
    cslope = jnp.where(col < 2 * N_HEADS * dec_seq, slopes[jnp.minimum(col // (2 * dec_seq), N_HEADS - 1)], 0.0)
    cslope = cslope.reshape(1, LANES).astype(F32)
    rw_pad = jnp.zeros((D_MODEL, LANES), F32).at[:, :N_EXPERTS].set(router_w)
    rw_hi = rw_pad.astype(BF16)
    rw_lo = (rw_pad - rw_hi.astype(F32)).astype(BF16)
    rw_cat = jnp.concatenate([rw_hi, rw_lo], axis=0)

    pages_k = cache_k.reshape(depth * n_pool, PAGE_SIZE * N_HEADS, V_DIM)
    pages_v = cache_v.reshape(depth * n_pool, PAGE_SIZE * N_HEADS, V_DIM)

    yp = x_prompt.reshape(b * t, D_MODEL)
    ys = x_sample.reshape(n * dec_seq, D_MODEL)
    outs = [[] for _ in range(8)]
    kbuf = jnp.zeros((depth * b * t * N_HEADS, V_DIM), F32)
    vbuf = jnp.zeros((depth * b * t * N_HEADS, V_DIM), F32)
    for l in range(depth):
        lam_init = 0.8 - 0.6 * math.exp(-0.3 * l)
        a_re, a_im, bb_re, bb_im, lam = _layer_prep(
            ssm_lambda_re[l], ssm_lambda_im[l], ssm_log_dt[l], ssm_b_re[l], ssm_b_im[l],
            lambda_q1[l], lambda_k1[l], lambda_q2[l], lambda_k2[l], lam_init)
        a_re, a_im = a_re.reshape(1, N_STATE), a_im.reshape(1, N_STATE)
        bbd = jnp.concatenate([_block_diag_in(bb_re), _block_diag_in(bb_im)], axis=1).astype(BF16)
        cbd = jnp.concatenate([_block_diag_out(ssm_c_re[l]), -_block_diag_out(ssm_c_im[l])], axis=0).astype(BF16)
        tail = (cbd, ssm_d[l].reshape(1, -1), ssm_w_glu[l].astype(BF16), ssm_b_glu[l].reshape(1, -1))
        w_in_b, w_out_b = w_in[l].astype(BF16), w_out[l].astype(BF16)

        def finish(x2d, a2d, s2d):
            x1, gates = _out_proj(a2d, s2d, x2d, w_out_b, ln1_g[l], ln1_b[l], rw_cat, router_bias)
            return _moe(x1, gates, moe_w_gate, moe_w_up, moe_w_down, l, ln2_g[l], ln2_b[l])

        u, q, kb, vt, kbuf, vbuf = _in_proj_prompt(yp, w_in_b, w_in_b[:, 3 * 512:].T, kbuf, vbuf, l, b)
        a = _attn_prompt(q.reshape(b, t, -1), kb.reshape(b, t, -1), vt, slopes, lam, subln_g[l], lam_init)
        s, hr, hi = _s5_prompt(u.reshape(b, t, -1), a_re, a_im, bbd, tail)
        yp = finish(yp, a.reshape(b * t, -1), s.reshape(b * t, -1))
        outs[2].append(hr.reshape(b, N_SSM_GROUPS, SSM_STATE))
        outs[3].append(hi.reshape(b, N_SSM_GROUPS, SSM_STATE))

        u, q, k, v = _in_proj_sample(ys, w_in_b)
        a = _attn_sample(q.reshape(n, dec_seq, -1), k.reshape(n, dec_seq, -1), v.reshape(n, dec_seq, -1),
                         pages_k, pages_v, l * n_pool, page_table, cslope, lam, subln_g[l], lam_init)
        s, hr, hi = _s5_sample(u.reshape(n, dec_seq, -1), state_ssm_re[l].reshape(n, N_STATE),
                               state_ssm_im[l].reshape(n, N_STATE), a_re, a_im, bbd, tail)
        ys = finish(ys, a.reshape(n * dec_seq, -1), s.reshape(n * dec_seq, -1))
        outs[4].append(k.reshape(n, dec_seq, N_HEADS, V_DIM))
        outs[5].append(v.reshape(n, dec_seq, N_HEADS, V_DIM))
        outs[6].append(hr.reshape(n, N_SSM_GROUPS, SSM_STATE))
        outs[7].append(hi.reshape(n, N_SSM_GROUPS, SSM_STATE))

    outs[0] = kbuf.reshape(depth, b, t, N_HEADS, V_DIM)
    outs[1] = vbuf.reshape(depth, b, t, N_HEADS, V_DIM)
    outs = [o if isinstance(o, jax.Array) else jnp.stack(o) for o in outs]
    return (yp.reshape(b, t, D_MODEL), ys.reshape(n, dec_seq, D_MODEL)) + tuple(outs)
```

```python
import functools
import math

import jax
import jax.numpy as jnp
from jax import lax
from jax.experimental import pallas as pl
from jax.experimental.pallas import tpu as pltpu

F32 = jnp.float32
BF16 = jnp.bfloat16

D_MODEL = 1024
ATT_WIDTH = 512
SSM_WIDTH = 512
HEAD_DIM = 64
V_DIM = 128
N_HEADS = 4
SSM_GROUP = 16
N_SSM_GROUPS = 32
SSM_STATE = 64
N_STATE = N_SSM_GROUPS * SSM_STATE
N_EXPERTS = 16
EXPERTS_PER_GROUP = 4
N_EXPERT_GROUPS = 4
D_EXPERT = 512
PAGE_SIZE = 128
DEPTH = 2
DN_ALPHA = (2 * DEPTH) ** 0.25
LN_EPS = 1e-5
RMS_EPS = 1e-5
ATTN_SCALE = HEAD_DIM ** -0.5

SUBLANES = 8
BF16_SUBLANES = 16
LANES = 128
SCAN_ROWS = SUBLANES
MOE_TOKEN_TILE = 1024
VMEM_LIMIT = 52 * 1024 * 1024


def _cparams(*sem):
    return pltpu.CompilerParams(dimension_semantics=sem, vmem_limit_bytes=VMEM_LIMIT)


def _layer_norm(z, g, b):
    mu = jnp.mean(z, -1, keepdims=True)
    var = jnp.mean(jnp.square(z - mu), -1, keepdims=True)
    return (z - mu) * lax.rsqrt(var + LN_EPS) * g + b


def _prep_kernel(lre_ref, lim_ref, ldt_ref, bre_ref, bim_ref, lq1_ref, lk1_ref, lq2_ref, lk2_ref,
                 are_ref, aim_ref, bbre_ref, bbim_ref, lam_ref, *, lam_init):
    lr, li = lre_ref[...], lim_ref[...]
    dt = jnp.exp(ldt_ref[...])
    mag = jnp.exp(lr * dt)
    a_re, a_im = mag * jnp.cos(li * dt), mag * jnp.sin(li * dt)
    den = lr * lr + li * li
    nr = a_re - 1.0
    coef_re = (nr * lr + a_im * li) / den
    coef_im = (a_im * lr - nr * li) / den
    br, bi = bre_ref[...], bim_ref[...]
    bbre_ref[...] = coef_re[..., None] * br - coef_im[..., None] * bi
    bbim_ref[...] = coef_re[..., None] * bi + coef_im[..., None] * br
    are_ref[...] = a_re
    aim_ref[...] = a_im
    s1 = jnp.sum(lq1_ref[...] * lk1_ref[...], axis=-1, keepdims=True)
    s2 = jnp.sum(lq2_ref[...] * lk2_ref[...], axis=-1, keepdims=True)
    lam_ref[...] = jnp.exp(s1) - jnp.exp(s2) + lam_init


def _layer_prep(lam_re, lam_im, log_dt, b_re, b_im, lq1, lk1, lq2, lk2, lam_init):
    g, p, c = N_SSM_GROUPS, SSM_STATE, SSM_GROUP
    out_shape = (jax.ShapeDtypeStruct((g, p), F32), jax.ShapeDtypeStruct((g, p), F32),
                 jax.ShapeDtypeStruct((g, p, c), F32), jax.ShapeDtypeStruct((g, p, c), F32),
                 jax.ShapeDtypeStruct((1, 1), F32))
    return pl.pallas_call(
        functools.partial(_prep_kernel, lam_init=lam_init), out_shape=out_shape, name="layer_prep",
    )(lam_re, lam_im, log_dt.reshape(g, 1), b_re, b_im,
      lq1.reshape(1, HEAD_DIM), lk1.reshape(1, HEAD_DIM), lq2.reshape(1, HEAD_DIM), lk2.reshape(1, HEAD_DIM))


def _block_diag_in(bb):
    eye = jnp.eye(N_SSM_GROUPS, dtype=bb.dtype)
    m = jnp.transpose(bb, (0, 2, 1))[:, :, None, :] * eye[:, None, :, None]
    return m.reshape(SSM_WIDTH, N_STATE)


def _block_diag_out(c):
    eye = jnp.eye(N_SSM_GROUPS, dtype=c.dtype)
    m = jnp.transpose(c, (0, 2, 1))[:, :, None, :] * eye[:, None, :, None]
    return m.reshape(N_STATE, SSM_WIDTH)


def _proj_sample_kernel(x_ref, w_ref, u_ref, q_ref, k_ref, v_ref):
    xb = x_ref[...].astype(BF16)
    for i, o_ref in enumerate((u_ref, q_ref, k_ref, v_ref)):
        o_ref[...] = jnp.dot(xb, w_ref[:, i * 512:(i + 1) * 512], preferred_element_type=F32)


def _in_proj_sample(x2d, w_bf16):
    t = x2d.shape[0]
    tm = min(512, t)
    out = jax.ShapeDtypeStruct((t, 512), F32)
    row = pl.BlockSpec((tm, 512), lambda i: (i, 0))
    return pl.pallas_call(
        _proj_sample_kernel, out_shape=(out,) * 4, grid=(t // tm,),
        in_specs=[pl.BlockSpec((tm, D_MODEL), lambda i: (i, 0)),
                  pl.BlockSpec((D_MODEL, 4 * 512), lambda i: (0, 0))],
        out_specs=(row,) * 4, compiler_params=_cparams("parallel"), name="in_proj_sample",
    )(x2d, w_bf16)


def _proj_prompt_kernel(x_ref, w_ref, wvt_ref, kbuf_in, vbuf_in, u_ref, q_ref, kb_ref, vt_ref, k_ref, v_ref):
    del kbuf_in, vbuf_in
    xb = x_ref[...].astype(BF16)
    tm = xb.shape[0]
    col = lambda i: jnp.dot(xb, w_ref[:, i * 512:(i + 1) * 512], preferred_element_type=F32)
    u_ref[...] = col(0)
    q_ref[...] = col(1)
    k, v = col(2), col(3)
    kb_ref[...] = k.astype(BF16)
    for h in range(N_HEADS):
        k_ref[pl.ds(h, tm, stride=N_HEADS), :] = k[:, h * V_DIM:(h + 1) * V_DIM]
        v_ref[pl.ds(h, tm, stride=N_HEADS), :] = v[:, h * V_DIM:(h + 1) * V_DIM]
    vt = lax.dot_general(wvt_ref[...], xb, (((1,), (1,)), ((), ())), preferred_element_type=F32)
    vt_ref[...] = vt.astype(BF16)


def _in_proj_prompt(x2d, w_bf16, wv_t, kbuf, vbuf, layer, n_seq):
    t = x2d.shape[0]
    seq_len = t // n_seq
    tm = min(512, seq_len)
    assert seq_len % tm == 0
    nt, nl = seq_len // tm, t // tm
    row = pl.BlockSpec((tm, 512), lambda i: (i, 0))
    cache_row = pl.BlockSpec((tm * N_HEADS, V_DIM), lambda i: (layer * nl + i, 0))
    anyspec = pl.BlockSpec(memory_space=pl.ANY)
    out_shape = (jax.ShapeDtypeStruct((t, 512), F32), jax.ShapeDtypeStruct((t, 512), F32),
                 jax.ShapeDtypeStruct((t, 512), BF16), jax.ShapeDtypeStruct((n_seq, 512, seq_len), BF16),
                 jax.ShapeDtypeStruct(kbuf.shape, F32), jax.ShapeDtypeStruct(vbuf.shape, F32))
    out_specs = (row, row, row,
                 pl.BlockSpec((None, 512, tm), lambda i: (i // nt, 0, i % nt)), cache_row, cache_row)
    return pl.pallas_call(
        _proj_prompt_kernel, out_shape=out_shape, grid=(nl,),
        in_specs=[pl.BlockSpec((tm, D_MODEL), lambda i: (i, 0)),
                  pl.BlockSpec((D_MODEL, 4 * 512), lambda i: (0, 0)),
                  pl.BlockSpec((512, D_MODEL), lambda i: (0, 0)), anyspec, anyspec],
        out_specs=out_specs, input_output_aliases={3: 4, 4: 5},
        compiler_params=_cparams("parallel"), name="in_proj_prompt",
    )(x2d, w_bf16, wv_t, kbuf, vbuf)


def _sub_ln(a, g, lam_init):
    return a * lax.rsqrt(jnp.mean(a * a, axis=-1, keepdims=True) + RMS_EPS) * g * (1.0 - lam_init)


ALIBI_SPLIT = 64
ATTN_KEY_BLOCK = 1024
ATTN_QUERY_BLOCK = 1024
ATTN_Q_CHUNK = 512


def _attn_prompt_kernel(qi_tab, ki_tab, slopes_ref, lam_ref, q_ref, kb_ref, vt_ref, kfeat_ref, vfeat_ref,
                        g_ref, o_ref, qs_scr, m_scr, acc_scr, *, tq, tk, lam_init):
    h, step = pl.program_id(1), pl.program_id(2)
    qi, ki = qi_tab[step], ki_tab[step]
    slope = slopes_ref[h]

    @pl.when(ki == 0)
    def _():
        qs = q_ref[...] * ATTN_SCALE
        lane = lax.broadcasted_iota(jnp.int32, qs.shape, 1)
        qfeat = jnp.where(lane == 0, ALIBI_SPLIT * slope, jnp.where(lane == 1, slope, 0.0)).astype(BF16)
        qs_scr[0:tq, 0:LANES] = jnp.where(lane < HEAD_DIM, qs, 0.0).astype(BF16)
        qs_scr[tq:2 * tq, 0:LANES] = jnp.where(lane >= HEAD_DIM, qs, 0.0).astype(BF16)
        qs_scr[0:tq, LANES:2 * LANES] = qfeat
        qs_scr[tq:2 * tq, LANES:2 * LANES] = qfeat
        m_scr[...] = jnp.full(m_scr.shape, -jnp.inf, F32)
        acc_scr[...] = jnp.zeros(acc_scr.shape, F32)

    def update(koff):
        kp = jnp.concatenate([kb_ref[...], kfeat_ref[...]], axis=1)
        vpt = jnp.concatenate([vt_ref[...], vfeat_ref[...]], axis=0)
        d = slope * (ki * tk - qi * tq).astype(F32)
        qc = ATTN_Q_CHUNK

        def n_keys(c0):
            return tk if koff is None else max(0, min(tk, (c0 % tq) + qc - koff))

        chunks = [c0 for c0 in range(0, 2 * tq, qc) if n_keys(c0) > 0]

        def scores(c0):
            return lax.dot_general(kp[0:n_keys(c0), :], qs_scr[c0:c0 + qc, :], (((1,), (1,)), ((), ())),
                                   preferred_element_type=F32)

        def softmax_part(s, c0):
            q0 = c0 % tq
            if koff is not None and koff + s.shape[0] - 1 > q0:
                key = lax.broadcasted_iota(jnp.int32, s.shape, 0) + koff
                qry = lax.broadcasted_iota(jnp.int32, s.shape, 1) + q0
                s = jnp.where(qry >= key, s, -jnp.inf)
            m_prev = m_scr[:, c0:c0 + qc]
            m_new = jnp.maximum(m_prev, jnp.max(s, axis=0, keepdims=True) + d)
            m_scr[:, c0:c0 + qc] = m_new
            return jnp.exp(s - (m_new - d)).astype(BF16), jnp.exp(m_prev - m_new)

        def accumulate(c0, p, alpha):
            pv = jnp.dot(vpt[:, 0:p.shape[0]], p, preferred_element_type=F32)
            acc_scr[:, c0:c0 + qc] = alpha * acc_scr[:, c0:c0 + qc] + pv

        n = len(chunks)
        s_q = [scores(chunks[0])] + ([scores(chunks[1])] if n > 1 else [])
        pending = None
        for i, c0 in enumerate(chunks):
            s = s_q.pop(0)
            if i + 2 < n:
                s_q.append(scores(chunks[i + 2]))
            if pending is not None:
                accumulate(*pending)
            pending = (c0,) + softmax_part(s, c0)
        accumulate(*pending)

    rel = ki - qi * (tq // tk)

    @pl.when(rel < 0)
    def _():
        update(None)

    for j in range(tq // tk):
        @pl.when(rel == j)
        def _(j=j):
            update(j * tk)

    @pl.when(rel == tq // tk - 1)
    def _():
        o1 = acc_scr[0:V_DIM, 0:tq] / acc_scr[V_DIM:V_DIM + 1, 0:tq]
        o2 = acc_scr[0:V_DIM, tq:2 * tq] / acc_scr[V_DIM:V_DIM + 1, tq:2 * tq]
        a = (o1 - lam_ref[0, 0] * o2).T
        o_ref[...] = _sub_ln(a, g_ref[...], lam_init)


def _attn_prompt(q, kb, vt, slopes, lam, subln_g, lam_init):
    b, t, _ = q.shape
    tk = min(ATTN_KEY_BLOCK, t)
    tq = min(ATTN_QUERY_BLOCK, t)
    nq, r = t // tq, tq // tk
    pairs = [(qi, ki) for qi in range(nq) for ki in range(r * (qi + 1))]
    qi_tab = jnp.asarray([p[0] for p in pairs], jnp.int32)
    ki_tab = jnp.asarray([p[1] for p in pairs], jnp.int32)
    c = jnp.arange(tk)[:, None]
    lane = jnp.arange(LANES)[None, :]
    kfeat = jnp.where(lane == 0, c // ALIBI_SPLIT, jnp.where(lane == 1, c % ALIBI_SPLIT, 0)).astype(BF16)
    ones_row = jnp.arange(BF16_SUBLANES)[:, None] == 0
    vfeat = jnp.broadcast_to(jnp.where(ones_row, 1.0, 0.0), (BF16_SUBLANES, tk)).astype(BF16)
    kernel = functools.partial(_attn_prompt_kernel, tq=tq, tk=tk, lam_init=lam_init)
    smem = pl.BlockSpec(memory_space=pltpu.SMEM)
    q_spec = pl.BlockSpec((None, tq, V_DIM), lambda b_, h, s, qt, kt: (b_, qt[s], h))
    k_spec = pl.BlockSpec((None, tk, V_DIM), lambda b_, h, s, qt, kt: (b_, kt[s], h))
    vt_spec = pl.BlockSpec((None, V_DIM, tk), lambda b_, h, s, qt, kt: (b_, h, kt[s]))
    const = lambda shape: pl.BlockSpec(shape, lambda b_, h, s, qt, kt: (0, 0))
    grid_spec = pltpu.PrefetchScalarGridSpec(
        num_scalar_prefetch=2, grid=(b, N_HEADS, len(pairs)),
        in_specs=[smem, smem, q_spec, k_spec, vt_spec, const((tk, LANES)), const((BF16_SUBLANES, tk)),
                  const((1, V_DIM))],
        out_specs=q_spec,
        scratch_shapes=[pltpu.VMEM((2 * tq, 2 * LANES), BF16), pltpu.VMEM((1, 2 * tq), F32),
                        pltpu.VMEM((V_DIM + BF16_SUBLANES, 2 * tq), F32)])
    return pl.pallas_call(
        kernel, out_shape=jax.ShapeDtypeStruct((b, t, ATT_WIDTH), F32), grid_spec=grid_spec,
        compiler_params=_cparams("parallel", "parallel", "arbitrary"), name="attn_prompt",
    )(qi_tab, ki_tab, slopes, lam, q, kb, vt, kfeat, vfeat, subln_g.reshape(1, V_DIM))


def _attn_sample_kernel(pt_ref, lam_ref, q_ref, knew_ref, vnew_ref, g_ref, cslope_ref, *rest,
                        n_pages, dec_seq, lam_init):
    del pt_ref
    k_pages = rest[:n_pages]
    v_pages = rest[n_pages:2 * n_pages]
    o_ref, s_scr = rest[2 * n_pages:]
    past = n_pages * PAGE_SIZE
    cols_per_head = 2 * dec_seq
    head_rows = lambda ref, h: ref[pl.ds(h, PAGE_SIZE, stride=N_HEADS), :]

    q = q_ref[...] * ATTN_SCALE
    lane = lax.broadcasted_iota(jnp.int32, (dec_seq, V_DIM), 1)
    qw = []
    for h in range(N_HEADS):
        qh = q[:, h * V_DIM:(h + 1) * V_DIM]
        blocks = [jnp.where(lane < HEAD_DIM, qh, 0.0), jnp.where(lane >= HEAD_DIM, qh, 0.0)]
        if h:
            blocks.insert(0, jnp.zeros((h * cols_per_head, V_DIM), F32))
        blocks.append(jnp.zeros((LANES - (h + 1) * cols_per_head, V_DIM), F32))
        qw.append(jnp.concatenate(blocks, axis=0))
    dn = (((1,), (1,)), ((), ()))

    s_past = None
    for h0 in range(0, N_HEADS, 2):
        keys = jnp.concatenate(
            [jnp.concatenate([head_rows(k_pages[p], h0), head_rows(k_pages[p], h0 + 1)], axis=1)
             for p in range(n_pages)], axis=0)
        w = jnp.concatenate([qw[h0], qw[h0 + 1]], axis=1)
        part = lax.dot_general(keys, w, dn, preferred_element_type=F32)
        s_past = part if s_past is None else s_past + part
    row = lax.broadcasted_iota(jnp.int32, (past, LANES), 0)
    col = lax.broadcasted_iota(jnp.int32, (past, LANES), 1)
    cslope = cslope_ref[...]
    dist = past + col % dec_seq - row
    s_scr[0:past, :] = s_past - cslope * dist.astype(F32)

    knew = knew_ref[...]
    s_new = None
    for h in range(N_HEADS):
        part = lax.dot_general(knew[:, h * V_DIM:(h + 1) * V_DIM], qw[h], dn, preferred_element_type=F32)
        s_new = part if s_new is None else s_new + part
    row = lax.broadcasted_iota(jnp.int32, (dec_seq, LANES), 0)
    col = lax.broadcasted_iota(jnp.int32, (dec_seq, LANES), 1)
    dist = col % dec_seq - row
    s_scr[past:past + dec_seq, :] = jnp.where(dist >= 0, s_new - cslope * dist.astype(F32), -jnp.inf)
    s_scr[past + dec_seq:past + PAGE_SIZE, :] = jnp.full((PAGE_SIZE - dec_seq, LANES), -jnp.inf, F32)

    s_all = s_scr[...]
    m = jnp.max(s_all, axis=0, keepdims=True)
    p_all = jnp.exp(s_all - m)
    l = jnp.sum(p_all, axis=0, keepdims=True)
    c_i = lax.broadcasted_iota(jnp.int32, (1, LANES), 1)
    sign = jnp.where((c_i // dec_seq) % 2 == 0, 1.0, -lam_ref[0, 0])
    s_scr[...] = p_all * (sign / l)

    vnew = vnew_ref[...]
    vzero = jnp.zeros((PAGE_SIZE - dec_seq, V_DIM), F32)
    pts = [s_scr[p * PAGE_SIZE:(p + 1) * PAGE_SIZE, :].T for p in range(n_pages + 1)]
    for h in range(N_HEADS):
        w_h = jnp.concatenate([pt[h * cols_per_head:(h + 1) * cols_per_head, :] for pt in pts], axis=1)
        v_h = jnp.concatenate([head_rows(v_pages[p], h) for p in range(n_pages)]
                              + [vnew[:, h * V_DIM:(h + 1) * V_DIM], vzero], axis=0)
        out = jnp.dot(w_h, v_h, preferred_element_type=F32)
        a = out[0:dec_seq, :] + out[dec_seq:2 * dec_seq, :]
        o_ref[:, h * V_DIM:(h + 1) * V_DIM] = _sub_ln(a, g_ref[...], lam_init)


def _attn_sample(q, k, v, k_pages, v_pages, page_base, page_table, cslope, lam, subln_g, lam_init):
    n, dec_seq, _ = q.shape
    n_pages = page_table.shape[1]
    assert 2 * N_HEADS * dec_seq <= LANES and dec_seq % SUBLANES == 0 and N_HEADS % 2 == 0
    kernel = functools.partial(_attn_sample_kernel, n_pages=n_pages, dec_seq=dec_seq, lam_init=lam_init)
    tok = pl.BlockSpec((None, dec_seq, ATT_WIDTH), lambda i, pt: (i, 0, 0))

    def page_spec(p):
        return pl.BlockSpec((None, PAGE_SIZE * N_HEADS, V_DIM),
                            lambda i, pt: (page_base + pt[i * n_pages + p], 0, 0))

    grid_spec = pltpu.PrefetchScalarGridSpec(
        num_scalar_prefetch=1, grid=(n,),
        in_specs=[pl.BlockSpec(memory_space=pltpu.SMEM), tok, tok, tok,
                  pl.BlockSpec((1, V_DIM), lambda i, pt: (0, 0)),
                  pl.BlockSpec((1, LANES), lambda i, pt: (0, 0))]
                 + [page_spec(p) for p in range(n_pages)] * 2,
        out_specs=tok,
        scratch_shapes=[pltpu.VMEM(((n_pages + 1) * PAGE_SIZE, LANES), F32)])
    return pl.pallas_call(
        kernel, out_shape=jax.ShapeDtypeStruct((n, dec_seq, ATT_WIDTH), F32), grid_spec=grid_spec,
        compiler_params=_cparams("arbitrary"), name="attn_sample",
    )(page_table.reshape(-1), lam, q, k, v, subln_g.reshape(1, V_DIM), cslope,
      *([k_pages] * n_pages), *([v_pages] * n_pages))


def _s5_scan_chunk(bu_scr, hs_scr, hre_scr, him_scr, are_ref, aim_ref, steps, lane_chunk):
    for c in range(N_STATE // lane_chunk):
        lo = c * lane_chunk
        are = jnp.broadcast_to(are_ref[:, lo:lo + lane_chunk], (SCAN_ROWS, lane_chunk))
        aim = jnp.broadcast_to(aim_ref[:, lo:lo + lane_chunk], (SCAN_ROWS, lane_chunk))

        def body(j, carry, lo=lo, are=are, aim=aim):
            hre, him = carry
            r0 = pl.multiple_of(j * SCAN_ROWS, SCAN_ROWS)
            bre = bu_scr[pl.ds(r0, SCAN_ROWS), lo:lo + lane_chunk]
            bim = bu_scr[pl.ds(r0, SCAN_ROWS), N_STATE + lo:N_STATE + lo + lane_chunk]
            nre = are * hre - aim * him + bre
            nim = are * him + aim * hre + bim
            if hs_scr is not None:
                hs_scr[pl.ds(r0, SCAN_ROWS), lo:lo + lane_chunk] = nre
                hs_scr[pl.ds(r0, SCAN_ROWS), N_STATE + lo:N_STATE + lo + lane_chunk] = nim
            return nre, nim

        hre, him = lax.fori_loop(0, steps, body,
                                 (hre_scr[:, lo:lo + lane_chunk], him_scr[:, lo:lo + lane_chunk]),
                                 unroll=min(4, steps))
        hre_scr[:, lo:lo + lane_chunk] = hre
        him_scr[:, lo:lo + lane_chunk] = him


def _s5_kernel(u_ref, h0re_ref, h0im_ref, are_ref, aim_ref, bbd_ref, *rest, steps, with_y):
    if with_y:
        cbd_ref, d_ref, wglu_ref, bglu_ref, s_ref, hre_ref, him_ref, hre_scr, him_scr, bu_scr, hs_scr = rest
    else:
        hre_ref, him_ref, hre_scr, him_scr, bu_scr = rest
        hs_scr = None
    c = pl.program_id(1)

    @pl.when(c == 0)
    def _():
        hre_scr[...] = h0re_ref[...]
        him_scr[...] = h0im_ref[...]

    u = u_ref[...]
    ub = u.astype(BF16)
    hu, hst = SSM_WIDTH // 2, N_STATE // 2
    for half in range(2):
        for part in range(2):
            cols = slice(part * N_STATE + half * hst, part * N_STATE + (half + 1) * hst)
            bu_scr[:, cols] = jnp.dot(ub[:, half * hu:(half + 1) * hu], bbd_ref[half * hu:(half + 1) * hu, cols],
                                      preferred_element_type=F32)
    _s5_scan_chunk(bu_scr, hs_scr, hre_scr, him_scr, are_ref, aim_ref, steps, lane_chunk=1024)
    if with_y:
        y_halves = []
        for half in range(2):
            ycols = slice(half * hu, (half + 1) * hu)
            acc = None
            for part in range(2):
                rows = slice(part * N_STATE + half * hst, part * N_STATE + (half + 1) * hst)
                term = jnp.dot(hs_scr[:, rows].astype(BF16), cbd_ref[rows, ycols], preferred_element_type=F32)
                acc = term if acc is None else acc + term
            y_halves.append(acc)
        y = jnp.concatenate(y_halves, axis=1) + d_ref[...] * u
        g = jax.nn.gelu(y)
        gate = jnp.dot(g.astype(BF16), wglu_ref[...], preferred_element_type=F32) + bglu_ref[...]
        s_ref[...] = g * jax.nn.sigmoid(gate)

    @pl.when(c == pl.num_programs(1) - 1)
    def _():
        hre_ref[...] = hre_scr[...]
        him_ref[...] = him_scr[...]


def _s5_call(u3, h0re, h0im, a_re, a_im, bbd, tail, steps_per_chunk, with_y):
    ng, rows, _ = u3.shape
    j_total = rows // SCAN_ROWS
    steps = min(steps_per_chunk, j_total)
    nc = j_total // steps
    blk = steps * SCAN_ROWS
    const = lambda shape: pl.BlockSpec(shape, lambda g, c: (0,) * len(shape))
    st_spec = pl.BlockSpec((None, SCAN_ROWS, N_STATE), lambda g, c: (g, 0, 0))
    u_spec = pl.BlockSpec((None, blk, SSM_WIDTH), lambda g, c: (g, c, 0))
    st_shape = jax.ShapeDtypeStruct((ng, SCAN_ROWS, N_STATE), F32)
    in_specs = [u_spec, st_spec, st_spec, const((1, N_STATE)), const((1, N_STATE)),
                const((SSM_WIDTH, 2 * N_STATE))]
    scratch = [pltpu.VMEM((SCAN_ROWS, N_STATE), F32), pltpu.VMEM((SCAN_ROWS, N_STATE), F32),
               pltpu.VMEM((blk, 2 * N_STATE), F32)]
    if with_y:
        in_specs += [const((2 * N_STATE, SSM_WIDTH)), const((1, SSM_WIDTH)),
                     const((SSM_WIDTH, SSM_WIDTH)), const((1, SSM_WIDTH))]
        out_shape = (jax.ShapeDtypeStruct((ng, rows, SSM_WIDTH), F32), st_shape, st_shape)
        out_specs = (u_spec, st_spec, st_spec)
        scratch.append(pltpu.VMEM((blk, 2 * N_STATE), F32))
    else:
        out_shape = (st_shape, st_shape)
        out_specs = (st_spec, st_spec)
    return pl.pallas_call(
        functools.partial(_s5_kernel, steps=steps, with_y=with_y),
        out_shape=out_shape, grid=(ng, nc), in_specs=in_specs, out_specs=out_specs,
        scratch_shapes=scratch, compiler_params=_cparams("parallel", "arbitrary"),
        name="s5_scan_glu" if with_y else "s5_scan_state",
    )(u3, h0re, h0im, a_re, a_im, bbd, *tail)


def _s5_seg_init_kernel(fre_ref, fim_ref, are_ref, aim_ref, ire_ref, iim_ref, *, seg_len, n_seq, n_seg):
    pr, pi = are_ref[...], aim_ref[...]
    ar, ai = None, None
    e = seg_len
    while e:
        if e & 1:
            ar, ai = (pr, pi) if ar is None else (ar * pr - ai * pi, ar * pi + ai * pr)
        e >>= 1
        if e:
            pr, pi = pr * pr - pi * pi, 2.0 * pr * pi
    for n in range(n_seq):
        hr = jnp.zeros((1, N_STATE), F32)
        hi = jnp.zeros((1, N_STATE), F32)
        for s in range(n_seg):
            r = n * n_seg + s
            ire_ref[r:r + 1, :] = hr
            iim_ref[r:r + 1, :] = hi
            fr, fi = fre_ref[r:r + 1, :], fim_ref[r:r + 1, :]
            hr, hi = ar * hr - ai * hi + fr, ar * hi + ai * hr + fi


def _s5_seg_init(f_re, f_im, a_re, a_im, seg_len, n_seq, n_seg):
    shape = jax.ShapeDtypeStruct(f_re.shape, F32)
    return pl.pallas_call(
        functools.partial(_s5_seg_init_kernel, seg_len=seg_len, n_seq=n_seq, n_seg=n_seg),
        out_shape=(shape, shape), name="s5_seg_init",
    )(f_re, f_im, a_re, a_im)


def _s5_prompt(u, a_re, a_im, bbd, tail):
    b, t, _ = u.shape
    n_seg = SCAN_ROWS // b
    assert b * n_seg == SCAN_ROWS and t % n_seg == 0
    j = t // n_seg
    u3 = u.reshape(b, n_seg, j, SSM_WIDTH).transpose(2, 0, 1, 3).reshape(1, j * SCAN_ROWS, SSM_WIDTH)
    zeros = jnp.zeros((1, SCAN_ROWS, N_STATE), F32)
    f_re, f_im = _s5_call(u3, zeros, zeros, a_re, a_im, bbd, (), 64, with_y=False)
    i_re, i_im = _s5_seg_init(f_re[0], f_im[0], a_re, a_im, j, b, n_seg)
    s3, h_re, h_im = _s5_call(u3, i_re[None], i_im[None], a_re, a_im, bbd, tail, 64, with_y=True)
    s = s3.reshape(j, b, n_seg, SSM_WIDTH).transpose(1, 2, 0, 3).reshape(b, t, SSM_WIDTH)
    last = h_re.reshape(b, n_seg, N_STATE)[:, -1], h_im.reshape(b, n_seg, N_STATE)[:, -1]
    return s, last[0], last[1]


def _s5_sample(u, h0_re, h0_im, a_re, a_im, bbd, tail):
    n, j, _ = u.shape
    ng = n // SCAN_ROWS
    u3 = u.reshape(ng, SCAN_ROWS, j, SSM_WIDTH).transpose(0, 2, 1, 3).reshape(ng, j * SCAN_ROWS, SSM_WIDTH)
    s3, h_re, h_im = _s5_call(u3, h0_re.reshape(ng, SCAN_ROWS, N_STATE), h0_im.reshape(ng, SCAN_ROWS, N_STATE),
                              a_re, a_im, bbd, tail, j, with_y=True)
    s = s3.reshape(ng, j, SCAN_ROWS, SSM_WIDTH).transpose(0, 2, 1, 3).reshape(n, j, SSM_WIDTH)
    return s, h_re.reshape(n, N_STATE), h_im.reshape(n, N_STATE)


def _top2_sum(a, b, c, d):
    m1, n1 = jnp.maximum(a, b), jnp.minimum(a, b)
    m2, n2 = jnp.maximum(c, d), jnp.minimum(c, d)
    return jnp.maximum(m1, m2) + jnp.maximum(jnp.minimum(m1, m2), jnp.maximum(n1, n2))


def _route_rows(logit_rows, rbias_ref):
    scores = [jax.nn.sigmoid(r) for r in logit_rows]
    sel = [scores[e] + rbias_ref[e] for e in range(N_EXPERTS)]
    gs = [_top2_sum(*sel[EXPERTS_PER_GROUP * g:EXPERTS_PER_GROUP * (g + 1)]) for g in range(N_EXPERT_GROUPS)]
    best_v, best_g = gs[0], jnp.zeros(gs[0].shape, jnp.int32)
    for g in range(1, N_EXPERT_GROUPS):
        upd = gs[g] > best_v
        best_g = jnp.where(upd, g, best_g)
        best_v = jnp.where(upd, gs[g], best_v)
    picked = []
    for e in range(N_EXPERTS):
        g = e // EXPERTS_PER_GROUP
        rank = jnp.zeros(sel[e].shape, jnp.int32)
        for o in range(EXPERTS_PER_GROUP * g, EXPERTS_PER_GROUP * (g + 1)):
            if o != e:
                ahead = (sel[o] >= sel[e]) if o < e else (sel[o] > sel[e])
                rank = rank + ahead.astype(jnp.int32)
        on = jnp.logical_and(best_g == g, rank < 2)
        picked.append(jnp.where(on, scores[e], 0.0))
    den = picked[0]
    for e in range(1, N_EXPERTS):
        den = den + picked[e]
    return [w / den for w in picked]


def _outproj_kernel(rbias_ref, a_ref, s_ref, x_ref, wo_ref, g_ref, b_ref, rw_ref, x1_ref, gates_ref):
    mix = (jnp.dot(a_ref[...].astype(BF16), wo_ref[0:ATT_WIDTH, :], preferred_element_type=F32)
           + jnp.dot(s_ref[...].astype(BF16), wo_ref[ATT_WIDTH:, :], preferred_element_type=F32))
    x1 = _layer_norm(DN_ALPHA * x_ref[...] + mix, g_ref[...], b_ref[...])
    x1_ref[...] = x1
    xh = x1.astype(BF16)
    xl = (x1 - xh.astype(F32)).astype(BF16)
    rwh, rwl = rw_ref[0:D_MODEL, :], rw_ref[D_MODEL:, :]
    logits = (jnp.dot(xh, rwh, preferred_element_type=F32)
              + jnp.dot(xl, rwh, preferred_element_type=F32)
              + jnp.dot(xh, rwl, preferred_element_type=F32))
    lt = logits.T
    gate_rows = _route_rows([lt[e:e + 1, :] for e in range(N_EXPERTS)], rbias_ref)
    tm = lt.shape[1]
    r_i = lax.broadcasted_iota(jnp.int32, (N_EXPERTS, tm), 0)
    gt = jnp.zeros((N_EXPERTS, tm), F32)
    for e in range(N_EXPERTS):
        gt = jnp.where(r_i == e, jnp.broadcast_to(gate_rows[e], (N_EXPERTS, tm)), gt)
    gt = jnp.concatenate([gt, jnp.zeros((LANES - N_EXPERTS, tm), F32)], axis=0)
    gates_ref[...] = gt.T


def _out_proj(a2d, s, x2d, wo_bf16, ln_g, ln_b, rw_cat, router_bias):
    t = x2d.shape[0]
    tm = min(512, t)
    row = lambda w: pl.BlockSpec((tm, w), lambda i: (i, 0))
    const = lambda shape: pl.BlockSpec(shape, lambda i: (0, 0))
    s_spec = row(SSM_WIDTH)
    return pl.pallas_call(
        _outproj_kernel,
        out_shape=(jax.ShapeDtypeStruct((t, D_MODEL), F32), jax.ShapeDtypeStruct((t, LANES), F32)),
        grid=(t // tm,),
        in_specs=[pl.BlockSpec(memory_space=pltpu.SMEM), row(ATT_WIDTH), s_spec, row(D_MODEL),
                  const((D_MODEL, D_MODEL)), const((1, D_MODEL)), const((1, D_MODEL)),
                  const((2 * D_MODEL, LANES))],
        out_specs=(row(D_MODEL), row(LANES)), compiler_params=_cparams("parallel"), name="out_proj_router",
    )(router_bias, a2d, s, x2d, wo_bf16, ln_g.reshape(1, -1), ln_b.reshape(1, -1), rw_cat)


def _moe_kernel(x_ref, gates_ref, wg_ref, wu_ref, wd_ref, g_ref, b_ref, o_ref, xb_scr, acc_scr):
    e = pl.program_id(1)

    @pl.when(e == 0)
    def _():
        xb_scr[...] = x_ref[...].astype(BF16)
        acc_scr[...] = jnp.zeros(acc_scr.shape, F32)

    xb = xb_scr[...]
    hg = jnp.dot(xb, wg_ref[...].astype(BF16), preferred_element_type=F32)
    hu = jnp.dot(xb, wu_ref[...].astype(BF16), preferred_element_type=F32)
    h = jax.nn.silu(hg) * hu
    y = jnp.dot(h.astype(BF16), wd_ref[...].astype(BF16), preferred_element_type=F32)
    gates = gates_ref[...]
    lane = lax.broadcasted_iota(jnp.int32, gates.shape, 1)
    gate = jnp.sum(jnp.where(lane == e, gates, 0.0), axis=-1, keepdims=True)
    acc_scr[...] = acc_scr[...] + gate * y

    @pl.when(e == pl.num_programs(1) - 1)
    def _():
        o_ref[...] = _layer_norm(DN_ALPHA * x_ref[...] + acc_scr[...], g_ref[...], b_ref[...])


def _moe(x2d, gates, wg, wu, wd, layer, ln_g, ln_b):
    t = x2d.shape[0]
    tm = min(MOE_TOKEN_TILE, t)
    row = lambda w: pl.BlockSpec((tm, w), lambda i, e: (i, 0))
    const = lambda shape: pl.BlockSpec(shape, lambda i, e: (0, 0))
    return pl.pallas_call(
        _moe_kernel, out_shape=jax.ShapeDtypeStruct((t, D_MODEL), F32), grid=(t // tm, N_EXPERTS),
        in_specs=[row(D_MODEL), row(LANES),
                  pl.BlockSpec((None, None, D_MODEL, D_EXPERT), lambda i, e: (layer, e, 0, 0)),
                  pl.BlockSpec((None, None, D_MODEL, D_EXPERT), lambda i, e: (layer, e, 0, 0)),
                  pl.BlockSpec((None, None, D_EXPERT, D_MODEL), lambda i, e: (layer, e, 0, 0)),
                  const((1, D_MODEL)), const((1, D_MODEL))],
        out_specs=row(D_MODEL),
        scratch_shapes=[pltpu.VMEM((tm, D_MODEL), BF16), pltpu.VMEM((tm, D_MODEL), F32)],
        compiler_params=_cparams("parallel", "arbitrary"), name="moe_ln",
    )(x2d, gates, wg, wu, wd, ln_g.reshape(1, -1), ln_b.reshape(1, -1))


def kernel(x_prompt, x_sample, cache_k, cache_v, state_ssm_re, state_ssm_im, page_table, w_in, w_out, ln1_g, ln1_b, ln2_g, ln2_b, lambda_q1, lambda_k1, lambda_q2, lambda_k2, subln_g, ssm_lambda_re, ssm_lambda_im, ssm_log_dt, ssm_b_re, ssm_b_im, ssm_c_re, ssm_c_im, ssm_d, ssm_w_glu, ssm_b_glu, router_w, router_bias, moe_w_gate, moe_w_up, moe_w_down):
    b, t, _ = x_prompt.shape
    n, dec_seq, _ = x_sample.shape
    depth = w_in.shape[0]
    n_pool = cache_k.shape[1]

    slopes = 2.0 ** (-8.0 * jnp.arange(1, N_HEADS + 1, dtype=F32) / N_HEADS)
    col = jnp.arange(LANES)
    cslope = jnp.where(col < 2 * N_HEADS * dec_seq, slopes[jnp.minimum(col // (2 * dec_seq), N_HEADS - 1)], 0.0)
    cslope = cslope.reshape(1, LANES).astype(F32)
    rw_pad = jnp.zeros((D_MODEL, LANES), F32).at[:, :N_EXPERTS].set(router_w)
    rw_hi = rw_pad.astype(BF16)
    rw_lo = (rw_pad - rw_hi.astype(F32)).astype(BF16)
    rw_cat = jnp.concatenate([rw_hi, rw_lo], axis=0)

    pages_k = cache_k.reshape(depth * n_pool, PAGE_SIZE * N_HEADS, V_DIM)
    pages_v = cache_v.reshape(depth * n_pool, PAGE_SIZE * N_HEADS, V_DIM)

    yp = x_prompt.reshape(b * t, D_MODEL)
    ys = x_sample.reshape(n * dec_seq, D_MODEL)
    outs = [[] for _ in range(8)]
    kbuf = jnp.zeros((depth * b * t * N_HEADS, V_DIM), F32)
    vbuf = jnp.zeros((depth * b * t * N_HEADS, V_DIM), F32)
    for l in range(depth):
        lam_init = 0.8 - 0.6 * math.exp(-0.3 * l)
        a_re, a_im, bb_re, bb_im, lam = _layer_prep(
            ssm_lambda_re[l], ssm_lambda_im[l], ssm_log_dt[l], ssm_b_re[l], ssm_b_im[l],
            lambda_q1[l], lambda_k1[l], lambda_q2[l], lambda_k2[l], lam_init)
        a_re, a_im = a_re.reshape(1, N_STATE), a_im.reshape(1, N_STATE)
        bbd = jnp.concatenate([_block_diag_in(bb_re), _block_diag_in(bb_im)], axis=1).astype(BF16)
        cbd = jnp.concatenate([_block_diag_out(ssm_c_re[l]), -_block_diag_out(ssm_c_im[l])], axis=0).astype(BF16)
        tail = (cbd, ssm_d[l].reshape(1, -1), ssm_w_glu[l].astype(BF16), ssm_b_glu[l].reshape(1, -1))
        w_in_b, w_out_b = w_in[l].astype(BF16), w_out[l].astype(BF16)

        def finish(x2d, a2d, s2d):
            x1, gates = _out_proj(a2d, s2d, x2d, w_out_b, ln1_g[l], ln1_b[l], rw_cat, router_bias)
            return _moe(x1, gates, moe_w_gate, moe_w_up, moe_w_down, l, ln2_g[l], ln2_b[l])

        u, q, kb, vt, kbuf, vbuf = _in_proj_prompt(yp, w_in_b, w_in_b[:, 3 * 512:].T, kbuf, vbuf, l, b)
        a = _attn_prompt(q.reshape(b, t, -1), kb.reshape(b, t, -1), vt, slopes, lam, subln_g[l], lam_init)
        s, hr, hi = _s5_prompt(u.reshape(b, t, -1), a_re, a_im, bbd, tail)
        yp = finish(yp, a.reshape(b * t, -1), s.reshape(b * t, -1))
        outs[2].append(hr.reshape(b, N_SSM_GROUPS, SSM_STATE))
        outs[3].append(hi.reshape(b, N_SSM_GROUPS, SSM_STATE))

        u, q, k, v = _in_proj_sample(ys, w_in_b)
        a = _attn_sample(q.reshape(n, dec_seq, -1), k.reshape(n, dec_seq, -1), v.reshape(n, dec_seq, -1),
                         pages_k, pages_v, l * n_pool, page_table, cslope, lam, subln_g[l], lam_init)
        s, hr, hi = _s5_sample(u.reshape(n, dec_seq, -1), state_ssm_re[l].reshape(n, N_STATE),
                               state_ssm_im[l].reshape(n, N_STATE), a_re, a_im, bbd, tail)
        ys = finish(ys, a.reshape(n * dec_seq, -1), s.reshape(n * dec_seq, -1))
        outs[4].append(k.reshape(n, dec_seq, N_HEADS, V_DIM))
        outs[5].append(v.reshape(n, dec_seq, N_HEADS, V_DIM))
        outs[6].append(hr.reshape(n, N_SSM_GROUPS, SSM_STATE))
        outs[7].append(hi.reshape(n, N_SSM_GROUPS, SSM_STATE))

    outs[0] = kbuf.reshape(depth, b, t, N_HEADS, V_DIM)
    outs[1] = vbuf.reshape(depth, b, t, N_HEADS, V_DIM)
    outs = [o if isinstance(o, jax.Array) else jnp.stack(o) for o in outs]
    return (yp.reshape(b, t, D_MODEL), ys.reshape(n, dec_seq, D_MODEL)) + tuple(outs)
```

```python
import functools
import math

import jax
import jax.numpy as jnp
from jax import lax
from jax.experimental import pallas as pl
from jax.experimental.pallas import tpu as pltpu

F32 = jnp.float32
BF16 = jnp.bfloat16

D_MODEL = 1024
ATT_WIDTH = 512
SSM_WIDTH = 512
HEAD_DIM = 64
V_DIM = 128
N_HEADS = 4
SSM_GROUP = 16
N_SSM_GROUPS = 32
SSM_STATE = 64
N_STATE = N_SSM_GROUPS * SSM_STATE
N_EXPERTS = 16
EXPERTS_PER_GROUP = 4
N_EXPERT_GROUPS = 4
D_EXPERT = 512
PAGE_SIZE = 128
DEPTH = 2
DN_ALPHA = (2 * DEPTH) ** 0.25
LN_EPS = 1e-5
RMS_EPS = 1e-5
ATTN_SCALE = HEAD_DIM ** -0.5

SUBLANES = 8
BF16_SUBLANES = 16
LANES = 128
SCAN_ROWS = SUBLANES
MOE_TOKEN_TILE = 1024
VMEM_LIMIT = 52 * 1024 * 1024


def _cparams(*sem):
    return pltpu.CompilerParams(dimension_semantics=sem, vmem_limit_bytes=VMEM_LIMIT)


def _layer_norm(z, g, b):
    mu = jnp.mean(z, -1, keepdims=True)
    var = jnp.mean(jnp.square(z - mu), -1, keepdims=True)
    return (z - mu) * lax.rsqrt(var + LN_EPS) * g + b


def _prep_kernel(lre_ref, lim_ref, ldt_ref, bre_ref, bim_ref, lq1_ref, lk1_ref, lq2_ref, lk2_ref,
                 are_ref, aim_ref, bbre_ref, bbim_ref, lam_ref, *, lam_init):
    lr, li = lre_ref[...], lim_ref[...]
    dt = jnp.exp(ldt_ref[...])
    mag = jnp.exp(lr * dt)
    a_re, a_im = mag * jnp.cos(li * dt), mag * jnp.sin(li * dt)
    den = lr * lr + li * li
    nr = a_re - 1.0
    coef_re = (nr * lr + a_im * li) / den
    coef_im = (a_im * lr - nr * li) / den
    br, bi = bre_ref[...], bim_ref[...]
    bbre_ref[...] = coef_re[..., None] * br - coef_im[..., None] * bi
    bbim_ref[...] = coef_re[..., None] * bi + coef_im[..., None] * br
    are_ref[...] = a_re
    aim_ref[...] = a_im
    s1 = jnp.sum(lq1_ref[...] * lk1_ref[...], axis=-1, keepdims=True)
    s2 = jnp.sum(lq2_ref[...] * lk2_ref[...], axis=-1, keepdims=True)
    lam_ref[...] = jnp.exp(s1) - jnp.exp(s2) + lam_init


def _layer_prep(lam_re, lam_im, log_dt, b_re, b_im, lq1, lk1, lq2, lk2, lam_init):
    g, p, c = N_SSM_GROUPS, SSM_STATE, SSM_GROUP
    out_shape = (jax.ShapeDtypeStruct((g, p), F32), jax.ShapeDtypeStruct((g, p), F32),
                 jax.ShapeDtypeStruct((g, p, c), F32), jax.ShapeDtypeStruct((g, p, c), F32),
                 jax.ShapeDtypeStruct((1, 1), F32))
    return pl.pallas_call(
        functools.partial(_prep_kernel, lam_init=lam_init), out_shape=out_shape, name="layer_prep",
    )(lam_re, lam_im, log_dt.reshape(g, 1), b_re, b_im,
      lq1.reshape(1, HEAD_DIM), lk1.reshape(1, HEAD_DIM), lq2.reshape(1, HEAD_DIM), lk2.reshape(1, HEAD_DIM))


def _block_diag_in(bb):
    eye = jnp.eye(N_SSM_GROUPS, dtype=bb.dtype)
    m = jnp.transpose(bb, (0, 2, 1))[:, :, None, :] * eye[:, None, :, None]
    return m.reshape(SSM_WIDTH, N_STATE)


def _block_diag_out(c):
    eye = jnp.eye(N_SSM_GROUPS, dtype=c.dtype)
    m = jnp.transpose(c, (0, 2, 1))[:, :, None, :] * eye[:, None, :, None]
    return m.reshape(N_STATE, SSM_WIDTH)


def _proj_sample_kernel(x_ref, w_ref, u_ref, q_ref, k_ref, v_ref):
    xb = x_ref[...].astype(BF16)
    for i, o_ref in enumerate((u_ref, q_ref, k_ref, v_ref)):
        o_ref[...] = jnp.dot(xb, w_ref[:, i * 512:(i + 1) * 512], preferred_element_type=F32)


def _in_proj_sample(x2d, w_bf16):
    t = x2d.shape[0]
    tm = min(512, t)
    out = jax.ShapeDtypeStruct((t, 512), F32)
    row = pl.BlockSpec((tm, 512), lambda i: (i, 0))
    return pl.pallas_call(
        _proj_sample_kernel, out_shape=(out,) * 4, grid=(t // tm,),
        in_specs=[pl.BlockSpec((tm, D_MODEL), lambda i: (i, 0)),
                  pl.BlockSpec((D_MODEL, 4 * 512), lambda i: (0, 0))],
        out_specs=(row,) * 4, compiler_params=_cparams("parallel"), name="in_proj_sample",
    )(x2d, w_bf16)


def _proj_prompt_kernel(x_ref, w_ref, wvt_ref, kbuf_in, vbuf_in, u_ref, q_ref, kb_ref, vt_ref, k_ref, v_ref):
    del kbuf_in, vbuf_in
    xb = x_ref[...].astype(BF16)
    tm = xb.shape[0]
    col = lambda i: jnp.dot(xb, w_ref[:, i * 512:(i + 1) * 512], preferred_element_type=F32)
    u_ref[...] = col(0)
    q_ref[...] = col(1)
    k, v = col(2), col(3)
    kb_ref[...] = k.astype(BF16)
    for h in range(N_HEADS):
        k_ref[pl.ds(h, tm, stride=N_HEADS), :] = k[:, h * V_DIM:(h + 1) * V_DIM]
        v_ref[pl.ds(h, tm, stride=N_HEADS), :] = v[:, h * V_DIM:(h + 1) * V_DIM]
    vt = lax.dot_general(wvt_ref[...], xb, (((1,), (1,)), ((), ())), preferred_element_type=F32)
    vt_ref[...] = vt.astype(BF16)


def _in_proj_prompt(x2d, w_bf16, wv_t, kbuf, vbuf, layer, n_seq):
    t = x2d.shape[0]
    seq_len = t // n_seq
    tm = min(512, seq_len)
    assert seq_len % tm == 0
    nt, nl = seq_len // tm, t // tm
    row = pl.BlockSpec((tm, 512), lambda i: (i, 0))
    cache_row = pl.BlockSpec((tm * N_HEADS, V_DIM), lambda i: (layer * nl + i, 0))
    anyspec = pl.BlockSpec(memory_space=pl.ANY)
    out_shape = (jax.ShapeDtypeStruct((t, 512), F32), jax.ShapeDtypeStruct((t, 512), F32),
                 jax.ShapeDtypeStruct((t, 512), BF16), jax.ShapeDtypeStruct((n_seq, 512, seq_len), BF16),
                 jax.ShapeDtypeStruct(kbuf.shape, F32), jax.ShapeDtypeStruct(vbuf.shape, F32))
    out_specs = (row, row, row,
                 pl.BlockSpec((None, 512, tm), lambda i: (i // nt, 0, i % nt)), cache_row, cache_row)
    return pl.pallas_call(
        _proj_prompt_kernel, out_shape=out_shape, grid=(nl,),
        in_specs=[pl.BlockSpec((tm, D_MODEL), lambda i: (i, 0)),
                  pl.BlockSpec((D_MODEL, 4 * 512), lambda i: (0, 0)),
                  pl.BlockSpec((512, D_MODEL), lambda i: (0, 0)), anyspec, anyspec],
        out_specs=out_specs, input_output_aliases={3: 4, 4: 5},
        compiler_params=_cparams("parallel"), name="in_proj_prompt",
    )(x2d, w_bf16, wv_t, kbuf, vbuf)


def _sub_ln(a, g, lam_init):
    return a * lax.rsqrt(jnp.mean(a * a, axis=-1, keepdims=True) + RMS_EPS) * g * (1.0 - lam_init)


ALIBI_SPLIT = 64
ATTN_KEY_BLOCK = 1024
ATTN_QUERY_BLOCK = 1024
ATTN_Q_CHUNK = 512


def _attn_prompt_kernel(qi_tab, ki_tab, slopes_ref, lam_ref, q_ref, kb_ref, vt_ref, kfeat_ref, vfeat_ref,
                        g_ref, o_ref, qs_scr, m_scr, acc_scr, *, tq, tk, lam_init):
    h, step = pl.program_id(1), pl.program_id(2)
    qi, ki = qi_tab[step], ki_tab[step]
    slope = slopes_ref[h]

    @pl.when(ki == 0)
    def _():
        qs = q_ref[...] * ATTN_SCALE
        lane = lax.broadcasted_iota(jnp.int32, qs.shape, 1)
        qfeat = jnp.where(lane == 0, ALIBI_SPLIT * slope, jnp.where(lane == 1, slope, 0.0)).astype(BF16)
        qs_scr[0:tq, 0:LANES] = jnp.where(lane < HEAD_DIM, qs, 0.0).astype(BF16)
        qs_scr[tq:2 * tq, 0:LANES] = jnp.where(lane >= HEAD_DIM, qs, 0.0).astype(BF16)
        qs_scr[0:tq, LANES:2 * LANES] = qfeat
        qs_scr[tq:2 * tq, LANES:2 * LANES] = qfeat
        m_scr[...] = jnp.full(m_scr.shape, -jnp.inf, F32)
        acc_scr[...] = jnp.zeros(acc_scr.shape, F32)

    def update(koff):
        kp = jnp.concatenate([kb_ref[...], kfeat_ref[...]], axis=1)
        vpt = jnp.concatenate([vt_ref[...], vfeat_ref[...]], axis=0)
        d = slope * (ki * tk - qi * tq).astype(F32)
        qc = ATTN_Q_CHUNK

        def n_keys(c0):
            return tk if koff is None else max(0, min(tk, (c0 % tq) + qc - koff))

        chunks = [c0 for c0 in range(0, 2 * tq, qc) if n_keys(c0) > 0]

        def scores(c0):
            return lax.dot_general(kp[0:n_keys(c0), :], qs_scr[c0:c0 + qc, :], (((1,), (1,)), ((), ())),
                                   preferred_element_type=F32)

        def softmax_part(s, c0):
            q0 = c0 % tq
            if koff is not None and koff + s.shape[0] - 1 > q0:
                key = lax.broadcasted_iota(jnp.int32, s.shape, 0) + koff
                qry = lax.broadcasted_iota(jnp.int32, s.shape, 1) + q0
                s = jnp.where(qry >= key, s, -jnp.inf)
            m_prev = m_scr[:, c0:c0 + qc]
            m_new = jnp.maximum(m_prev, jnp.max(s, axis=0, keepdims=True) + d)
            m_scr[:, c0:c0 + qc] = m_new
            return jnp.exp(s - (m_new - d)).astype(BF16), jnp.exp(m_prev - m_new)

        def accumulate(c0, p, alpha):
            pv = jnp.dot(vpt[:, 0:p.shape[0]], p, preferred_element_type=F32)
            acc_scr[:, c0:c0 + qc] = alpha * acc_scr[:, c0:c0 + qc] + pv

        n = len(chunks)
        s_q = [scores(chunks[0])] + ([scores(chunks[1])] if n > 1 else [])
        pending = None
        for i, c0 in enumerate(chunks):
            s = s_q.pop(0)
            if i + 2 < n:
                s_q.append(scores(chunks[i + 2]))
            if pending is not None:
                accumulate(*pending)
            pending = (c0,) + softmax_part(s, c0)
        accumulate(*pending)

    rel = ki - qi * (tq // tk)

    @pl.when(rel < 0)
    def _():
        update(None)

    for j in range(tq // tk):
        @pl.when(rel == j)
        def _(j=j):
            update(j * tk)

    @pl.when(rel == tq // tk - 1)
    def _():
        o1 = acc_scr[0:V_DIM, 0:tq] / acc_scr[V_DIM:V_DIM + 1, 0:tq]
        o2 = acc_scr[0:V_DIM, tq:2 * tq] / acc_scr[V_DIM:V_DIM + 1, tq:2 * tq]
        a = (o1 - lam_ref[0, 0] * o2).T
        o_ref[...] = _sub_ln(a, g_ref[...], lam_init)


def _attn_prompt(q, kb, vt, slopes, lam, subln_g, lam_init):
    b, t, _ = q.shape
    tk = min(ATTN_KEY_BLOCK, t)
    tq = min(ATTN_QUERY_BLOCK, t)
    nq, r = t // tq, tq // tk
    pairs = [(qi, ki) for qi in range(nq) for ki in range(r * (qi + 1))]
    qi_tab = jnp.asarray([p[0] for p in pairs], jnp.int32)
    ki_tab = jnp.asarray([p[1] for p in pairs], jnp.int32)
    c = jnp.arange(tk)[:, None]
    lane = jnp.arange(LANES)[None, :]
    kfeat = jnp.where(lane == 0, c // ALIBI_SPLIT, jnp.where(lane == 1, c % ALIBI_SPLIT, 0)).astype(BF16)
    ones_row = jnp.arange(BF16_SUBLANES)[:, None] == 0
    vfeat = jnp.broadcast_to(jnp.where(ones_row, 1.0, 0.0), (BF16_SUBLANES, tk)).astype(BF16)
    kernel = functools.partial(_attn_prompt_kernel, tq=tq, tk=tk, lam_init=lam_init)
    smem = pl.BlockSpec(memory_space=pltpu.SMEM)
    q_spec = pl.BlockSpec((None, tq, V_DIM), lambda b_, h, s, qt, kt: (b_, qt[s], h))
    k_spec = pl.BlockSpec((None, tk, V_DIM), lambda b_, h, s, qt, kt: (b_, kt[s], h))
    vt_spec = pl.BlockSpec((None, V_DIM, tk), lambda b_, h, s, qt, kt: (b_, h, kt[s]))
    const = lambda shape: pl.BlockSpec(shape, lambda b_, h, s, qt, kt: (0, 0))
    grid_spec = pltpu.PrefetchScalarGridSpec(
        num_scalar_prefetch=2, grid=(b, N_HEADS, len(pairs)),
        in_specs=[smem, smem, q_spec, k_spec, vt_spec, const((tk, LANES)), const((BF16_SUBLANES, tk)),
                  const((1, V_DIM))],
        out_specs=q_spec,
        scratch_shapes=[pltpu.VMEM((2 * tq, 2 * LANES), BF16), pltpu.VMEM((1, 2 * tq), F32),
                        pltpu.VMEM((V_DIM + BF16_SUBLANES, 2 * tq), F32)])
    return pl.pallas_call(
        kernel, out_shape=jax.ShapeDtypeStruct((b, t, ATT_WIDTH), F32), grid_spec=grid_spec,
        compiler_params=_cparams("parallel", "parallel", "arbitrary"), name="attn_prompt",
    )(qi_tab, ki_tab, slopes, lam, q, kb, vt, kfeat, vfeat, subln_g.reshape(1, V_DIM))


def _attn_sample_kernel(pt_ref, lam_ref, q_ref, knew_ref, vnew_ref, g_ref, cslope_ref, *rest,
                        n_pages, dec_seq, lam_init):
    del pt_ref
    k_pages = rest[:n_pages]
    v_pages = rest[n_pages:2 * n_pages]
    o_ref, s_scr = rest[2 * n_pages:]
    past = n_pages * PAGE_SIZE
    cols_per_head = 2 * dec_seq
    head_rows = lambda ref, h: ref[pl.ds(h, PAGE_SIZE, stride=N_HEADS), :]

    q = q_ref[...] * ATTN_SCALE
    lane = lax.broadcasted_iota(jnp.int32, (dec_seq, V_DIM), 1)
    qw = []
    for h in range(N_HEADS):
        qh = q[:, h * V_DIM:(h + 1) * V_DIM]
        blocks = [jnp.where(lane < HEAD_DIM, qh, 0.0), jnp.where(lane >= HEAD_DIM, qh, 0.0)]
        if h:
            blocks.insert(0, jnp.zeros((h * cols_per_head, V_DIM), F32))
        blocks.append(jnp.zeros((LANES - (h + 1) * cols_per_head, V_DIM), F32))
        qw.append(jnp.concatenate(blocks, axis=0))
    dn = (((1,), (1,)), ((), ()))

    s_past = None
    for h0 in range(0, N_HEADS, 2):
        keys = jnp.concatenate(
            [jnp.concatenate([head_rows(k_pages[p], h0), head_rows(k_pages[p], h0 + 1)], axis=1)
             for p in range(n_pages)], axis=0)
        w = jnp.concatenate([qw[h0], qw[h0 + 1]], axis=1)
        part = lax.dot_general(keys, w, dn, preferred_element_type=F32)
        s_past = part if s_past is None else s_past + part
    row = lax.broadcasted_iota(jnp.int32, (past, LANES), 0)
    col = lax.broadcasted_iota(jnp.int32, (past, LANES), 1)
    cslope = cslope_ref[...]
    dist = past + col % dec_seq - row
    s_scr[0:past, :] = s_past - cslope * dist.astype(F32)

    knew = knew_ref[...]
    s_new = None
    for h in range(N_HEADS):
        part = lax.dot_general(knew[:, h * V_DIM:(h + 1) * V_DIM], qw[h], dn, preferred_element_type=F32)
        s_new = part if s_new is None else s_new + part
    row = lax.broadcasted_iota(jnp.int32, (dec_seq, LANES), 0)
    col = lax.broadcasted_iota(jnp.int32, (dec_seq, LANES), 1)
    dist = col % dec_seq - row
    s_scr[past:past + dec_seq, :] = jnp.where(dist >= 0, s_new - cslope * dist.astype(F32), -jnp.inf)
    s_scr[past + dec_seq:past + PAGE_SIZE, :] = jnp.full((PAGE_SIZE - dec_seq, LANES), -jnp.inf, F32)

    s_all = s_scr[...]
    m = jnp.max(s_all, axis=0, keepdims=True)
    p_all = jnp.exp(s_all - m)
    l = jnp.sum(p_all, axis=0, keepdims=True)
    c_i = lax.broadcasted_iota(jnp.int32, (1, LANES), 1)
    sign = jnp.where((c_i // dec_seq) % 2 == 0, 1.0, -lam_ref[0, 0])
    s_scr[...] = p_all * (sign / l)

    vnew = vnew_ref[...]
    vzero = jnp.zeros((PAGE_SIZE - dec_seq, V_DIM), F32)
    pts = [s_scr[p * PAGE_SIZE:(p + 1) * PAGE_SIZE, :].T for p in range(n_pages + 1)]
    for h in range(N_HEADS):
        w_h = jnp.concatenate([pt[h * cols_per_head:(h + 1) * cols_per_head, :] for pt in pts], axis=1)
        v_h = jnp.concatenate([head_rows(v_pages[p], h) for p in range(n_pages)]
                              + [vnew[:, h * V_DIM:(h + 1) * V_DIM], vzero], axis=0)
        out = jnp.dot(w_h, v_h, preferred_element_type=F32)
        a = out[0:dec_seq, :] + out[dec_seq:2 * dec_seq, :]
        o_ref[:, h * V_DIM:(h + 1) * V_DIM] = _sub_ln(a, g_ref[...], lam_init)


def _attn_sample(q, k, v, k_pages, v_pages, page_base, page_table, cslope, lam, subln_g, lam_init):
    n, dec_seq, _ = q.shape
    n_pages = page_table.shape[1]
    assert 2 * N_HEADS * dec_seq <= LANES and dec_seq % SUBLANES == 0 and N_HEADS % 2 == 0
    kernel = functools.partial(_attn_sample_kernel, n_pages=n_pages, dec_seq=dec_seq, lam_init=lam_init)
    tok = pl.BlockSpec((None, dec_seq, ATT_WIDTH), lambda i, pt: (i, 0, 0))

    def page_spec(p):
        return pl.BlockSpec((None, PAGE_SIZE * N_HEADS, V_DIM),
                            lambda i, pt: (page_base + pt[i * n_pages + p], 0, 0))

    grid_spec = pltpu.PrefetchScalarGridSpec(
        num_scalar_prefetch=1, grid=(n,),
        in_specs=[pl.BlockSpec(memory_space=pltpu.SMEM), tok, tok, tok,
                  pl.BlockSpec((1, V_DIM), lambda i, pt: (0, 0)),
                  pl.BlockSpec((1, LANES), lambda i, pt: (0, 0))]
                 + [page_spec(p) for p in range(n_pages)] * 2,
        out_specs=tok,
        scratch_shapes=[pltpu.VMEM(((n_pages + 1) * PAGE_SIZE, LANES), F32)])
    return pl.pallas_call(
        kernel, out_shape=jax.ShapeDtypeStruct((n, dec_seq, ATT_WIDTH), F32), grid_spec=grid_spec,
        compiler_params=_cparams("arbitrary"), name="attn_sample",
    )(page_table.reshape(-1), lam, q, k, v, subln_g.reshape(1, V_DIM), cslope,
      *([k_pages] * n_pages), *([v_pages] * n_pages))


def _s5_scan_chunk(bu_scr, hs_scr, hre_scr, him_scr, are_ref, aim_ref, steps, lane_chunk):
    for c in range(N_STATE // lane_chunk):
        lo = c * lane_chunk
        are = jnp.broadcast_to(are_ref[:, lo:lo + lane_chunk], (SCAN_ROWS, lane_chunk))
        aim = jnp.broadcast_to(aim_ref[:, lo:lo + lane_chunk], (SCAN_ROWS, lane_chunk))

        def body(j, carry, lo=lo, are=are, aim=aim):
            hre, him = carry
            r0 = pl.multiple_of(j * SCAN_ROWS, SCAN_ROWS)
            bre = bu_scr[pl.ds(r0, SCAN_ROWS), lo:lo + lane_chunk]
            bim = bu_scr[pl.ds(r0, SCAN_ROWS), N_STATE + lo:N_STATE + lo + lane_chunk]
            nre = are * hre - aim * him + bre
            nim = are * him + aim * hre + bim
            if hs_scr is not None:
                hs_scr[pl.ds(r0, SCAN_ROWS), lo:lo + lane_chunk] = nre
                hs_scr[pl.ds(r0, SCAN_ROWS), N_STATE + lo:N_STATE + lo + lane_chunk] = nim
            return nre, nim

        hre, him = lax.fori_loop(0, steps, body,
                                 (hre_scr[:, lo:lo + lane_chunk], him_scr[:, lo:lo + lane_chunk]),
                                 unroll=min(4, steps))
        hre_scr[:, lo:lo + lane_chunk] = hre
        him_scr[:, lo:lo + lane_chunk] = him


def _s5_kernel(u_ref, h0re_ref, h0im_ref, are_ref, aim_ref, bbd_ref, *rest, steps, with_y):
    perm_scr, rest = rest[-1], rest[:-1]
    if with_y:
        cbd_ref, d_ref, wglu_ref, bglu_ref, s_ref, hre_ref, him_ref, hre_scr, him_scr, bu_scr, hs_scr = rest
    else:
        hre_ref, him_ref, hre_scr, him_scr, bu_scr = rest
        hs_scr = None
    c = pl.program_id(1)

    @pl.when(c == 0)
    def _():
        hre_scr[...] = h0re_ref[...]
        him_scr[...] = h0im_ref[...]

    n_slabs = SSM_WIDTH // LANES
    for r in range(SCAN_ROWS):
        for k in range(n_slabs):
            perm_scr[k, pl.ds(r, steps, stride=SCAN_ROWS), :] = u_ref[r, :, k * LANES:(k + 1) * LANES]
    u = jnp.concatenate([perm_scr[k] for k in range(n_slabs)], axis=1)
    ub = u.astype(BF16)
    hu, hst = SSM_WIDTH // 2, N_STATE // 2
    for half in range(2):
        for part in range(2):
            cols = slice(part * N_STATE + half * hst, part * N_STATE + (half + 1) * hst)
            bu_scr[:, cols] = jnp.dot(ub[:, half * hu:(half + 1) * hu], bbd_ref[half * hu:(half + 1) * hu, cols],
                                      preferred_element_type=F32)
    _s5_scan_chunk(bu_scr, hs_scr, hre_scr, him_scr, are_ref, aim_ref, steps, lane_chunk=1024)
    if with_y:
        y_halves = []
        for half in range(2):
            ycols = slice(half * hu, (half + 1) * hu)
            acc = None
            for part in range(2):
                rows = slice(part * N_STATE + half * hst, part * N_STATE + (half + 1) * hst)
                term = jnp.dot(hs_scr[:, rows].astype(BF16), cbd_ref[rows, ycols], preferred_element_type=F32)
                acc = term if acc is None else acc + term
            y_halves.append(acc)
        y = jnp.concatenate(y_halves, axis=1) + d_ref[...] * u
        g = jax.nn.gelu(y)
        gate = jnp.dot(g.astype(BF16), wglu_ref[...], preferred_element_type=F32) + bglu_ref[...]
        s = g * jax.nn.sigmoid(gate)
        for k in range(n_slabs):
            perm_scr[k] = s[:, k * LANES:(k + 1) * LANES]
        for r in range(SCAN_ROWS):
            rows = [perm_scr[k, pl.ds(r, steps, stride=SCAN_ROWS), :] for k in range(n_slabs)]
            s_ref[r] = jnp.concatenate(rows, axis=1).astype(s_ref.dtype)

    @pl.when(c == pl.num_programs(1) - 1)
    def _():
        hre_ref[...] = hre_scr[...]
        him_ref[...] = him_scr[...]


def _s5_call(u4, h0re, h0im, a_re, a_im, bbd, tail, steps_per_chunk, with_y):
    ng, _, j_total, _ = u4.shape
    steps = min(steps_per_chunk, j_total)
    nc = j_total // steps
    blk = steps * SCAN_ROWS
    const = lambda shape: pl.BlockSpec(shape, lambda g, c: (0,) * len(shape))
    st_spec = pl.BlockSpec((None, SCAN_ROWS, N_STATE), lambda g, c: (g, 0, 0))
    st_shape = jax.ShapeDtypeStruct((ng, SCAN_ROWS, N_STATE), F32)
    u_spec = pl.BlockSpec((None, SCAN_ROWS, steps, SSM_WIDTH), lambda g, c: (g, 0, c, 0))
    in_specs = [u_spec, st_spec, st_spec, const((1, N_STATE)), const((1, N_STATE)),
                const((SSM_WIDTH, 2 * N_STATE))]
    scratch = [pltpu.VMEM((SCAN_ROWS, N_STATE), F32), pltpu.VMEM((SCAN_ROWS, N_STATE), F32),
               pltpu.VMEM((blk, 2 * N_STATE), F32)]
    if with_y:
        in_specs += [const((2 * N_STATE, SSM_WIDTH)), const((1, SSM_WIDTH)),
                     const((SSM_WIDTH, SSM_WIDTH)), const((1, SSM_WIDTH))]
        out_shape = (jax.ShapeDtypeStruct(u4.shape, BF16), st_shape, st_shape)
        out_specs = (u_spec, st_spec, st_spec)
        scratch.append(pltpu.VMEM((blk, 2 * N_STATE), F32))
    else:
        out_shape = (st_shape, st_shape)
        out_specs = (st_spec, st_spec)
    scratch.append(pltpu.VMEM((SSM_WIDTH // LANES, blk, LANES), F32))
    return pl.pallas_call(
        functools.partial(_s5_kernel, steps=steps, with_y=with_y),
        out_shape=out_shape, grid=(ng, nc), in_specs=in_specs, out_specs=out_specs,
        scratch_shapes=scratch, compiler_params=_cparams("parallel", "arbitrary"),
        name="s5_scan_glu" if with_y else "s5_scan_state",
    )(u4, h0re, h0im, a_re, a_im, bbd, *tail)


def _s5_seg_init_kernel(fre_ref, fim_ref, are_ref, aim_ref, ire_ref, iim_ref, *, seg_len, n_seq, n_seg):
    pr, pi = are_ref[...], aim_ref[...]
    ar, ai = None, None
    e = seg_len
    while e:
        if e & 1:
            ar, ai = (pr, pi) if ar is None else (ar * pr - ai * pi, ar * pi + ai * pr)
        e >>= 1
        if e:
            pr, pi = pr * pr - pi * pi, 2.0 * pr * pi
    for n in range(n_seq):
        hr = jnp.zeros((1, N_STATE), F32)
        hi = jnp.zeros((1, N_STATE), F32)
        for s in range(n_seg):
            r = n * n_seg + s
            ire_ref[r:r + 1, :] = hr
            iim_ref[r:r + 1, :] = hi
            fr, fi = fre_ref[r:r + 1, :], fim_ref[r:r + 1, :]
            hr, hi = ar * hr - ai * hi + fr, ar * hi + ai * hr + fi


def _s5_seg_init(f_re, f_im, a_re, a_im, seg_len, n_seq, n_seg):
    shape = jax.ShapeDtypeStruct(f_re.shape, F32)
    return pl.pallas_call(
        functools.partial(_s5_seg_init_kernel, seg_len=seg_len, n_seq=n_seq, n_seg=n_seg),
        out_shape=(shape, shape), name="s5_seg_init",
    )(f_re, f_im, a_re, a_im)


def _s5_prompt(u, a_re, a_im, bbd, tail):
    b, t, _ = u.shape
    n_seg = SCAN_ROWS // b
    assert b * n_seg == SCAN_ROWS and t % n_seg == 0
    j = t // n_seg
    u4 = u.reshape(1, SCAN_ROWS, j, SSM_WIDTH)
    zeros = jnp.zeros((1, SCAN_ROWS, N_STATE), F32)
    f_re, f_im = _s5_call(u4, zeros, zeros, a_re, a_im, bbd, (), 64, with_y=False)
    i_re, i_im = _s5_seg_init(f_re[0], f_im[0], a_re, a_im, j, b, n_seg)
    s4, h_re, h_im = _s5_call(u4, i_re[None], i_im[None], a_re, a_im, bbd, tail, 64, with_y=True)
    last = h_re.reshape(b, n_seg, N_STATE)[:, -1], h_im.reshape(b, n_seg, N_STATE)[:, -1]
    return s4.reshape(b, t, SSM_WIDTH), last[0], last[1]


def _s5_sample(u, h0_re, h0_im, a_re, a_im, bbd, tail):
    n, j, _ = u.shape
    ng = n // SCAN_ROWS
    s4, h_re, h_im = _s5_call(u.reshape(ng, SCAN_ROWS, j, SSM_WIDTH), h0_re.reshape(ng, SCAN_ROWS, N_STATE),
                              h0_im.reshape(ng, SCAN_ROWS, N_STATE), a_re, a_im, bbd, tail, j, with_y=True)
    return s4.reshape(n, j, SSM_WIDTH), h_re.reshape(n, N_STATE), h_im.reshape(n, N_STATE)


def _top2_sum(a, b, c, d):
    m1, n1 = jnp.maximum(a, b), jnp.minimum(a, b)
    m2, n2 = jnp.maximum(c, d), jnp.minimum(c, d)
    return jnp.maximum(m1, m2) + jnp.maximum(jnp.minimum(m1, m2), jnp.maximum(n1, n2))


def _route_rows(logit_rows, rbias_ref):
    scores = [jax.nn.sigmoid(r) for r in logit_rows]
    sel = [scores[e] + rbias_ref[e] for e in range(N_EXPERTS)]
    gs = [_top2_sum(*sel[EXPERTS_PER_GROUP * g:EXPERTS_PER_GROUP * (g + 1)]) for g in range(N_EXPERT_GROUPS)]
    best_v, best_g = gs[0], jnp.zeros(gs[0].shape, jnp.int32)
    for g in range(1, N_EXPERT_GROUPS):
        upd = gs[g] > best_v
        best_g = jnp.where(upd, g, best_g)
        best_v = jnp.where(upd, gs[g], best_v)
    picked = []
    for e in range(N_EXPERTS):
        g = e // EXPERTS_PER_GROUP
        rank = jnp.zeros(sel[e].shape, jnp.int32)
        for o in range(EXPERTS_PER_GROUP * g, EXPERTS_PER_GROUP * (g + 1)):
            if o != e:
                ahead = (sel[o] >= sel[e]) if o < e else (sel[o] > sel[e])
                rank = rank + ahead.astype(jnp.int32)
        on = jnp.logical_and(best_g == g, rank < 2)
        picked.append(jnp.where(on, scores[e], 0.0))
    den = picked[0]
    for e in range(1, N_EXPERTS):
        den = den + picked[e]
    return [w / den for w in picked]


def _outproj_kernel(rbias_ref, a_ref, s_ref, x_ref, wo_ref, g_ref, b_ref, rw_ref, x1_ref, gates_ref):
    mix = (jnp.dot(a_ref[...].astype(BF16), wo_ref[0:ATT_WIDTH, :], preferred_element_type=F32)
           + jnp.dot(s_ref[...].astype(BF16), wo_ref[ATT_WIDTH:, :], preferred_element_type=F32))
    x1 = _layer_norm(DN_ALPHA * x_ref[...] + mix, g_ref[...], b_ref[...])
    x1_ref[...] = x1
    xh = x1.astype(BF16)
    xl = (x1 - xh.astype(F32)).astype(BF16)
    rwh, rwl = rw_ref[0:D_MODEL, :], rw_ref[D_MODEL:, :]
    logits = (jnp.dot(xh, rwh, preferred_element_type=F32)
              + jnp.dot(xl, rwh, preferred_element_type=F32)
              + jnp.dot(xh, rwl, preferred_element_type=F32))
    lt = logits.T
    gate_rows = _route_rows([lt[e:e + 1, :] for e in range(N_EXPERTS)], rbias_ref)
    tm = lt.shape[1]
    r_i = lax.broadcasted_iota(jnp.int32, (N_EXPERTS, tm), 0)
    gt = jnp.zeros((N_EXPERTS, tm), F32)
    for e in range(N_EXPERTS):
        gt = jnp.where(r_i == e, jnp.broadcast_to(gate_rows[e], (N_EXPERTS, tm)), gt)
    gt = jnp.concatenate([gt, jnp.zeros((LANES - N_EXPERTS, tm), F32)], axis=0)
    gates_ref[...] = gt.T


def _out_proj(a2d, s, x2d, wo_bf16, ln_g, ln_b, rw_cat, router_bias):
    t = x2d.shape[0]
    tm = min(512, t)
    row = lambda w: pl.BlockSpec((tm, w), lambda i: (i, 0))
    const = lambda shape: pl.BlockSpec(shape, lambda i: (0, 0))
    s_spec = row(SSM_WIDTH)
    return pl.pallas_call(
        _outproj_kernel,
        out_shape=(jax.ShapeDtypeStruct((t, D_MODEL), F32), jax.ShapeDtypeStruct((t, LANES), F32)),
        grid=(t // tm,),
        in_specs=[pl.BlockSpec(memory_space=pltpu.SMEM), row(ATT_WIDTH), s_spec, row(D_MODEL),
                  const((D_MODEL, D_MODEL)), const((1, D_MODEL)), const((1, D_MODEL)),
                  const((2 * D_MODEL, LANES))],
        out_specs=(row(D_MODEL), row(LANES)), compiler_params=_cparams("parallel"), name="out_proj_router",
    )(router_bias, a2d, s, x2d, wo_bf16, ln_g.reshape(1, -1), ln_b.reshape(1, -1), rw_cat)


def _moe_kernel(x_ref, gates_ref, wg_ref, wu_ref, wd_ref, g_ref, b_ref, o_ref, xb_scr, acc_scr):
    e = pl.program_id(1)

    @pl.when(e == 0)
    def _():
        xb_scr[...] = x_ref[...].astype(BF16)
        acc_scr[...] = jnp.zeros(acc_scr.shape, F32)

    xb = xb_scr[...]
    hg = jnp.dot(xb, wg_ref[...].astype(BF16), preferred_element_type=F32)
    hu = jnp.dot(xb, wu_ref[...].astype(BF16), preferred_element_type=F32)
    h = jax.nn.silu(hg) * hu
    y = jnp.dot(h.astype(BF16), wd_ref[...].astype(BF16), preferred_element_type=F32)
    gates = gates_ref[...]
    lane = lax.broadcasted_iota(jnp.int32, gates.shape, 1)
    gate = jnp.sum(jnp.where(lane == e, gates, 0.0), axis=-1, keepdims=True)
    acc_scr[...] = acc_scr[...] + gate * y

    @pl.when(e == pl.num_programs(1) - 1)
    def _():
        o_ref[...] = _layer_norm(DN_ALPHA * x_ref[...] + acc_scr[...], g_ref[...], b_ref[...])


def _moe(x2d, gates, wg, wu, wd, layer, ln_g, ln_b):
    t = x2d.shape[0]
    tm = min(MOE_TOKEN_TILE, t)
    row = lambda w: pl.BlockSpec((tm, w), lambda i, e: (i, 0))
    const = lambda shape: pl.BlockSpec(shape, lambda i, e: (0, 0))
    return pl.pallas_call(
        _moe_kernel, out_shape=jax.ShapeDtypeStruct((t, D_MODEL), F32), grid=(t // tm, N_EXPERTS),
        in_specs=[row(D_MODEL), row(LANES),
                  pl.BlockSpec((None, None, D_MODEL, D_EXPERT), lambda i, e: (layer, e, 0, 0)),
                  pl.BlockSpec((None, None, D_MODEL, D_EXPERT), lambda i, e: (layer, e, 0, 0)),
                  pl.BlockSpec((None, None, D_EXPERT, D_MODEL), lambda i, e: (layer, e, 0, 0)),
                  const((1, D_MODEL)), const((1, D_MODEL))],
        out_specs=row(D_MODEL),
        scratch_shapes=[pltpu.VMEM((tm, D_MODEL), BF16), pltpu.VMEM((tm, D_MODEL), F32)],
        compiler_params=_cparams("parallel", "arbitrary"), name="moe_ln",
    )(x2d, gates, wg, wu, wd, ln_g.reshape(1, -1), ln_b.reshape(1, -1))


def kernel(x_prompt, x_sample, cache_k, cache_v, state_ssm_re, state_ssm_im, page_table, w_in, w_out, ln1_g, ln1_b, ln2_g, ln2_b, lambda_q1, lambda_k1, lambda_q2, lambda_k2, subln_g, ssm_lambda_re, ssm_lambda_im, ssm_log_dt, ssm_b_re, ssm_b_im, ssm_c_re, ssm_c_im, ssm_d, ssm_w_glu, ssm_b_glu, router_w, router_bias, moe_w_gate, moe_w_up, moe_w_down):
    b, t, _ = x_prompt.shape
    n, dec_seq, _ = x_sample.shape
    depth = w_in.shape[0]
    n_pool = cache_k.shape[1]

    slopes = 2.0 ** (-8.0 * jnp.arange(1, N_HEADS + 1, dtype=F32) / N_HEADS)
    col = jnp.arange(LANES)
    cslope = jnp.where(col < 2 * N_HEADS * dec_seq, slopes[jnp.minimum(col // (2 * dec_seq), N_HEADS - 1)], 0.0)
    cslope = cslope.reshape(1, LANES).astype(F32)
    rw_pad = jnp.zeros((D_MODEL, LANES), F32).at[:, :N_EXPERTS].set(router_w)
    rw_hi = rw_pad.astype(BF16)
    rw_lo = (rw_pad - rw_hi.astype(F32)).astype(BF16)
    rw_cat = jnp.concatenate([rw_hi, rw_lo], axis=0)

    pages_k = cache_k.reshape(depth * n_pool, PAGE_SIZE * N_HEADS, V_DIM)
    pages_v = cache_v.reshape(depth * n_pool, PAGE_SIZE * N_HEADS, V_DIM)

    yp = x_prompt.reshape(b * t, D_MODEL)
    ys = x_sample.reshape(n * dec_seq, D_MODEL)
    outs = [[] for _ in range(8)]
    kbuf = jnp.zeros((depth * b * t * N_HEADS, V_DIM), F32)
    vbuf = jnp.zeros((depth * b * t * N_HEADS, V_DIM), F32)
    for l in range(depth):
        lam_init = 0.8 - 0.6 * math.exp(-0.3 * l)
        a_re, a_im, bb_re, bb_im, lam = _layer_prep(
            ssm_lambda_re[l], ssm_lambda_im[l], ssm_log_dt[l], ssm_b_re[l], ssm_b_im[l],
            lambda_q1[l], lambda_k1[l], lambda_q2[l], lambda_k2[l], lam_init)
        a_re, a_im = a_re.reshape(1, N_STATE), a_im.reshape(1, N_STATE)
        bbd = jnp.concatenate([_block_diag_in(bb_re), _block_diag_in(bb_im)], axis=1).astype(BF16)
        cbd = jnp.concatenate([_block_diag_out(ssm_c_re[l]), -_block_diag_out(ssm_c_im[l])], axis=0).astype(BF16)
        tail = (cbd, ssm_d[l].reshape(1, -1), ssm_w_glu[l].astype(BF16), ssm_b_glu[l].reshape(1, -1))
        w_in_b, w_out_b = w_in[l].astype(BF16), w_out[l].astype(BF16)

        def finish(x2d, a2d, s2d):
            x1, gates = _out_proj(a2d, s2d, x2d, w_out_b, ln1_g[l], ln1_b[l], rw_cat, router_bias)
            return _moe(x1, gates, moe_w_gate, moe_w_up, moe_w_down, l, ln2_g[l], ln2_b[l])

        u, q, kb, vt, kbuf, vbuf = _in_proj_prompt(yp, w_in_b, w_in_b[:, 3 * 512:].T, kbuf, vbuf, l, b)
        a = _attn_prompt(q.reshape(b, t, -1), kb.reshape(b, t, -1), vt, slopes, lam, subln_g[l], lam_init)
        s, hr, hi = _s5_prompt(u.reshape(b, t, -1), a_re, a_im, bbd, tail)
        yp = finish(yp, a.reshape(b * t, -1), s.reshape(b * t, -1))
        outs[2].append(hr.reshape(b, N_SSM_GROUPS, SSM_STATE))
        outs[3].append(hi.reshape(b, N_SSM_GROUPS, SSM_STATE))

        u, q, k, v = _in_proj_sample(ys, w_in_b)
        a = _attn_sample(q.reshape(n, dec_seq, -1), k.reshape(n, dec_seq, -1), v.reshape(n, dec_seq, -1),
                         pages_k, pages_v, l * n_pool, page_table, cslope, lam, subln_g[l], lam_init)
        s, hr, hi = _s5_sample(u.reshape(n, dec_seq, -1), state_ssm_re[l].reshape(n, N_STATE),
                               state_ssm_im[l].reshape(n, N_STATE), a_re, a_im, bbd, tail)
        ys = finish(ys, a.reshape(n * dec_seq, -1), s.reshape(n * dec_seq, -1))
        outs[4].append(k.reshape(n, dec_seq, N_HEADS, V_DIM))
        outs[5].append(v.reshape(n, dec_seq, N_HEADS, V_DIM))
        outs[6].append(hr.reshape(n, N_SSM_GROUPS, SSM_STATE))
        outs[7].append(hi.reshape(n, N_SSM_GROUPS, SSM_STATE))

    outs[0] = kbuf.reshape(depth, b, t, N_HEADS, V_DIM)
    outs[1] = vbuf.reshape(depth, b, t, N_HEADS, V_DIM)
    outs = [o if isinstance(o, jax.Array) else jnp.stack(o) for o in outs]
    return (yp.reshape(b, t, D_MODEL), ys.reshape(n, dec_seq, D_MODEL)) + tuple(outs)
```

```python
import functools
import math

import jax
import jax.numpy as jnp
from jax import lax
from jax.experimental import pallas as pl
from jax.experimental.pallas import tpu as pltpu

F32 = jnp.float32
BF16 = jnp.bfloat16

D_MODEL = 1024
ATT_WIDTH = 512
SSM_WIDTH = 512
HEAD_DIM = 64
V_DIM = 128
N_HEADS = 4
SSM_GROUP = 16
N_SSM_GROUPS = 32
SSM_STATE = 64
N_STATE = N_SSM_GROUPS * SSM_STATE
N_EXPERTS = 16
EXPERTS_PER_GROUP = 4
N_EXPERT_GROUPS = 4
D_EXPERT = 512
PAGE_SIZE = 128
DEPTH = 2
DN_ALPHA = (2 * DEPTH) ** 0.25
LN_EPS = 1e-5
RMS_EPS = 1e-5
ATTN_SCALE = HEAD_DIM ** -0.5

SUBLANES = 8
BF16_SUBLANES = 16
LANES = 128
SCAN_ROWS = SUBLANES
MOE_TOKEN_TILE = 1024
VMEM_LIMIT = 52 * 1024 * 1024


def _cparams(*sem):
    return pltpu.CompilerParams(dimension_semantics=sem, vmem_limit_bytes=VMEM_LIMIT)


def _layer_norm(z, g, b):
    mu = jnp.mean(z, -1, keepdims=True)
    var = jnp.mean(jnp.square(z - mu), -1, keepdims=True)
    return (z - mu) * lax.rsqrt(var + LN_EPS) * g + b


def _prep_kernel(lre_ref, lim_ref, ldt_ref, bre_ref, bim_ref, lq1_ref, lk1_ref, lq2_ref, lk2_ref,
                 are_ref, aim_ref, bbre_ref, bbim_ref, lam_ref, *, lam_init):
    lr, li = lre_ref[...], lim_ref[...]
    dt = jnp.exp(ldt_ref[...])
    mag = jnp.exp(lr * dt)
    a_re, a_im = mag * jnp.cos(li * dt), mag * jnp.sin(li * dt)
    den = lr * lr + li * li
    nr = a_re - 1.0
    coef_re = (nr * lr + a_im * li) / den
    coef_im = (a_im * lr - nr * li) / den
    br, bi = bre_ref[...], bim_ref[...]
    bbre_ref[...] = coef_re[..., None] * br - coef_im[..., None] * bi
    bbim_ref[...] = coef_re[..., None] * bi + coef_im[..., None] * br
    are_ref[...] = a_re
    aim_ref[...] = a_im
    s1 = jnp.sum(lq1_ref[...] * lk1_ref[...], axis=-1, keepdims=True)
    s2 = jnp.sum(lq2_ref[...] * lk2_ref[...], axis=-1, keepdims=True)
    lam_ref[...] = jnp.exp(s1) - jnp.exp(s2) + lam_init


def _layer_prep(lam_re, lam_im, log_dt, b_re, b_im, lq1, lk1, lq2, lk2, lam_init):
    g, p, c = N_SSM_GROUPS, SSM_STATE, SSM_GROUP
    out_shape = (jax.ShapeDtypeStruct((g, p), F32), jax.ShapeDtypeStruct((g, p), F32),
                 jax.ShapeDtypeStruct((g, p, c), F32), jax.ShapeDtypeStruct((g, p, c), F32),
                 jax.ShapeDtypeStruct((1, 1), F32))
    return pl.pallas_call(
        functools.partial(_prep_kernel, lam_init=lam_init), out_shape=out_shape, name="layer_prep",
    )(lam_re, lam_im, log_dt.reshape(g, 1), b_re, b_im,
      lq1.reshape(1, HEAD_DIM), lk1.reshape(1, HEAD_DIM), lq2.reshape(1, HEAD_DIM), lk2.reshape(1, HEAD_DIM))


def _block_diag_in(bb):
    eye = jnp.eye(N_SSM_GROUPS, dtype=bb.dtype)
    m = jnp.transpose(bb, (0, 2, 1))[:, :, None, :] * eye[:, None, :, None]
    return m.reshape(SSM_WIDTH, N_STATE)


def _block_diag_out(c):
    eye = jnp.eye(N_SSM_GROUPS, dtype=c.dtype)
    m = jnp.transpose(c, (0, 2, 1))[:, :, None, :] * eye[:, None, :, None]
    return m.reshape(N_STATE, SSM_WIDTH)


def _proj_sample_kernel(x_ref, w_ref, u_ref, q_ref, k_ref, v_ref):
    xb = x_ref[...].astype(BF16)
    for i, o_ref in enumerate((u_ref, q_ref, k_ref, v_ref)):
        o_ref[...] = jnp.dot(xb, w_ref[:, i * 512:(i + 1) * 512], preferred_element_type=F32)


def _in_proj_sample(x2d, w_bf16):
    t = x2d.shape[0]
    tm = min(512, t)
    out = jax.ShapeDtypeStruct((t, 512), F32)
    row = pl.BlockSpec((tm, 512), lambda i: (i, 0))
    return pl.pallas_call(
        _proj_sample_kernel, out_shape=(out,) * 4, grid=(t // tm,),
        in_specs=[pl.BlockSpec((tm, D_MODEL), lambda i: (i, 0)),
                  pl.BlockSpec((D_MODEL, 4 * 512), lambda i: (0, 0))],
        out_specs=(row,) * 4, compiler_params=_cparams("parallel"), name="in_proj_sample",
    )(x2d, w_bf16)


def _proj_prompt_kernel(x_ref, w_ref, wvt_ref, kbuf_in, vbuf_in, u_ref, q_ref, kb_ref, vt_ref, k_ref, v_ref):
    del kbuf_in, vbuf_in
    xb = x_ref[...].astype(BF16)
    tm = xb.shape[0]
    col = lambda i: jnp.dot(xb, w_ref[:, i * 512:(i + 1) * 512], preferred_element_type=F32)
    u_ref[...] = col(0)
    q_ref[...] = col(1)
    k, v = col(2), col(3)
    kb_ref[...] = k.astype(BF16)
    for h in range(N_HEADS):
        k_ref[pl.ds(h, tm, stride=N_HEADS), :] = k[:, h * V_DIM:(h + 1) * V_DIM]
        v_ref[pl.ds(h, tm, stride=N_HEADS), :] = v[:, h * V_DIM:(h + 1) * V_DIM]
    vt = lax.dot_general(wvt_ref[...], xb, (((1,), (1,)), ((), ())), preferred_element_type=F32)
    vt_ref[...] = vt.astype(BF16)


def _in_proj_prompt(x2d, w_bf16, wv_t, kbuf, vbuf, layer, n_seq):
    t = x2d.shape[0]
    seq_len = t // n_seq
    tm = min(512, seq_len)
    assert seq_len % tm == 0
    nt, nl = seq_len // tm, t // tm
    row = pl.BlockSpec((tm, 512), lambda i: (i, 0))
    cache_row = pl.BlockSpec((tm * N_HEADS, V_DIM), lambda i: (layer * nl + i, 0))
    anyspec = pl.BlockSpec(memory_space=pl.ANY)
    out_shape = (jax.ShapeDtypeStruct((t, 512), F32), jax.ShapeDtypeStruct((t, 512), F32),
                 jax.ShapeDtypeStruct((t, 512), BF16), jax.ShapeDtypeStruct((n_seq, 512, seq_len), BF16),
                 jax.ShapeDtypeStruct(kbuf.shape, F32), jax.ShapeDtypeStruct(vbuf.shape, F32))
    out_specs = (row, row, row,
                 pl.BlockSpec((None, 512, tm), lambda i: (i // nt, 0, i % nt)), cache_row, cache_row)
    return pl.pallas_call(
        _proj_prompt_kernel, out_shape=out_shape, grid=(nl,),
        in_specs=[pl.BlockSpec((tm, D_MODEL), lambda i: (i, 0)),
                  pl.BlockSpec((D_MODEL, 4 * 512), lambda i: (0, 0)),
                  pl.BlockSpec((512, D_MODEL), lambda i: (0, 0)), anyspec, anyspec],
        out_specs=out_specs, input_output_aliases={3: 4, 4: 5},
        compiler_params=_cparams("parallel"), name="in_proj_prompt",
    )(x2d, w_bf16, wv_t, kbuf, vbuf)


def _sub_ln(a, g, lam_init):
    return a * lax.rsqrt(jnp.mean(a * a, axis=-1, keepdims=True) + RMS_EPS) * g * (1.0 - lam_init)


ALIBI_SPLIT = 64
ATTN_KEY_BLOCK = 2048
ATTN_QUERY_BLOCK = 2048
ATTN_Q_CHUNK = 512
ATTN_SCORE_LOOKAHEAD = 2


def _attn_prompt_kernel(qi_tab, ki_tab, slopes_ref, lam_ref, q_ref, kb_ref, vt_ref, kfeat_ref, vfeat_ref,
                        g_ref, o_ref, qs_scr, m_scr, acc_scr, *, tq, tk, lam_init):
    h, step = pl.program_id(1), pl.program_id(2)
    qi, ki = qi_tab[step], ki_tab[step]
    slope = slopes_ref[h]

    @pl.when(ki == 0)
    def _():
        qs = q_ref[...] * ATTN_SCALE
        lane = lax.broadcasted_iota(jnp.int32, qs.shape, 1)
        qfeat = jnp.where(lane == 0, ALIBI_SPLIT * slope, jnp.where(lane == 1, slope, 0.0)).astype(BF16)
        qs_scr[0:tq, 0:LANES] = jnp.where(lane < HEAD_DIM, qs, 0.0).astype(BF16)
        qs_scr[tq:2 * tq, 0:LANES] = jnp.where(lane >= HEAD_DIM, qs, 0.0).astype(BF16)
        qs_scr[0:tq, LANES:2 * LANES] = qfeat
        qs_scr[tq:2 * tq, LANES:2 * LANES] = qfeat
        m_scr[...] = jnp.full(m_scr.shape, -jnp.inf, F32)
        acc_scr[...] = jnp.zeros(acc_scr.shape, F32)

    def update(koff):
        kp = jnp.concatenate([kb_ref[...], kfeat_ref[...]], axis=1)
        vpt = jnp.concatenate([vt_ref[...], vfeat_ref[...]], axis=0)
        d = slope * (ki * tk - qi * tq).astype(F32)
        qc = ATTN_Q_CHUNK

        def n_keys(c0):
            return tk if koff is None else max(0, min(tk, (c0 % tq) + qc - koff))

        chunks = [c0 for c0 in range(0, 2 * tq, qc) if n_keys(c0) > 0]

        def scores(c0):
            return lax.dot_general(kp[0:n_keys(c0), :], qs_scr[c0:c0 + qc, :], (((1,), (1,)), ((), ())),
                                   preferred_element_type=F32)

        def softmax_part(s, c0):
            q0 = c0 % tq
            if koff is not None and koff + s.shape[0] - 1 > q0:
                key = lax.broadcasted_iota(jnp.int32, s.shape, 0) + koff
                qry = lax.broadcasted_iota(jnp.int32, s.shape, 1) + q0
                s = jnp.where(qry >= key, s, -jnp.inf)
            m_prev = m_scr[:, c0:c0 + qc]
            m_new = jnp.maximum(m_prev, jnp.max(s, axis=0, keepdims=True) + d)
            m_scr[:, c0:c0 + qc] = m_new
            return jnp.exp(s - (m_new - d)).astype(BF16), jnp.exp(m_prev - m_new)

        def accumulate(c0, p, alpha):
            pv = jnp.dot(vpt[:, 0:p.shape[0]], p, preferred_element_type=F32)
            acc_scr[:, c0:c0 + qc] = alpha * acc_scr[:, c0:c0 + qc] + pv

        n = len(chunks)
        s_q = [scores(c0) for c0 in chunks[:ATTN_SCORE_LOOKAHEAD]]
        pending = None
        for i, c0 in enumerate(chunks):
            s = s_q.pop(0)
            if i + ATTN_SCORE_LOOKAHEAD < n:
                s_q.append(scores(chunks[i + ATTN_SCORE_LOOKAHEAD]))
            if pending is not None:
                accumulate(*pending)
            pending = (c0,) + softmax_part(s, c0)
        accumulate(*pending)

    rel = ki - qi * (tq // tk)

    @pl.when(rel < 0)
    def _():
        update(None)

    for j in range(tq // tk):
        @pl.when(rel == j)
        def _(j=j):
            update(j * tk)

    @pl.when(rel == tq // tk - 1)
    def _():
        o1 = acc_scr[0:V_DIM, 0:tq] / acc_scr[V_DIM:V_DIM + 1, 0:tq]
        o2 = acc_scr[0:V_DIM, tq:2 * tq] / acc_scr[V_DIM:V_DIM + 1, tq:2 * tq]
        a = (o1 - lam_ref[0, 0] * o2).T
        o_ref[...] = _sub_ln(a, g_ref[...], lam_init)


def _attn_prompt(q, kb, vt, slopes, lam, subln_g, lam_init):
    b, t, _ = q.shape
    tk = min(ATTN_KEY_BLOCK, t)
    tq = min(ATTN_QUERY_BLOCK, t)
    nq, r = t // tq, tq // tk
    pairs = [(qi, ki) for qi in range(nq) for ki in range(r * (qi + 1))]
    qi_tab = jnp.asarray([p[0] for p in pairs], jnp.int32)
    ki_tab = jnp.asarray([p[1] for p in pairs], jnp.int32)
    c = jnp.arange(tk)[:, None]
    lane = jnp.arange(LANES)[None, :]
    kfeat = jnp.where(lane == 0, c // ALIBI_SPLIT, jnp.where(lane == 1, c % ALIBI_SPLIT, 0)).astype(BF16)
    ones_row = jnp.arange(BF16_SUBLANES)[:, None] == 0
    vfeat = jnp.broadcast_to(jnp.where(ones_row, 1.0, 0.0), (BF16_SUBLANES, tk)).astype(BF16)
    kernel = functools.partial(_attn_prompt_kernel, tq=tq, tk=tk, lam_init=lam_init)
    smem = pl.BlockSpec(memory_space=pltpu.SMEM)
    q_spec = pl.BlockSpec((None, tq, V_DIM), lambda b_, h, s, qt, kt: (b_, qt[s], h))
    k_spec = pl.BlockSpec((None, tk, V_DIM), lambda b_, h, s, qt, kt: (b_, kt[s], h))
    vt_spec = pl.BlockSpec((None, V_DIM, tk), lambda b_, h, s, qt, kt: (b_, h, kt[s]))
    const = lambda shape: pl.BlockSpec(shape, lambda b_, h, s, qt, kt: (0, 0))
    grid_spec = pltpu.PrefetchScalarGridSpec(
        num_scalar_prefetch=2, grid=(b, N_HEADS, len(pairs)),
        in_specs=[smem, smem, q_spec, k_spec, vt_spec, const((tk, LANES)), const((BF16_SUBLANES, tk)),
                  const((1, V_DIM))],
        out_specs=q_spec,
        scratch_shapes=[pltpu.VMEM((2 * tq, 2 * LANES), BF16), pltpu.VMEM((1, 2 * tq), F32),
                        pltpu.VMEM((V_DIM + BF16_SUBLANES, 2 * tq), F32)])
    return pl.pallas_call(
        kernel, out_shape=jax.ShapeDtypeStruct((b, t, ATT_WIDTH), F32), grid_spec=grid_spec,
        compiler_params=_cparams("parallel", "parallel", "arbitrary"), name="attn_prompt",
    )(qi_tab, ki_tab, slopes, lam, q, kb, vt, kfeat, vfeat, subln_g.reshape(1, V_DIM))


def _attn_sample_kernel(pt_ref, lam_ref, q_ref, knew_ref, vnew_ref, g_ref, cslope_ref, *rest,
                        n_pages, dec_seq, lam_init):
    del pt_ref
    k_pages = rest[:n_pages]
    v_pages = rest[n_pages:2 * n_pages]
    o_ref, s_scr = rest[2 * n_pages:]
    past = n_pages * PAGE_SIZE
    cols_per_head = 2 * dec_seq
    head_rows = lambda ref, h: ref[pl.ds(h, PAGE_SIZE, stride=N_HEADS), :]

    q = q_ref[...] * ATTN_SCALE
    lane = lax.broadcasted_iota(jnp.int32, (dec_seq, V_DIM), 1)
    qw = []
    for h in range(N_HEADS):
        qh = q[:, h * V_DIM:(h + 1) * V_DIM]
        blocks = [jnp.where(lane < HEAD_DIM, qh, 0.0), jnp.where(lane >= HEAD_DIM, qh, 0.0)]
        if h:
            blocks.insert(0, jnp.zeros((h * cols_per_head, V_DIM), F32))
        blocks.append(jnp.zeros((LANES - (h + 1) * cols_per_head, V_DIM), F32))
        qw.append(jnp.concatenate(blocks, axis=0))
    dn = (((1,), (1,)), ((), ()))

    s_past = None
    for h0 in range(0, N_HEADS, 2):
        keys = jnp.concatenate(
            [jnp.concatenate([head_rows(k_pages[p], h0), head_rows(k_pages[p], h0 + 1)], axis=1)
             for p in range(n_pages)], axis=0)
        w = jnp.concatenate([qw[h0], qw[h0 + 1]], axis=1)
        part = lax.dot_general(keys, w, dn, preferred_element_type=F32)
        s_past = part if s_past is None else s_past + part
    row = lax.broadcasted_iota(jnp.int32, (past, LANES), 0)
    col = lax.broadcasted_iota(jnp.int32, (past, LANES), 1)
    cslope = cslope_ref[...]
    dist = past + col % dec_seq - row
    s_scr[0:past, :] = s_past - cslope * dist.astype(F32)

    knew = knew_ref[...]
    s_new = None
    for h in range(N_HEADS):
        part = lax.dot_general(knew[:, h * V_DIM:(h + 1) * V_DIM], qw[h], dn, preferred_element_type=F32)
        s_new = part if s_new is None else s_new + part
    row = lax.broadcasted_iota(jnp.int32, (dec_seq, LANES), 0)
    col = lax.broadcasted_iota(jnp.int32, (dec_seq, LANES), 1)
    dist = col % dec_seq - row
    s_scr[past:past + dec_seq, :] = jnp.where(dist >= 0, s_new - cslope * dist.astype(F32), -jnp.inf)
    s_scr[past + dec_seq:past + PAGE_SIZE, :] = jnp.full((PAGE_SIZE - dec_seq, LANES), -jnp.inf, F32)

    s_all = s_scr[...]
    m = jnp.max(s_all, axis=0, keepdims=True)
    p_all = jnp.exp(s_all - m)
    l = jnp.sum(p_all, axis=0, keepdims=True)
    c_i = lax.broadcasted_iota(jnp.int32, (1, LANES), 1)
    sign = jnp.where((c_i // dec_seq) % 2 == 0, 1.0, -lam_ref[0, 0])
    s_scr[...] = p_all * (sign / l)

    vnew = vnew_ref[...]
    vzero = jnp.zeros((PAGE_SIZE - dec_seq, V_DIM), F32)
    pts = [s_scr[p * PAGE_SIZE:(p + 1) * PAGE_SIZE, :].T for p in range(n_pages + 1)]
    for h in range(N_HEADS):
        w_h = jnp.concatenate([pt[h * cols_per_head:(h + 1) * cols_per_head, :] for pt in pts], axis=1)
        v_h = jnp.concatenate([head_rows(v_pages[p], h) for p in range(n_pages)]
                              + [vnew[:, h * V_DIM:(h + 1) * V_DIM], vzero], axis=0)
        out = jnp.dot(w_h, v_h, preferred_element_type=F32)
        a = out[0:dec_seq, :] + out[dec_seq:2 * dec_seq, :]
        o_ref[:, h * V_DIM:(h + 1) * V_DIM] = _sub_ln(a, g_ref[...], lam_init)


def _attn_sample(q, k, v, k_pages, v_pages, page_base, page_table, cslope, lam, subln_g, lam_init):
    n, dec_seq, _ = q.shape
    n_pages = page_table.shape[1]
    assert 2 * N_HEADS * dec_seq <= LANES and dec_seq % SUBLANES == 0 and N_HEADS % 2 == 0
    kernel = functools.partial(_attn_sample_kernel, n_pages=n_pages, dec_seq=dec_seq, lam_init=lam_init)
    tok = pl.BlockSpec((None, dec_seq, ATT_WIDTH), lambda i, pt: (i, 0, 0))

    def page_spec(p):
        return pl.BlockSpec((None, PAGE_SIZE * N_HEADS, V_DIM),
                            lambda i, pt: (page_base + pt[i * n_pages + p], 0, 0))

    grid_spec = pltpu.PrefetchScalarGridSpec(
        num_scalar_prefetch=1, grid=(n,),
        in_specs=[pl.BlockSpec(memory_space=pltpu.SMEM), tok, tok, tok,
                  pl.BlockSpec((1, V_DIM), lambda i, pt: (0, 0)),
                  pl.BlockSpec((1, LANES), lambda i, pt: (0, 0))]
                 + [page_spec(p) for p in range(n_pages)] * 2,
        out_specs=tok,
        scratch_shapes=[pltpu.VMEM(((n_pages + 1) * PAGE_SIZE, LANES), F32)])
    return pl.pallas_call(
        kernel, out_shape=jax.ShapeDtypeStruct((n, dec_seq, ATT_WIDTH), F32), grid_spec=grid_spec,
        compiler_params=_cparams("arbitrary"), name="attn_sample",
    )(page_table.reshape(-1), lam, q, k, v, subln_g.reshape(1, V_DIM), cslope,
      *([k_pages] * n_pages), *([v_pages] * n_pages))


def _s5_scan_chunk(bu_scr, hs_scr, hre_scr, him_scr, are_ref, aim_ref, steps, lane_chunk):
    for c in range(N_STATE // lane_chunk):
        lo = c * lane_chunk
        are = jnp.broadcast_to(are_ref[:, lo:lo + lane_chunk], (SCAN_ROWS, lane_chunk))
        aim = jnp.broadcast_to(aim_ref[:, lo:lo + lane_chunk], (SCAN_ROWS, lane_chunk))

        def body(j, carry, lo=lo, are=are, aim=aim):
            hre, him = carry
            r0 = pl.multiple_of(j * SCAN_ROWS, SCAN_ROWS)
            bre = bu_scr[pl.ds(r0, SCAN_ROWS), lo:lo + lane_chunk]
            bim = bu_scr[pl.ds(r0, SCAN_ROWS), N_STATE + lo:N_STATE + lo + lane_chunk]
            nre = are * hre - aim * him + bre
            nim = are * him + aim * hre + bim
            if hs_scr is not None:
                hs_scr[pl.ds(r0, SCAN_ROWS), lo:lo + lane_chunk] = nre
                hs_scr[pl.ds(r0, SCAN_ROWS), N_STATE + lo:N_STATE + lo + lane_chunk] = nim
            return nre, nim

        hre, him = lax.fori_loop(0, steps, body,
                                 (hre_scr[:, lo:lo + lane_chunk], him_scr[:, lo:lo + lane_chunk]),
                                 unroll=min(4, steps))
        hre_scr[:, lo:lo + lane_chunk] = hre
        him_scr[:, lo:lo + lane_chunk] = him


def _s5_kernel(u_ref, h0re_ref, h0im_ref, are_ref, aim_ref, bbd_ref, *rest, steps, with_y):
    perm_scr, rest = rest[-1], rest[:-1]
    if with_y:
        cbd_ref, d_ref, wglu_ref, bglu_ref, s_ref, hre_ref, him_ref, hre_scr, him_scr, bu_scr, hs_scr = rest
    else:
        hre_ref, him_ref, hre_scr, him_scr, bu_scr = rest
        hs_scr = None
    c = pl.program_id(1)

    @pl.when(c == 0)
    def _():
        hre_scr[...] = h0re_ref[...]
        him_scr[...] = h0im_ref[...]

    n_slabs = SSM_WIDTH // LANES
    for r in range(SCAN_ROWS):
        for k in range(n_slabs):
            perm_scr[k, pl.ds(r, steps, stride=SCAN_ROWS), :] = u_ref[r, :, k * LANES:(k + 1) * LANES]
    u = jnp.concatenate([perm_scr[k] for k in range(n_slabs)], axis=1)
    ub = u.astype(BF16)
    hu, hst = SSM_WIDTH // 2, N_STATE // 2
    for half in range(2):
        for part in range(2):
            cols = slice(part * N_STATE + half * hst, part * N_STATE + (half + 1) * hst)
            bu_scr[:, cols] = jnp.dot(ub[:, half * hu:(half + 1) * hu], bbd_ref[half * hu:(half + 1) * hu, cols],
                                      preferred_element_type=F32)
    _s5_scan_chunk(bu_scr, hs_scr, hre_scr, him_scr, are_ref, aim_ref, steps, lane_chunk=1024)
    if with_y:
        y_halves = []
        for half in range(2):
            ycols = slice(half * hu, (half + 1) * hu)
            acc = None
            for part in range(2):
                rows = slice(part * N_STATE + half * hst, part * N_STATE + (half + 1) * hst)
                term = jnp.dot(hs_scr[:, rows].astype(BF16), cbd_ref[rows, ycols], preferred_element_type=F32)
                acc = term if acc is None else acc + term
            y_halves.append(acc)
        y = jnp.concatenate(y_halves, axis=1) + d_ref[...] * u
        g = jax.nn.gelu(y)
        gate = jnp.dot(g.astype(BF16), wglu_ref[...], preferred_element_type=F32) + bglu_ref[...]
        s = g * jax.nn.sigmoid(gate)
        for k in range(n_slabs):
            perm_scr[k] = s[:, k * LANES:(k + 1) * LANES]
        for r in range(SCAN_ROWS):
            rows = [perm_scr[k, pl.ds(r, steps, stride=SCAN_ROWS), :] for k in range(n_slabs)]
            s_ref[r] = jnp.concatenate(rows, axis=1).astype(s_ref.dtype)

    @pl.when(c == pl.num_programs(1) - 1)
    def _():
        hre_ref[...] = hre_scr[...]
        him_ref[...] = him_scr[...]


def _s5_call(u4, h0re, h0im, a_re, a_im, bbd, tail, steps_per_chunk, with_y):
    ng, _, j_total, _ = u4.shape
    steps = min(steps_per_chunk, j_total)
    nc = j_total // steps
    blk = steps * SCAN_ROWS
    const = lambda shape: pl.BlockSpec(shape, lambda g, c: (0,) * len(shape))
    st_spec = pl.BlockSpec((None, SCAN_ROWS, N_STATE), lambda g, c: (g, 0, 0))
    st_shape = jax.ShapeDtypeStruct((ng, SCAN_ROWS, N_STATE), F32)
    u_spec = pl.BlockSpec((None, SCAN_ROWS, steps, SSM_WIDTH), lambda g, c: (g, 0, c, 0))
    in_specs = [u_spec, st_spec, st_spec, const((1, N_STATE)), const((1, N_STATE)),
                const((SSM_WIDTH, 2 * N_STATE))]
    scratch = [pltpu.VMEM((SCAN_ROWS, N_STATE), F32), pltpu.VMEM((SCAN_ROWS, N_STATE), F32),
               pltpu.VMEM((blk, 2 * N_STATE), F32)]
    if with_y:
        in_specs += [const((2 * N_STATE, SSM_WIDTH)), const((1, SSM_WIDTH)),
                     const((SSM_WIDTH, SSM_WIDTH)), const((1, SSM_WIDTH))]
        out_shape = (jax.ShapeDtypeStruct(u4.shape, BF16), st_shape, st_shape)
        out_specs = (u_spec, st_spec, st_spec)
        scratch.append(pltpu.VMEM((blk, 2 * N_STATE), F32))
    else:
        out_shape = (st_shape, st_shape)
        out_specs = (st_spec, st_spec)
    scratch.append(pltpu.VMEM((SSM_WIDTH // LANES, blk, LANES), F32))
    return pl.pallas_call(
        functools.partial(_s5_kernel, steps=steps, with_y=with_y),
        out_shape=out_shape, grid=(ng, nc), in_specs=in_specs, out_specs=out_specs,
        scratch_shapes=scratch, compiler_params=_cparams("parallel", "arbitrary"),
        name="s5_scan_glu" if with_y else "s5_scan_state",
    )(u4, h0re, h0im, a_re, a_im, bbd, *tail)


def _s5_seg_init_kernel(fre_ref, fim_ref, are_ref, aim_ref, ire_ref, iim_ref, *, seg_len, n_seq, n_seg):
    pr, pi = are_ref[...], aim_ref[...]
    ar, ai = None, None
    e = seg_len
    while e:
        if e & 1:
            ar, ai = (pr, pi) if ar is None else (ar * pr - ai * pi, ar * pi + ai * pr)
        e >>= 1
        if e:
            pr, pi = pr * pr - pi * pi, 2.0 * pr * pi
    for n in range(n_seq):
        hr = jnp.zeros((1, N_STATE), F32)
        hi = jnp.zeros((1, N_STATE), F32)
        for s in range(n_seg):
            r = n * n_seg + s
            ire_ref[r:r + 1, :] = hr
            iim_ref[r:r + 1, :] = hi
            fr, fi = fre_ref[r:r + 1, :], fim_ref[r:r + 1, :]
            hr, hi = ar * hr - ai * hi + fr, ar * hi + ai * hr + fi


def _s5_seg_init(f_re, f_im, a_re, a_im, seg_len, n_seq, n_seg):
    shape = jax.ShapeDtypeStruct(f_re.shape, F32)
    return pl.pallas_call(
        functools.partial(_s5_seg_init_kernel, seg_len=seg_len, n_seq=n_seq, n_seg=n_seg),
        out_shape=(shape, shape), name="s5_seg_init",
    )(f_re, f_im, a_re, a_im)


def _s5_prompt(u, a_re, a_im, bbd, tail):
    b, t, _ = u.shape
    n_seg = SCAN_ROWS // b
    assert b * n_seg == SCAN_ROWS and t % n_seg == 0
    j = t // n_seg
    u4 = u.reshape(1, SCAN_ROWS, j, SSM_WIDTH)
    zeros = jnp.zeros((1, SCAN_ROWS, N_STATE), F32)
    f_re, f_im = _s5_call(u4, zeros, zeros, a_re, a_im, bbd, (), 64, with_y=False)
    i_re, i_im = _s5_seg_init(f_re[0], f_im[0], a_re, a_im, j, b, n_seg)
    s4, h_re, h_im = _s5_call(u4, i_re[None], i_im[None], a_re, a_im, bbd, tail, 64, with_y=True)
    last = h_re.reshape(b, n_seg, N_STATE)[:, -1], h_im.reshape(b, n_seg, N_STATE)[:, -1]
    return s4.reshape(b, t, SSM_WIDTH), last[0], last[1]


def _s5_sample(u, h0_re, h0_im, a_re, a_im, bbd, tail):
    n, j, _ = u.shape
    ng = n // SCAN_ROWS
    s4, h_re, h_im = _s5_call(u.reshape(ng, SCAN_ROWS, j, SSM_WIDTH), h0_re.reshape(ng, SCAN_ROWS, N_STATE),
                              h0_im.reshape(ng, SCAN_ROWS, N_STATE), a_re, a_im, bbd, tail, j, with_y=True)
    return s4.reshape(n, j, SSM_WIDTH), h_re.reshape(n, N_STATE), h_im.reshape(n, N_STATE)


def _top2_sum(a, b, c, d):
    m1, n1 = jnp.maximum(a, b), jnp.minimum(a, b)
    m2, n2 = jnp.maximum(c, d), jnp.minimum(c, d)
    return jnp.maximum(m1, m2) + jnp.maximum(jnp.minimum(m1, m2), jnp.maximum(n1, n2))


def _route_rows(logit_rows, rbias_ref):
    scores = [jax.nn.sigmoid(r) for r in logit_rows]
    sel = [scores[e] + rbias_ref[e] for e in range(N_EXPERTS)]
    gs = [_top2_sum(*sel[EXPERTS_PER_GROUP * g:EXPERTS_PER_GROUP * (g + 1)]) for g in range(N_EXPERT_GROUPS)]
    best_v, best_g = gs[0], jnp.zeros(gs[0].shape, jnp.int32)
    for g in range(1, N_EXPERT_GROUPS):
        upd = gs[g] > best_v
        best_g = jnp.where(upd, g, best_g)
        best_v = jnp.where(upd, gs[g], best_v)
    picked = []
    for e in range(N_EXPERTS):
        g = e // EXPERTS_PER_GROUP
        rank = jnp.zeros(sel[e].shape, jnp.int32)
        for o in range(EXPERTS_PER_GROUP * g, EXPERTS_PER_GROUP * (g + 1)):
            if o != e:
                ahead = (sel[o] >= sel[e]) if o < e else (sel[o] > sel[e])
                rank = rank + ahead.astype(jnp.int32)
        on = jnp.logical_and(best_g == g, rank < 2)
        picked.append(jnp.where(on, scores[e], 0.0))
    den = picked[0]
    for e in range(1, N_EXPERTS):
        den = den + picked[e]
    return [w / den for w in picked]


def _outproj_kernel(rbias_ref, a_ref, s_ref, x_ref, wo_ref, g_ref, b_ref, rw_ref, x1_ref, gates_ref):
    mix = (jnp.dot(a_ref[...].astype(BF16), wo_ref[0:ATT_WIDTH, :], preferred_element_type=F32)
           + jnp.dot(s_ref[...].astype(BF16), wo_ref[ATT_WIDTH:, :], preferred_element_type=F32))
    x1 = _layer_norm(DN_ALPHA * x_ref[...] + mix, g_ref[...], b_ref[...])
    x1_ref[...] = x1
    xh = x1.astype(BF16)
    xl = (x1 - xh.astype(F32)).astype(BF16)
    rwh, rwl = rw_ref[0:D_MODEL, :], rw_ref[D_MODEL:, :]
    logits = (jnp.dot(xh, rwh, preferred_element_type=F32)
              + jnp.dot(xl, rwh, preferred_element_type=F32)
              + jnp.dot(xh, rwl, preferred_element_type=F32))
    lt = logits.T
    gate_rows = _route_rows([lt[e:e + 1, :] for e in range(N_EXPERTS)], rbias_ref)
    tm = lt.shape[1]
    r_i = lax.broadcasted_iota(jnp.int32, (N_EXPERTS, tm), 0)
    gt = jnp.zeros((N_EXPERTS, tm), F32)
    for e in range(N_EXPERTS):
        gt = jnp.where(r_i == e, jnp.broadcast_to(gate_rows[e], (N_EXPERTS, tm)), gt)
    gt = jnp.concatenate([gt, jnp.zeros((LANES - N_EXPERTS, tm), F32)], axis=0)
    gates_ref[...] = gt.T


def _out_proj(a2d, s, x2d, wo_bf16, ln_g, ln_b, rw_cat, router_bias):
    t = x2d.shape[0]
    tm = min(512, t)
    row = lambda w: pl.BlockSpec((tm, w), lambda i: (i, 0))
    const = lambda shape: pl.BlockSpec(shape, lambda i: (0, 0))
    s_spec = row(SSM_WIDTH)
    return pl.pallas_call(
        _outproj_kernel,
        out_shape=(jax.ShapeDtypeStruct((t, D_MODEL), F32), jax.ShapeDtypeStruct((t, LANES), F32)),
        grid=(t // tm,),
        in_specs=[pl.BlockSpec(memory_space=pltpu.SMEM), row(ATT_WIDTH), s_spec, row(D_MODEL),
                  const((D_MODEL, D_MODEL)), const((1, D_MODEL)), const((1, D_MODEL)),
                  const((2 * D_MODEL, LANES))],
        out_specs=(row(D_MODEL), row(LANES)), compiler_params=_cparams("parallel"), name="out_proj_router",
    )(router_bias, a2d, s, x2d, wo_bf16, ln_g.reshape(1, -1), ln_b.reshape(1, -1), rw_cat)


def _moe_kernel(x_ref, gates_ref, wg_ref, wu_ref, wd_ref, g_ref, b_ref, o_ref, xb_scr, acc_scr):
    e = pl.program_id(1)

    @pl.when(e == 0)
    def _():
        xb_scr[...] = x_ref[...].astype(BF16)
        acc_scr[...] = jnp.zeros(acc_scr.shape, F32)

    xb = xb_scr[...]
    hg = jnp.dot(xb, wg_ref[...].astype(BF16), preferred_element_type=F32)
    hu = jnp.dot(xb, wu_ref[...].astype(BF16), preferred_element_type=F32)
    h = jax.nn.silu(hg) * hu
    y = jnp.dot(h.astype(BF16), wd_ref[...].astype(BF16), preferred_element_type=F32)
    gates = gates_ref[...]
    lane = lax.broadcasted_iota(jnp.int32, gates.shape, 1)
    gate = jnp.sum(jnp.where(lane == e, gates, 0.0), axis=-1, keepdims=True)
    acc_scr[...] = acc_scr[...] + gate * y

    @pl.when(e == pl.num_programs(1) - 1)
    def _():
        o_ref[...] = _layer_norm(DN_ALPHA * x_ref[...] + acc_scr[...], g_ref[...], b_ref[...])


def _moe(x2d, gates, wg, wu, wd, layer, ln_g, ln_b):
    t = x2d.shape[0]
    tm = min(MOE_TOKEN_TILE, t)
    row = lambda w: pl.BlockSpec((tm, w), lambda i, e: (i, 0))
    const = lambda shape: pl.BlockSpec(shape, lambda i, e: (0, 0))
    return pl.pallas_call(
        _moe_kernel, out_shape=jax.ShapeDtypeStruct((t, D_MODEL), F32), grid=(t // tm, N_EXPERTS),
        in_specs=[row(D_MODEL), row(LANES),
                  pl.BlockSpec((None, None, D_MODEL, D_EXPERT), lambda i, e: (layer, e, 0, 0)),
                  pl.BlockSpec((None, None, D_MODEL, D_EXPERT), lambda i, e: (layer, e, 0, 0)),
                  pl.BlockSpec((None, None, D_EXPERT, D_MODEL), lambda i, e: (layer, e, 0, 0)),
                  const((1, D_MODEL)), const((1, D_MODEL))],
        out_specs=row(D_MODEL),
        scratch_shapes=[pltpu.VMEM((tm, D_MODEL), BF16), pltpu.VMEM((tm, D_MODEL), F32)],
        compiler_params=_cparams("parallel", "arbitrary"), name="moe_ln",
    )(x2d, gates, wg, wu, wd, ln_g.reshape(1, -1), ln_b.reshape(1, -1))


def kernel(x_prompt, x_sample, cache_k, cache_v, state_ssm_re, state_ssm_im, page_table, w_in, w_out, ln1_g, ln1_b, ln2_g, ln2_b, lambda_q1, lambda_k1, lambda_q2, lambda_k2, subln_g, ssm_lambda_re, ssm_lambda_im, ssm_log_dt, ssm_b_re, ssm_b_im, ssm_c_re, ssm_c_im, ssm_d, ssm_w_glu, ssm_b_glu, router_w, router_bias, moe_w_gate, moe_w_up, moe_w_down):
    b, t, _ = x_prompt.shape
    n, dec_seq, _ = x_sample.shape
    depth = w_in.shape[0]
    n_pool = cache_k.shape[1]

    slopes = 2.0 ** (-8.0 * jnp.arange(1, N_HEADS + 1, dtype=F32) / N_HEADS)
    col = jnp.arange(LANES)
    cslope = jnp.where(col < 2 * N_HEADS * dec_seq, slopes[jnp.minimum(col // (2 * dec_seq), N_HEADS - 1)], 0.0)
    cslope = cslope.reshape(1, LANES).astype(F32)
    rw_pad = jnp.zeros((D_MODEL, LANES), F32).at[:, :N_EXPERTS].set(router_w)
    rw_hi = rw_pad.astype(BF16)
    rw_lo = (rw_pad - rw_hi.astype(F32)).astype(BF16)
    rw_cat = jnp.concatenate([rw_hi, rw_lo], axis=0)

    pages_k = cache_k.reshape(depth * n_pool, PAGE_SIZE * N_HEADS, V_DIM)
    pages_v = cache_v.reshape(depth * n_pool, PAGE_SIZE * N_HEADS, V_DIM)

    yp = x_prompt.reshape(b * t, D_MODEL)
    ys = x_sample.reshape(n * dec_seq, D_MODEL)
    outs = [[] for _ in range(8)]
    kbuf = jnp.zeros((depth * b * t * N_HEADS, V_DIM), F32)
    vbuf = jnp.zeros((depth * b * t * N_HEADS, V_DIM), F32)
    for l in range(depth):
        lam_init = 0.8 - 0.6 * math.exp(-0.3 * l)
        a_re, a_im, bb_re, bb_im, lam = _layer_prep(
            ssm_lambda_re[l], ssm_lambda_im[l], ssm_log_dt[l], ssm_b_re[l], ssm_b_im[l],
            lambda_q1[l], lambda_k1[l], lambda_q2[l], lambda_k2[l], lam_init)
        a_re, a_im = a_re.reshape(1, N_STATE), a_im.reshape(1, N_STATE)
        bbd = jnp.concatenate([_block_diag_in(bb_re), _block_diag_in(bb_im)], axis=1).astype(BF16)
        cbd = jnp.concatenate([_block_diag_out(ssm_c_re[l]), -_block_diag_out(ssm_c_im[l])], axis=0).astype(BF16)
        tail = (cbd, ssm_d[l].reshape(1, -1), ssm_w_glu[l].astype(BF16), ssm_b_glu[l].reshape(1, -1))
        w_in_b, w_out_b = w_in[l].astype(BF16), w_out[l].astype(BF16)

        def finish(x2d, a2d, s2d):
            x1, gates = _out_proj(a2d, s2d, x2d, w_out_b, ln1_g[l], ln1_b[l], rw_cat, router_bias)
            return _moe(x1, gates, moe_w_gate, moe_w_up, moe_w_down, l, ln2_g[l], ln2_b[l])

        u, q, kb, vt, kbuf, vbuf = _in_proj_prompt(yp, w_in_b, w_in_b[:, 3 * 512:].T, kbuf, vbuf, l, b)
        a = _attn_prompt(q.reshape(b, t, -1), kb.reshape(b, t, -1), vt, slopes, lam, subln_g[l], lam_init)
        s, hr, hi = _s5_prompt(u.reshape(b, t, -1), a_re, a_im, bbd, tail)
        yp = finish(yp, a.reshape(b * t, -1), s.reshape(b * t, -1))
        outs[2].append(hr.reshape(b, N_SSM_GROUPS, SSM_STATE))
        outs[3].append(hi.reshape(b, N_SSM_GROUPS, SSM_STATE))

        u, q, k, v = _in_proj_sample(ys, w_in_b)
        a = _attn_sample(q.reshape(n, dec_seq, -1), k.reshape(n, dec_seq, -1), v.reshape(n, dec_seq, -1),
                         pages_k, pages_v, l * n_pool, page_table, cslope, lam, subln_g[l], lam_init)
        s, hr, hi = _s5_sample(u.reshape(n, dec_seq, -1), state_ssm_re[l].reshape(n, N_STATE),
                               state_ssm_im[l].reshape(n, N_STATE), a_re, a_im, bbd, tail)
        ys = finish(ys, a.reshape(n * dec_seq, -1), s.reshape(n * dec_seq, -1))
        outs[4].append(k.reshape(n, dec_seq, N_HEADS, V_DIM))
        outs[5].append(v.reshape(n, dec_seq, N_HEADS, V_DIM))
        outs[6].append(hr.reshape(n, N_SSM_GROUPS, SSM_STATE))
        outs[7].append(hi.reshape(n, N_SSM_GROUPS, SSM_STATE))

    outs[0] = kbuf.reshape(depth, b, t, N_HEADS, V_DIM)
    outs[1] = vbuf.reshape(depth, b, t, N_HEADS, V_DIM)
    outs = [o if isinstance(o, jax.Array) else jnp.stack(o) for o in outs]
    return (yp.reshape(b, t, D_MODEL), ys.reshape(n, dec_seq, D_MODEL)) + tuple(outs)
```

```python
import functools
import math

import jax
import jax.numpy as jnp
from jax import lax
from jax.experimental import pallas as pl
from jax.experimental.pallas import tpu as pltpu

F32 = jnp.float32
BF16 = jnp.bfloat16

D_MODEL = 1024
ATT_WIDTH = 512
SSM_WIDTH = 512
HEAD_DIM = 64
V_DIM = 128
N_HEADS = 4
SSM_GROUP = 16
N_SSM_GROUPS = 32
SSM_STATE = 64
N_STATE = N_SSM_GROUPS * SSM_STATE
N_EXPERTS = 16
EXPERTS_PER_GROUP = 4
N_EXPERT_GROUPS = 4
D_EXPERT = 512
PAGE_SIZE = 128
DEPTH = 2
DN_ALPHA = (2 * DEPTH) ** 0.25
LN_EPS = 1e-5
RMS_EPS = 1e-5
ATTN_SCALE = HEAD_DIM ** -0.5

SUBLANES = 8
BF16_SUBLANES = 16
LANES = 128
SCAN_ROWS = SUBLANES
MOE_TOKEN_TILE = 1024
PROJ_TOKEN_TILE = 1024
S5_STATE_STEPS = 128
S5_OUTPUT_STEPS = 64
VMEM_LIMIT = 52 * 1024 * 1024


def _cparams(*sem):
    return pltpu.CompilerParams(dimension_semantics=sem, vmem_limit_bytes=VMEM_LIMIT)


def _layer_norm(z, g, b):
    mu = jnp.mean(z, -1, keepdims=True)
    var = jnp.mean(jnp.square(z - mu), -1, keepdims=True)
    return (z - mu) * lax.rsqrt(var + LN_EPS) * g + b


def _prep_kernel(lre_ref, lim_ref, ldt_ref, bre_ref, bim_ref, lq1_ref, lk1_ref, lq2_ref, lk2_ref,
                 are_ref, aim_ref, bbre_ref, bbim_ref, lam_ref, *, lam_init):
    lr, li = lre_ref[...], lim_ref[...]
    dt = jnp.exp(ldt_ref[...])
    mag = jnp.exp(lr * dt)
    a_re, a_im = mag * jnp.cos(li * dt), mag * jnp.sin(li * dt)
    den = lr * lr + li * li
    nr = a_re - 1.0
    coef_re = (nr * lr + a_im * li) / den
    coef_im = (a_im * lr - nr * li) / den
    br, bi = bre_ref[...], bim_ref[...]
    bbre_ref[...] = coef_re[..., None] * br - coef_im[..., None] * bi
    bbim_ref[...] = coef_re[..., None] * bi + coef_im[..., None] * br
    are_ref[...] = a_re
    aim_ref[...] = a_im
    s1 = jnp.sum(lq1_ref[...] * lk1_ref[...], axis=-1, keepdims=True)
    s2 = jnp.sum(lq2_ref[...] * lk2_ref[...], axis=-1, keepdims=True)
    lam_ref[...] = jnp.exp(s1) - jnp.exp(s2) + lam_init


def _layer_prep(lam_re, lam_im, log_dt, b_re, b_im, lq1, lk1, lq2, lk2, lam_init):
    g, p, c = N_SSM_GROUPS, SSM_STATE, SSM_GROUP
    out_shape = (jax.ShapeDtypeStruct((g, p), F32), jax.ShapeDtypeStruct((g, p), F32),
                 jax.ShapeDtypeStruct((g, p, c), F32), jax.ShapeDtypeStruct((g, p, c), F32),
                 jax.ShapeDtypeStruct((1, 1), F32))
    return pl.pallas_call(
        functools.partial(_prep_kernel, lam_init=lam_init), out_shape=out_shape, name="layer_prep",
    )(lam_re, lam_im, log_dt.reshape(g, 1), b_re, b_im,
      lq1.reshape(1, HEAD_DIM), lk1.reshape(1, HEAD_DIM), lq2.reshape(1, HEAD_DIM), lk2.reshape(1, HEAD_DIM))


def _block_diag_in(bb):
    eye = jnp.eye(N_SSM_GROUPS, dtype=bb.dtype)
    m = jnp.transpose(bb, (0, 2, 1))[:, :, None, :] * eye[:, None, :, None]
    return m.reshape(SSM_WIDTH, N_STATE)


def _block_diag_out(c):
    eye = jnp.eye(N_SSM_GROUPS, dtype=c.dtype)
    m = jnp.transpose(c, (0, 2, 1))[:, :, None, :] * eye[:, None, :, None]
    return m.reshape(N_STATE, SSM_WIDTH)


def _proj_sample_kernel(x_ref, w_ref, u_ref, q_ref, k_ref, v_ref):
    xb = x_ref[...].astype(BF16)
    for i, o_ref in enumerate((u_ref, q_ref, k_ref, v_ref)):
        o_ref[...] = jnp.dot(xb, w_ref[:, i * 512:(i + 1) * 512], preferred_element_type=F32)


def _in_proj_sample(x2d, w_bf16):
    t = x2d.shape[0]
    tm = min(PROJ_TOKEN_TILE, t)
    out = jax.ShapeDtypeStruct((t, 512), F32)
    row = pl.BlockSpec((tm, 512), lambda i: (i, 0))
    return pl.pallas_call(
        _proj_sample_kernel, out_shape=(out,) * 4, grid=(t // tm,),
        in_specs=[pl.BlockSpec((tm, D_MODEL), lambda i: (i, 0)),
                  pl.BlockSpec((D_MODEL, 4 * 512), lambda i: (0, 0))],
        out_specs=(row,) * 4, compiler_params=_cparams("parallel"), name="in_proj_sample",
    )(x2d, w_bf16)


def _proj_prompt_kernel(x_ref, w_ref, wvt_ref, kbuf_in, vbuf_in, u_ref, q_ref, kb_ref, vt_ref, k_ref, v_ref):
    del kbuf_in, vbuf_in
    xb = x_ref[...].astype(BF16)
    tm = xb.shape[0]
    col = lambda i: jnp.dot(xb, w_ref[:, i * 512:(i + 1) * 512], preferred_element_type=F32)
    u_ref[...] = col(0)
    q_ref[...] = col(1)
    k, v = col(2), col(3)
    kb_ref[...] = k.astype(BF16)
    for h in range(N_HEADS):
        k_ref[pl.ds(h, tm, stride=N_HEADS), :] = k[:, h * V_DIM:(h + 1) * V_DIM]
        v_ref[pl.ds(h, tm, stride=N_HEADS), :] = v[:, h * V_DIM:(h + 1) * V_DIM]
    vt = lax.dot_general(wvt_ref[...], xb, (((1,), (1,)), ((), ())), preferred_element_type=F32)
    vt_ref[...] = vt.astype(BF16)


def _in_proj_prompt(x2d, w_bf16, wv_t, kbuf, vbuf, layer, n_seq):
    t = x2d.shape[0]
    seq_len = t // n_seq
    tm = min(PROJ_TOKEN_TILE, seq_len)
    assert seq_len % tm == 0
    nt, nl = seq_len // tm, t // tm
    row = pl.BlockSpec((tm, 512), lambda i: (i, 0))
    cache_row = pl.BlockSpec((tm * N_HEADS, V_DIM), lambda i: (layer * nl + i, 0))
    anyspec = pl.BlockSpec(memory_space=pl.ANY)
    out_shape = (jax.ShapeDtypeStruct((t, 512), F32), jax.ShapeDtypeStruct((t, 512), F32),
                 jax.ShapeDtypeStruct((t, 512), BF16), jax.ShapeDtypeStruct((n_seq, 512, seq_len), BF16),
                 jax.ShapeDtypeStruct(kbuf.shape, F32), jax.ShapeDtypeStruct(vbuf.shape, F32))
    out_specs = (row, row, row,
                 pl.BlockSpec((None, 512, tm), lambda i: (i // nt, 0, i % nt)), cache_row, cache_row)
    return pl.pallas_call(
        _proj_prompt_kernel, out_shape=out_shape, grid=(nl,),
        in_specs=[pl.BlockSpec((tm, D_MODEL), lambda i: (i, 0)),
                  pl.BlockSpec((D_MODEL, 4 * 512), lambda i: (0, 0)),
                  pl.BlockSpec((512, D_MODEL), lambda i: (0, 0)), anyspec, anyspec],
        out_specs=out_specs, input_output_aliases={3: 4, 4: 5},
        compiler_params=_cparams("parallel"), name="in_proj_prompt",
    )(x2d, w_bf16, wv_t, kbuf, vbuf)


def _sub_ln(a, g, lam_init):
    return a * lax.rsqrt(jnp.mean(a * a, axis=-1, keepdims=True) + RMS_EPS) * g * (1.0 - lam_init)


ALIBI_SPLIT = 64
ATTN_KEY_BLOCK = 2048
ATTN_QUERY_BLOCK = 2048
ATTN_Q_CHUNK = 512
ATTN_SCORE_LOOKAHEAD = 2


def _attn_prompt_kernel(qi_tab, ki_tab, slopes_ref, lam_ref, q_ref, kb_ref, vt_ref, kfeat_ref, vfeat_ref,
                        g_ref, o_ref, qs_scr, m_scr, acc_scr, *, tq, tk, lam_init):
    h, step = pl.program_id(1), pl.program_id(2)
    qi, ki = qi_tab[step], ki_tab[step]
    slope = slopes_ref[h]

    @pl.when(ki == 0)
    def _():
        qs = q_ref[...] * ATTN_SCALE
        lane = lax.broadcasted_iota(jnp.int32, qs.shape, 1)
        qfeat = jnp.where(lane == 0, ALIBI_SPLIT * slope, jnp.where(lane == 1, slope, 0.0)).astype(BF16)
        qs_scr[0:tq, 0:LANES] = jnp.where(lane < HEAD_DIM, qs, 0.0).astype(BF16)
        qs_scr[tq:2 * tq, 0:LANES] = jnp.where(lane >= HEAD_DIM, qs, 0.0).astype(BF16)
        qs_scr[0:tq, LANES:2 * LANES] = qfeat
        qs_scr[tq:2 * tq, LANES:2 * LANES] = qfeat
        m_scr[...] = jnp.full(m_scr.shape, -jnp.inf, F32)
        acc_scr[...] = jnp.zeros(acc_scr.shape, F32)

    def update(koff):
        kp = jnp.concatenate([kb_ref[...], kfeat_ref[...]], axis=1)
        vpt = jnp.concatenate([vt_ref[...], vfeat_ref[...]], axis=0)
        d = slope * (ki * tk - qi * tq).astype(F32)
        qc = ATTN_Q_CHUNK

        def n_keys(c0):
            return tk if koff is None else max(0, min(tk, (c0 % tq) + qc - koff))

        chunks = [c0 for c0 in range(0, 2 * tq, qc) if n_keys(c0) > 0]

        def scores(c0):
            return lax.dot_general(kp[0:n_keys(c0), :], qs_scr[c0:c0 + qc, :], (((1,), (1,)), ((), ())),
                                   preferred_element_type=F32)

        def softmax_part(s, c0):
            q0 = c0 % tq
            if koff is not None and koff + s.shape[0] - 1 > q0:
                key = lax.broadcasted_iota(jnp.int32, s.shape, 0) + koff
                qry = lax.broadcasted_iota(jnp.int32, s.shape, 1) + q0
                s = jnp.where(qry >= key, s, -jnp.inf)
            m_prev = m_scr[:, c0:c0 + qc]
            m_new = jnp.maximum(m_prev, jnp.max(s, axis=0, keepdims=True) + d)
            m_scr[:, c0:c0 + qc] = m_new
            return jnp.exp(s - (m_new - d)).astype(BF16), jnp.exp(m_prev - m_new)

        def accumulate(c0, p, alpha):
            pv = jnp.dot(vpt[:, 0:p.shape[0]], p, preferred_element_type=F32)
            acc_scr[:, c0:c0 + qc] = alpha * acc_scr[:, c0:c0 + qc] + pv

        n = len(chunks)
        s_q = [scores(c0) for c0 in chunks[:ATTN_SCORE_LOOKAHEAD]]
        pending = None
        for i, c0 in enumerate(chunks):
            s = s_q.pop(0)
            if i + ATTN_SCORE_LOOKAHEAD < n:
                s_q.append(scores(chunks[i + ATTN_SCORE_LOOKAHEAD]))
            if pending is not None:
                accumulate(*pending)
            pending = (c0,) + softmax_part(s, c0)
        accumulate(*pending)

    rel = ki - qi * (tq // tk)

    @pl.when(rel < 0)
    def _():
        update(None)

    for j in range(tq // tk):
        @pl.when(rel == j)
        def _(j=j):
            update(j * tk)

    @pl.when(rel == tq // tk - 1)
    def _():
        o1 = acc_scr[0:V_DIM, 0:tq] / acc_scr[V_DIM:V_DIM + 1, 0:tq]
        o2 = acc_scr[0:V_DIM, tq:2 * tq] / acc_scr[V_DIM:V_DIM + 1, tq:2 * tq]
        a = (o1 - lam_ref[0, 0] * o2).T
        o_ref[...] = _sub_ln(a, g_ref[...], lam_init)


def _attn_prompt(q, kb, vt, slopes, lam, subln_g, lam_init):
    b, t, _ = q.shape
    tk = min(ATTN_KEY_BLOCK, t)
    tq = min(ATTN_QUERY_BLOCK, t)
    nq, r = t // tq, tq // tk
    pairs = [(qi, ki) for qi in range(nq) for ki in range(r * (qi + 1))]
    qi_tab = jnp.asarray([p[0] for p in pairs], jnp.int32)
    ki_tab = jnp.asarray([p[1] for p in pairs], jnp.int32)
    c = jnp.arange(tk)[:, None]
    lane = jnp.arange(LANES)[None, :]
    kfeat = jnp.where(lane == 0, c // ALIBI_SPLIT, jnp.where(lane == 1, c % ALIBI_SPLIT, 0)).astype(BF16)
    ones_row = jnp.arange(BF16_SUBLANES)[:, None] == 0
    vfeat = jnp.broadcast_to(jnp.where(ones_row, 1.0, 0.0), (BF16_SUBLANES, tk)).astype(BF16)
    kernel = functools.partial(_attn_prompt_kernel, tq=tq, tk=tk, lam_init=lam_init)
    smem = pl.BlockSpec(memory_space=pltpu.SMEM)
    q_spec = pl.BlockSpec((None, tq, V_DIM), lambda b_, h, s, qt, kt: (b_, qt[s], h))
    k_spec = pl.BlockSpec((None, tk, V_DIM), lambda b_, h, s, qt, kt: (b_, kt[s], h))
    vt_spec = pl.BlockSpec((None, V_DIM, tk), lambda b_, h, s, qt, kt: (b_, h, kt[s]))
    const = lambda shape: pl.BlockSpec(shape, lambda b_, h, s, qt, kt: (0, 0))
    grid_spec = pltpu.PrefetchScalarGridSpec(
        num_scalar_prefetch=2, grid=(b, N_HEADS, len(pairs)),
        in_specs=[smem, smem, q_spec, k_spec, vt_spec, const((tk, LANES)), const((BF16_SUBLANES, tk)),
                  const((1, V_DIM))],
        out_specs=q_spec,
        scratch_shapes=[pltpu.VMEM((2 * tq, 2 * LANES), BF16), pltpu.VMEM((1, 2 * tq), F32),
                        pltpu.VMEM((V_DIM + BF16_SUBLANES, 2 * tq), F32)])
    return pl.pallas_call(
        kernel, out_shape=jax.ShapeDtypeStruct((b, t, ATT_WIDTH), F32), grid_spec=grid_spec,
        compiler_params=_cparams("parallel", "parallel", "arbitrary"), name="attn_prompt",
    )(qi_tab, ki_tab, slopes, lam, q, kb, vt, kfeat, vfeat, subln_g.reshape(1, V_DIM))


def _attn_sample_kernel(pt_ref, lam_ref, q_ref, knew_ref, vnew_ref, g_ref, cslope_ref, *rest,
                        n_pages, dec_seq, lam_init):
    del pt_ref
    k_pages = rest[:n_pages]
    v_pages = rest[n_pages:2 * n_pages]
    o_ref, s_scr = rest[2 * n_pages:]
    past = n_pages * PAGE_SIZE
    cols_per_head = 2 * dec_seq
    head_rows = lambda ref, h: ref[pl.ds(h, PAGE_SIZE, stride=N_HEADS), :]

    q = q_ref[...] * ATTN_SCALE
    lane = lax.broadcasted_iota(jnp.int32, (dec_seq, V_DIM), 1)
    qw = []
    for h in range(N_HEADS):
        qh = q[:, h * V_DIM:(h + 1) * V_DIM]
        blocks = [jnp.where(lane < HEAD_DIM, qh, 0.0), jnp.where(lane >= HEAD_DIM, qh, 0.0)]
        if h:
            blocks.insert(0, jnp.zeros((h * cols_per_head, V_DIM), F32))
        blocks.append(jnp.zeros((LANES - (h + 1) * cols_per_head, V_DIM), F32))
        qw.append(jnp.concatenate(blocks, axis=0))
    dn = (((1,), (1,)), ((), ()))

    s_past = None
    for h0 in range(0, N_HEADS, 2):
        keys = jnp.concatenate(
            [jnp.concatenate([head_rows(k_pages[p], h0), head_rows(k_pages[p], h0 + 1)], axis=1)
             for p in range(n_pages)], axis=0)
        w = jnp.concatenate([qw[h0], qw[h0 + 1]], axis=1)
        part = lax.dot_general(keys, w, dn, preferred_element_type=F32)
        s_past = part if s_past is None else s_past + part
    row = lax.broadcasted_iota(jnp.int32, (past, LANES), 0)
    col = lax.broadcasted_iota(jnp.int32, (past, LANES), 1)
    cslope = cslope_ref[...]
    dist = past + col % dec_seq - row
    s_scr[0:past, :] = s_past - cslope * dist.astype(F32)

    knew = knew_ref[...]
    s_new = None
    for h in range(N_HEADS):
        part = lax.dot_general(knew[:, h * V_DIM:(h + 1) * V_DIM], qw[h], dn, preferred_element_type=F32)
        s_new = part if s_new is None else s_new + part
    row = lax.broadcasted_iota(jnp.int32, (dec_seq, LANES), 0)
    col = lax.broadcasted_iota(jnp.int32, (dec_seq, LANES), 1)
    dist = col % dec_seq - row
    s_scr[past:past + dec_seq, :] = jnp.where(dist >= 0, s_new - cslope * dist.astype(F32), -jnp.inf)
    s_scr[past + dec_seq:past + PAGE_SIZE, :] = jnp.full((PAGE_SIZE - dec_seq, LANES), -jnp.inf, F32)

    s_all = s_scr[...]
    m = jnp.max(s_all, axis=0, keepdims=True)
    p_all = jnp.exp(s_all - m)
    l = jnp.sum(p_all, axis=0, keepdims=True)
    c_i = lax.broadcasted_iota(jnp.int32, (1, LANES), 1)
    sign = jnp.where((c_i // dec_seq) % 2 == 0, 1.0, -lam_ref[0, 0])
    s_scr[...] = p_all * (sign / l)

    vnew = vnew_ref[...]
    vzero = jnp.zeros((PAGE_SIZE - dec_seq, V_DIM), F32)
    pts = [s_scr[p * PAGE_SIZE:(p + 1) * PAGE_SIZE, :].T for p in range(n_pages + 1)]
    for h in range(N_HEADS):
        w_h = jnp.concatenate([pt[h * cols_per_head:(h + 1) * cols_per_head, :] for pt in pts], axis=1)
        v_h = jnp.concatenate([head_rows(v_pages[p], h) for p in range(n_pages)]
                              + [vnew[:, h * V_DIM:(h + 1) * V_DIM], vzero], axis=0)
        out = jnp.dot(w_h, v_h, preferred_element_type=F32)
        a = out[0:dec_seq, :] + out[dec_seq:2 * dec_seq, :]
        o_ref[:, h * V_DIM:(h + 1) * V_DIM] = _sub_ln(a, g_ref[...], lam_init)


def _attn_sample(q, k, v, k_pages, v_pages, page_base, page_table, cslope, lam, subln_g, lam_init):
    n, dec_seq, _ = q.shape
    n_pages = page_table.shape[1]
    assert 2 * N_HEADS * dec_seq <= LANES and dec_seq % SUBLANES == 0 and N_HEADS % 2 == 0
    kernel = functools.partial(_attn_sample_kernel, n_pages=n_pages, dec_seq=dec_seq, lam_init=lam_init)
    tok = pl.BlockSpec((None, dec_seq, ATT_WIDTH), lambda i, pt: (i, 0, 0))

    def page_spec(p):
        return pl.BlockSpec((None, PAGE_SIZE * N_HEADS, V_DIM),
                            lambda i, pt: (page_base + pt[i * n_pages + p], 0, 0))

    grid_spec = pltpu.PrefetchScalarGridSpec(
        num_scalar_prefetch=1, grid=(n,),
        in_specs=[pl.BlockSpec(memory_space=pltpu.SMEM), tok, tok, tok,
                  pl.BlockSpec((1, V_DIM), lambda i, pt: (0, 0)),
                  pl.BlockSpec((1, LANES), lambda i, pt: (0, 0))]
                 + [page_spec(p) for p in range(n_pages)] * 2,
        out_specs=tok,
        scratch_shapes=[pltpu.VMEM(((n_pages + 1) * PAGE_SIZE, LANES), F32)])
    return pl.pallas_call(
        kernel, out_shape=jax.ShapeDtypeStruct((n, dec_seq, ATT_WIDTH), F32), grid_spec=grid_spec,
        compiler_params=_cparams("arbitrary"), name="attn_sample",
    )(page_table.reshape(-1), lam, q, k, v, subln_g.reshape(1, V_DIM), cslope,
      *([k_pages] * n_pages), *([v_pages] * n_pages))


def _s5_scan_chunk(bu_scr, hs_scr, hre_scr, him_scr, are_ref, aim_ref, steps, lane_chunk):
    for c in range(N_STATE // lane_chunk):
        lo = c * lane_chunk
        are = jnp.broadcast_to(are_ref[:, lo:lo + lane_chunk], (SCAN_ROWS, lane_chunk))
        aim = jnp.broadcast_to(aim_ref[:, lo:lo + lane_chunk], (SCAN_ROWS, lane_chunk))

        def body(j, carry, lo=lo, are=are, aim=aim):
            hre, him = carry
            r0 = pl.multiple_of(j * SCAN_ROWS, SCAN_ROWS)
            bre = bu_scr[pl.ds(r0, SCAN_ROWS), lo:lo + lane_chunk]
            bim = bu_scr[pl.ds(r0, SCAN_ROWS), N_STATE + lo:N_STATE + lo + lane_chunk]
            nre = are * hre - aim * him + bre
            nim = are * him + aim * hre + bim
            if hs_scr is not None:
                hs_scr[pl.ds(r0, SCAN_ROWS), lo:lo + lane_chunk] = nre
                hs_scr[pl.ds(r0, SCAN_ROWS), N_STATE + lo:N_STATE + lo + lane_chunk] = nim
            return nre, nim

        hre, him = lax.fori_loop(0, steps, body,
                                 (hre_scr[:, lo:lo + lane_chunk], him_scr[:, lo:lo + lane_chunk]),
                                 unroll=min(4, steps))
        hre_scr[:, lo:lo + lane_chunk] = hre
        him_scr[:, lo:lo + lane_chunk] = him


def _s5_kernel(u_ref, h0re_ref, h0im_ref, are_ref, aim_ref, bbd_ref, *rest, steps, with_y):
    perm_scr, rest = rest[-1], rest[:-1]
    if with_y:
        cbd_ref, d_ref, wglu_ref, bglu_ref, s_ref, hre_ref, him_ref, hre_scr, him_scr, bu_scr, hs_scr = rest
    else:
        hre_ref, him_ref, hre_scr, him_scr, bu_scr = rest
        hs_scr = None
    c = pl.program_id(1)

    @pl.when(c == 0)
    def _():
        hre_scr[...] = h0re_ref[...]
        him_scr[...] = h0im_ref[...]

    n_slabs = SSM_WIDTH // LANES
    for r in range(SCAN_ROWS):
        for k in range(n_slabs):
            perm_scr[k, pl.ds(r, steps, stride=SCAN_ROWS), :] = u_ref[r, :, k * LANES:(k + 1) * LANES]
    u = jnp.concatenate([perm_scr[k] for k in range(n_slabs)], axis=1)
    ub = u.astype(BF16)
    hu, hst = SSM_WIDTH // 2, N_STATE // 2
    for half in range(2):
        for part in range(2):
            cols = slice(part * N_STATE + half * hst, part * N_STATE + (half + 1) * hst)
            bu_scr[:, cols] = jnp.dot(ub[:, half * hu:(half + 1) * hu], bbd_ref[half * hu:(half + 1) * hu, cols],
                                      preferred_element_type=F32)
    _s5_scan_chunk(bu_scr, hs_scr, hre_scr, him_scr, are_ref, aim_ref, steps, lane_chunk=1024)
    if with_y:
        y_halves = []
        for half in range(2):
            ycols = slice(half * hu, (half + 1) * hu)
            acc = None
            for part in range(2):
                rows = slice(part * N_STATE + half * hst, part * N_STATE + (half + 1) * hst)
                term = jnp.dot(hs_scr[:, rows].astype(BF16), cbd_ref[rows, ycols], preferred_element_type=F32)
                acc = term if acc is None else acc + term
            y_halves.append(acc)
        y = jnp.concatenate(y_halves, axis=1) + d_ref[...] * u
        g = jax.nn.gelu(y)
        gate = jnp.dot(g.astype(BF16), wglu_ref[...], preferred_element_type=F32) + bglu_ref[...]
        s = g * jax.nn.sigmoid(gate)
        for k in range(n_slabs):
            perm_scr[k] = s[:, k * LANES:(k + 1) * LANES]
        for r in range(SCAN_ROWS):
            rows = [perm_scr[k, pl.ds(r, steps, stride=SCAN_ROWS), :] for k in range(n_slabs)]
            s_ref[r] = jnp.concatenate(rows, axis=1).astype(s_ref.dtype)

    @pl.when(c == pl.num_programs(1) - 1)
    def _():
        hre_ref[...] = hre_scr[...]
        him_ref[...] = him_scr[...]


def _s5_call(u4, h0re, h0im, a_re, a_im, bbd, tail, steps_per_chunk, with_y):
    ng, _, j_total, _ = u4.shape
    steps = min(steps_per_chunk, j_total)
    nc = j_total // steps
    blk = steps * SCAN_ROWS
    const = lambda shape: pl.BlockSpec(shape, lambda g, c: (0,) * len(shape))
    st_spec = pl.BlockSpec((None, SCAN_ROWS, N_STATE), lambda g, c: (g, 0, 0))
    st_shape = jax.ShapeDtypeStruct((ng, SCAN_ROWS, N_STATE), F32)
    u_spec = pl.BlockSpec((None, SCAN_ROWS, steps, SSM_WIDTH), lambda g, c: (g, 0, c, 0))
    in_specs = [u_spec, st_spec, st_spec, const((1, N_STATE)), const((1, N_STATE)),
                const((SSM_WIDTH, 2 * N_STATE))]
    scratch = [pltpu.VMEM((SCAN_ROWS, N_STATE), F32), pltpu.VMEM((SCAN_ROWS, N_STATE), F32),
               pltpu.VMEM((blk, 2 * N_STATE), F32)]
    if with_y:
        in_specs += [const((2 * N_STATE, SSM_WIDTH)), const((1, SSM_WIDTH)),
                     const((SSM_WIDTH, SSM_WIDTH)), const((1, SSM_WIDTH))]
        out_shape = (jax.ShapeDtypeStruct(u4.shape, BF16), st_shape, st_shape)
        out_specs = (u_spec, st_spec, st_spec)
        scratch.append(pltpu.VMEM((blk, 2 * N_STATE), F32))
    else:
        out_shape = (st_shape, st_shape)
        out_specs = (st_spec, st_spec)
    scratch.append(pltpu.VMEM((SSM_WIDTH // LANES, blk, LANES), F32))
    return pl.pallas_call(
        functools.partial(_s5_kernel, steps=steps, with_y=with_y),
        out_shape=out_shape, grid=(ng, nc), in_specs=in_specs, out_specs=out_specs,
        scratch_shapes=scratch, compiler_params=_cparams("parallel", "arbitrary"),
        name="s5_scan_glu" if with_y else "s5_scan_state",
    )(u4, h0re, h0im, a_re, a_im, bbd, *tail)


def _s5_seg_init_kernel(fre_ref, fim_ref, are_ref, aim_ref, ire_ref, iim_ref, *, seg_len, n_seq, n_seg):
    pr, pi = are_ref[...], aim_ref[...]
    ar, ai = None, None
    e = seg_len
    while e:
        if e & 1:
            ar, ai = (pr, pi) if ar is None else (ar * pr - ai * pi, ar * pi + ai * pr)
        e >>= 1
        if e:
            pr, pi = pr * pr - pi * pi, 2.0 * pr * pi
    for n in range(n_seq):
        hr = jnp.zeros((1, N_STATE), F32)
        hi = jnp.zeros((1, N_STATE), F32)
        for s in range(n_seg):
            r = n * n_seg + s
            ire_ref[r:r + 1, :] = hr
            iim_ref[r:r + 1, :] = hi
            fr, fi = fre_ref[r:r + 1, :], fim_ref[r:r + 1, :]
            hr, hi = ar * hr - ai * hi + fr, ar * hi + ai * hr + fi


def _s5_seg_init(f_re, f_im, a_re, a_im, seg_len, n_seq, n_seg):
    shape = jax.ShapeDtypeStruct(f_re.shape, F32)
    return pl.pallas_call(
        functools.partial(_s5_seg_init_kernel, seg_len=seg_len, n_seq=n_seq, n_seg=n_seg),
        out_shape=(shape, shape), name="s5_seg_init",
    )(f_re, f_im, a_re, a_im)


def _s5_prompt(u, a_re, a_im, bbd, tail):
    b, t, _ = u.shape
    n_seg = SCAN_ROWS // b
    assert b * n_seg == SCAN_ROWS and t % n_seg == 0
    j = t // n_seg
    u4 = u.reshape(1, SCAN_ROWS, j, SSM_WIDTH)
    zeros = jnp.zeros((1, SCAN_ROWS, N_STATE), F32)
    f_re, f_im = _s5_call(u4, zeros, zeros, a_re, a_im, bbd, (), S5_STATE_STEPS, with_y=False)
    i_re, i_im = _s5_seg_init(f_re[0], f_im[0], a_re, a_im, j, b, n_seg)
    s4, h_re, h_im = _s5_call(u4, i_re[None], i_im[None], a_re, a_im, bbd, tail, S5_OUTPUT_STEPS,
                              with_y=True)
    last = h_re.reshape(b, n_seg, N_STATE)[:, -1], h_im.reshape(b, n_seg, N_STATE)[:, -1]
    return s4.reshape(b, t, SSM_WIDTH), last[0], last[1]


def _s5_sample(u, h0_re, h0_im, a_re, a_im, bbd, tail):
    n, j, _ = u.shape
    ng = n // SCAN_ROWS
    s4, h_re, h_im = _s5_call(u.reshape(ng, SCAN_ROWS, j, SSM_WIDTH), h0_re.reshape(ng, SCAN_ROWS, N_STATE),
                              h0_im.reshape(ng, SCAN_ROWS, N_STATE), a_re, a_im, bbd, tail, j, with_y=True)
    return s4.reshape(n, j, SSM_WIDTH), h_re.reshape(n, N_STATE), h_im.reshape(n, N_STATE)


def _top2_sum(a, b, c, d):
    m1, n1 = jnp.maximum(a, b), jnp.minimum(a, b)
    m2, n2 = jnp.maximum(c, d), jnp.minimum(c, d)
    return jnp.maximum(m1, m2) + jnp.maximum(jnp.minimum(m1, m2), jnp.maximum(n1, n2))


def _route_rows(logit_rows, rbias_ref):
    scores = [jax.nn.sigmoid(r) for r in logit_rows]
    sel = [scores[e] + rbias_ref[e] for e in range(N_EXPERTS)]
    gs = [_top2_sum(*sel[EXPERTS_PER_GROUP * g:EXPERTS_PER_GROUP * (g + 1)]) for g in range(N_EXPERT_GROUPS)]
    best_v, best_g = gs[0], jnp.zeros(gs[0].shape, jnp.int32)
    for g in range(1, N_EXPERT_GROUPS):
        upd = gs[g] > best_v
        best_g = jnp.where(upd, g, best_g)
        best_v = jnp.where(upd, gs[g], best_v)
    picked = []
    for e in range(N_EXPERTS):
        g = e // EXPERTS_PER_GROUP
        rank = jnp.zeros(sel[e].shape, jnp.int32)
        for o in range(EXPERTS_PER_GROUP * g, EXPERTS_PER_GROUP * (g + 1)):
            if o != e:
                ahead = (sel[o] >= sel[e]) if o < e else (sel[o] > sel[e])
                rank = rank + ahead.astype(jnp.int32)
        on = jnp.logical_and(best_g == g, rank < 2)
        picked.append(jnp.where(on, scores[e], 0.0))
    den = picked[0]
    for e in range(1, N_EXPERTS):
        den = den + picked[e]
    return [w / den for w in picked]


def _outproj_kernel(rbias_ref, a_ref, s_ref, x_ref, wo_ref, g_ref, b_ref, rw_ref, x1_ref, gates_ref):
    mix = (jnp.dot(a_ref[...].astype(BF16), wo_ref[0:ATT_WIDTH, :], preferred_element_type=F32)
           + jnp.dot(s_ref[...].astype(BF16), wo_ref[ATT_WIDTH:, :], preferred_element_type=F32))
    x1 = _layer_norm(DN_ALPHA * x_ref[...] + mix, g_ref[...], b_ref[...])
    x1_ref[...] = x1
    xh = x1.astype(BF16)
    xl = (x1 - xh.astype(F32)).astype(BF16)
    rwh, rwl = rw_ref[0:D_MODEL, :], rw_ref[D_MODEL:, :]
    logits = (jnp.dot(xh, rwh, preferred_element_type=F32)
              + jnp.dot(xl, rwh, preferred_element_type=F32)
              + jnp.dot(xh, rwl, preferred_element_type=F32))
    lt = logits.T
    gate_rows = _route_rows([lt[e:e + 1, :] for e in range(N_EXPERTS)], rbias_ref)
    tm = lt.shape[1]
    r_i = lax.broadcasted_iota(jnp.int32, (N_EXPERTS, tm), 0)
    gt = jnp.zeros((N_EXPERTS, tm), F32)
    for e in range(N_EXPERTS):
        gt = jnp.where(r_i == e, jnp.broadcast_to(gate_rows[e], (N_EXPERTS, tm)), gt)
    gt = jnp.concatenate([gt, jnp.zeros((LANES - N_EXPERTS, tm), F32)], axis=0)
    gates_ref[...] = gt.T


def _out_proj(a2d, s, x2d, wo_bf16, ln_g, ln_b, rw_cat, router_bias):
    t = x2d.shape[0]
    tm = min(PROJ_TOKEN_TILE, t)
    row = lambda w: pl.BlockSpec((tm, w), lambda i: (i, 0))
    const = lambda shape: pl.BlockSpec(shape, lambda i: (0, 0))
    s_spec = row(SSM_WIDTH)
    return pl.pallas_call(
        _outproj_kernel,
        out_shape=(jax.ShapeDtypeStruct((t, D_MODEL), F32), jax.ShapeDtypeStruct((t, LANES), F32)),
        grid=(t // tm,),
        in_specs=[pl.BlockSpec(memory_space=pltpu.SMEM), row(ATT_WIDTH), s_spec, row(D_MODEL),
                  const((D_MODEL, D_MODEL)), const((1, D_MODEL)), const((1, D_MODEL)),
                  const((2 * D_MODEL, LANES))],
        out_specs=(row(D_MODEL), row(LANES)), compiler_params=_cparams("parallel"), name="out_proj_router",
    )(router_bias, a2d, s, x2d, wo_bf16, ln_g.reshape(1, -1), ln_b.reshape(1, -1), rw_cat)


def _moe_kernel(x_ref, gates_ref, wg_ref, wu_ref, wd_ref, g_ref, b_ref, o_ref, xb_scr, acc_scr):
    e = pl.program_id(1)

    @pl.when(e == 0)
    def _():
        xb_scr[...] = x_ref[...].astype(BF16)
        acc_scr[...] = jnp.zeros(acc_scr.shape, F32)

    xb = xb_scr[...]
    hg = jnp.dot(xb, wg_ref[...].astype(BF16), preferred_element_type=F32)
    hu = jnp.dot(xb, wu_ref[...].astype(BF16), preferred_element_type=F32)
    h = jax.nn.silu(hg) * hu
    y = jnp.dot(h.astype(BF16), wd_ref[...].astype(BF16), preferred_element_type=F32)
    gates = gates_ref[...]
    lane = lax.broadcasted_iota(jnp.int32, gates.shape, 1)
    gate = jnp.sum(jnp.where(lane == e, gates, 0.0), axis=-1, keepdims=True)
    acc_scr[...] = acc_scr[...] + gate * y

    @pl.when(e == pl.num_programs(1) - 1)
    def _():
        o_ref[...] = _layer_norm(DN_ALPHA * x_ref[...] + acc_scr[...], g_ref[...], b_ref[...])


def _moe(x2d, gates, wg, wu, wd, layer, ln_g, ln_b):
    t = x2d.shape[0]
    tm = min(MOE_TOKEN_TILE, t)
    row = lambda w: pl.BlockSpec((tm, w), lambda i, e: (i, 0))
    const = lambda shape: pl.BlockSpec(shape, lambda i, e: (0, 0))
    return pl.pallas_call(
        _moe_kernel, out_shape=jax.ShapeDtypeStruct((t, D_MODEL), F32), grid=(t // tm, N_EXPERTS),
        in_specs=[row(D_MODEL), row(LANES),
                  pl.BlockSpec((None, None, D_MODEL, D_EXPERT), lambda i, e: (layer, e, 0, 0)),
                  pl.BlockSpec((None, None, D_MODEL, D_EXPERT), lambda i, e: (layer, e, 0, 0)),
                  pl.BlockSpec((None, None, D_EXPERT, D_MODEL), lambda i, e: (layer, e, 0, 0)),
                  const((1, D_MODEL)), const((1, D_MODEL))],
        out_specs=row(D_MODEL),
        scratch_shapes=[pltpu.VMEM((tm, D_MODEL), BF16), pltpu.VMEM((tm, D_MODEL), F32)],
        compiler_params=_cparams("parallel", "arbitrary"), name="moe_ln",
    )(x2d, gates, wg, wu, wd, ln_g.reshape(1, -1), ln_b.reshape(1, -1))


def kernel(x_prompt, x_sample, cache_k, cache_v, state_ssm_re, state_ssm_im, page_table, w_in, w_out, ln1_g, ln1_b, ln2_g, ln2_b, lambda_q1, lambda_k1, lambda_q2, lambda_k2, subln_g, ssm_lambda_re, ssm_lambda_im, ssm_log_dt, ssm_b_re, ssm_b_im, ssm_c_re, ssm_c_im, ssm_d, ssm_w_glu, ssm_b_glu, router_w, router_bias, moe_w_gate, moe_w_up, moe_w_down):
    b, t, _ = x_prompt.shape
    n, dec_seq, _ = x_sample.shape
    depth = w_in.shape[0]
    n_pool = cache_k.shape[1]

    slopes = 2.0 ** (-8.0 * jnp.arange(1, N_HEADS + 1, dtype=F32) / N_HEADS)
    col = jnp.arange(LANES)
    cslope = jnp.where(col < 2 * N_HEADS * dec_seq, slopes[jnp.minimum(col // (2 * dec_seq), N_HEADS - 1)], 0.0)
    cslope = cslope.reshape(1, LANES).astype(F32)
    rw_pad = jnp.zeros((D_MODEL, LANES), F32).at[:, :N_EXPERTS].set(router_w)
    rw_hi = rw_pad.astype(BF16)
    rw_lo = (rw_pad - rw_hi.astype(F32)).astype(BF16)
    rw_cat = jnp.concatenate([rw_hi, rw_lo], axis=0)

    pages_k = cache_k.reshape(depth * n_pool, PAGE_SIZE * N_HEADS, V_DIM)
    pages_v = cache_v.reshape(depth * n_pool, PAGE_SIZE * N_HEADS, V_DIM)

    yp = x_prompt.reshape(b * t, D_MODEL)
    ys = x_sample.reshape(n * dec_seq, D_MODEL)
    outs = [[] for _ in range(8)]
    kbuf = jnp.zeros((depth * b * t * N_HEADS, V_DIM), F32)
    vbuf = jnp.zeros((depth * b * t * N_HEADS, V_DIM), F32)
    for l in range(depth):
        lam_init = 0.8 - 0.6 * math.exp(-0.3 * l)
        a_re, a_im, bb_re, bb_im, lam = _layer_prep(
            ssm_lambda_re[l], ssm_lambda_im[l], ssm_log_dt[l], ssm_b_re[l], ssm_b_im[l],
            lambda_q1[l], lambda_k1[l], lambda_q2[l], lambda_k2[l], lam_init)
        a_re, a_im = a_re.reshape(1, N_STATE), a_im.reshape(1, N_STATE)
        bbd = jnp.concatenate([_block_diag_in(bb_re), _block_diag_in(bb_im)], axis=1).astype(BF16)
        cbd = jnp.concatenate([_block_diag_out(ssm_c_re[l]), -_block_diag_out(ssm_c_im[l])], axis=0).astype(BF16)
        tail = (cbd, ssm_d[l].reshape(1, -1), ssm_w_glu[l].astype(BF16), ssm_b_glu[l].reshape(1, -1))
        w_in_b, w_out_b = w_in[l].astype(BF16), w_out[l].astype(BF16)

        def finish(x2d, a2d, s2d):
            x1, gates = _out_proj(a2d, s2d, x2d, w_out_b, ln1_g[l], ln1_b[l], rw_cat, router_bias)
            return _moe(x1, gates, moe_w_gate, moe_w_up, moe_w_down, l, ln2_g[l], ln2_b[l])

        u, q, kb, vt, kbuf, vbuf = _in_proj_prompt(yp, w_in_b, w_in_b[:, 3 * 512:].T, kbuf, vbuf, l, b)
        a = _attn_prompt(q.reshape(b, t, -1), kb.reshape(b, t, -1), vt, slopes, lam, subln_g[l], lam_init)
        s, hr, hi = _s5_prompt(u.reshape(b, t, -1), a_re, a_im, bbd, tail)
        yp = finish(yp, a.reshape(b * t, -1), s.reshape(b * t, -1))
        outs[2].append(hr.reshape(b, N_SSM_GROUPS, SSM_STATE))
        outs[3].append(hi.reshape(b, N_SSM_GROUPS, SSM_STATE))

        u, q, k, v = _in_proj_sample(ys, w_in_b)
        a = _attn_sample(q.reshape(n, dec_seq, -1), k.reshape(n, dec_seq, -1), v.reshape(n, dec_seq, -1),
                         pages_k, pages_v, l * n_pool, page_table, cslope, lam, subln_g[l], lam_init)
        s, hr, hi = _s5_sample(u.reshape(n, dec_seq, -1), state_ssm_re[l].reshape(n, N_STATE),
                               state_ssm_im[l].reshape(n, N_STATE), a_re, a_im, bbd, tail)
        ys = finish(ys, a.reshape(n * dec_seq, -1), s.reshape(n * dec_seq, -1))
        outs[4].append(k.reshape(n, dec_seq, N_HEADS, V_DIM))
        outs[5].append(v.reshape(n, dec_seq, N_HEADS, V_DIM))
        outs[6].append(hr.reshape(n, N_SSM_GROUPS, SSM_STATE))
        outs[7].append(hi.reshape(n, N_SSM_GROUPS, SSM_STATE))

    outs[0] = kbuf.reshape(depth, b, t, N_HEADS, V_DIM)
    outs[1] = vbuf.reshape(depth, b, t, N_HEADS, V_DIM)
    outs = [o if isinstance(o, jax.Array) else jnp.stack(o) for o in outs]
    return (yp.reshape(b, t, D_MODEL), ys.reshape(n, dec_seq, D_MODEL)) + tuple(outs)
```

```python
import functools
import math

import jax
import jax.numpy as jnp
from jax import lax
from jax.experimental import pallas as pl
from jax.experimental.pallas import tpu as pltpu

F32 = jnp.float32
BF16 = jnp.bfloat16

D_MODEL = 1024
ATT_WIDTH = 512
SSM_WIDTH = 512
HEAD_DIM = 64
V_DIM = 128
N_HEADS = 4
SSM_GROUP = 16
N_SSM_GROUPS = 32
SSM_STATE = 64
N_STATE = N_SSM_GROUPS * SSM_STATE
N_EXPERTS = 16
EXPERTS_PER_GROUP = 4
N_EXPERT_GROUPS = 4
D_EXPERT = 512
PAGE_SIZE = 128
DEPTH = 2
DN_ALPHA = (2 * DEPTH) ** 0.25
LN_EPS = 1e-5
RMS_EPS = 1e-5
ATTN_SCALE = HEAD_DIM ** -0.5

SUBLANES = 8
BF16_SUBLANES = 16
LANES = 128
SCAN_ROWS = SUBLANES
MOE_TOKEN_TILE = 1024
MOE_HIDDEN_SLICE = 256
PROJ_TOKEN_TILE = 1024
S5_STATE_STEPS = 128
S5_OUTPUT_STEPS = 64
VMEM_LIMIT = 52 * 1024 * 1024


def _cparams(*sem):
    return pltpu.CompilerParams(dimension_semantics=sem, vmem_limit_bytes=VMEM_LIMIT)


def _layer_norm(z, g, b):
    mu = jnp.mean(z, -1, keepdims=True)
    var = jnp.mean(jnp.square(z - mu), -1, keepdims=True)
    return (z - mu) * lax.rsqrt(var + LN_EPS) * g + b


def _prep_kernel(lre_ref, lim_ref, ldt_ref, bre_ref, bim_ref, lq1_ref, lk1_ref, lq2_ref, lk2_ref,
                 are_ref, aim_ref, bbre_ref, bbim_ref, lam_ref, *, lam_init):
    lr, li = lre_ref[...], lim_ref[...]
    dt = jnp.exp(ldt_ref[...])
    mag = jnp.exp(lr * dt)
    a_re, a_im = mag * jnp.cos(li * dt), mag * jnp.sin(li * dt)
    den = lr * lr + li * li
    nr = a_re - 1.0
    coef_re = (nr * lr + a_im * li) / den
    coef_im = (a_im * lr - nr * li) / den
    br, bi = bre_ref[...], bim_ref[...]
    bbre_ref[...] = coef_re[..., None] * br - coef_im[..., None] * bi
    bbim_ref[...] = coef_re[..., None] * bi + coef_im[..., None] * br
    are_ref[...] = a_re
    aim_ref[...] = a_im
    s1 = jnp.sum(lq1_ref[...] * lk1_ref[...], axis=-1, keepdims=True)
    s2 = jnp.sum(lq2_ref[...] * lk2_ref[...], axis=-1, keepdims=True)
    lam_ref[...] = jnp.exp(s1) - jnp.exp(s2) + lam_init


def _layer_prep(lam_re, lam_im, log_dt, b_re, b_im, lq1, lk1, lq2, lk2, lam_init):
    g, p, c = N_SSM_GROUPS, SSM_STATE, SSM_GROUP
    out_shape = (jax.ShapeDtypeStruct((g, p), F32), jax.ShapeDtypeStruct((g, p), F32),
                 jax.ShapeDtypeStruct((g, p, c), F32), jax.ShapeDtypeStruct((g, p, c), F32),
                 jax.ShapeDtypeStruct((1, 1), F32))
    return pl.pallas_call(
        functools.partial(_prep_kernel, lam_init=lam_init), out_shape=out_shape, name="layer_prep",
    )(lam_re, lam_im, log_dt.reshape(g, 1), b_re, b_im,
      lq1.reshape(1, HEAD_DIM), lk1.reshape(1, HEAD_DIM), lq2.reshape(1, HEAD_DIM), lk2.reshape(1, HEAD_DIM))


def _block_diag_in(bb):
    eye = jnp.eye(N_SSM_GROUPS, dtype=bb.dtype)
    m = jnp.transpose(bb, (0, 2, 1))[:, :, None, :] * eye[:, None, :, None]
    return m.reshape(SSM_WIDTH, N_STATE)


def _block_diag_out(c):
    eye = jnp.eye(N_SSM_GROUPS, dtype=c.dtype)
    m = jnp.transpose(c, (0, 2, 1))[:, :, None, :] * eye[:, None, :, None]
    return m.reshape(N_STATE, SSM_WIDTH)


def _proj_sample_kernel(x_ref, w_ref, u_ref, q_ref, k_ref, v_ref):
    xb = x_ref[...].astype(BF16)
    for i, o_ref in enumerate((u_ref, q_ref, k_ref, v_ref)):
        o_ref[...] = jnp.dot(xb, w_ref[:, i * 512:(i + 1) * 512], preferred_element_type=F32)


def _in_proj_sample(x2d, w_bf16):
    t = x2d.shape[0]
    tm = min(PROJ_TOKEN_TILE, t)
    out = jax.ShapeDtypeStruct((t, 512), F32)
    row = pl.BlockSpec((tm, 512), lambda i: (i, 0))
    return pl.pallas_call(
        _proj_sample_kernel, out_shape=(out,) * 4, grid=(t // tm,),
        in_specs=[pl.BlockSpec((tm, D_MODEL), lambda i: (i, 0)),
                  pl.BlockSpec((D_MODEL, 4 * 512), lambda i: (0, 0))],
        out_specs=(row,) * 4, compiler_params=_cparams("parallel"), name="in_proj_sample",
    )(x2d, w_bf16)


def _proj_prompt_kernel(x_ref, w_ref, wvt_ref, kbuf_in, vbuf_in, u_ref, q_ref, kb_ref, vt_ref, k_ref, v_ref):
    del kbuf_in, vbuf_in
    xb = x_ref[...].astype(BF16)
    tm = xb.shape[0]
    col = lambda i: jnp.dot(xb, w_ref[:, i * 512:(i + 1) * 512], preferred_element_type=F32)
    u_ref[...] = col(0)
    q_ref[...] = col(1)
    k, v = col(2), col(3)
    kb_ref[...] = k.astype(BF16)
    for h in range(N_HEADS):
        k_ref[pl.ds(h, tm, stride=N_HEADS), :] = k[:, h * V_DIM:(h + 1) * V_DIM]
        v_ref[pl.ds(h, tm, stride=N_HEADS), :] = v[:, h * V_DIM:(h + 1) * V_DIM]
    vt = lax.dot_general(wvt_ref[...], xb, (((1,), (1,)), ((), ())), preferred_element_type=F32)
    vt_ref[...] = vt.astype(BF16)


def _in_proj_prompt(x2d, w_bf16, wv_t, kbuf, vbuf, layer, n_seq):
    t = x2d.shape[0]
    seq_len = t // n_seq
    tm = min(PROJ_TOKEN_TILE, seq_len)
    assert seq_len % tm == 0
    nt, nl = seq_len // tm, t // tm
    row = pl.BlockSpec((tm, 512), lambda i: (i, 0))
    cache_row = pl.BlockSpec((tm * N_HEADS, V_DIM), lambda i: (layer * nl + i, 0))
    anyspec = pl.BlockSpec(memory_space=pl.ANY)
    out_shape = (jax.ShapeDtypeStruct((t, 512), F32), jax.ShapeDtypeStruct((t, 512), F32),
                 jax.ShapeDtypeStruct((t, 512), BF16), jax.ShapeDtypeStruct((n_seq, 512, seq_len), BF16),
                 jax.ShapeDtypeStruct(kbuf.shape, F32), jax.ShapeDtypeStruct(vbuf.shape, F32))
    out_specs = (row, row, row,
                 pl.BlockSpec((None, 512, tm), lambda i: (i // nt, 0, i % nt)), cache_row, cache_row)
    return pl.pallas_call(
        _proj_prompt_kernel, out_shape=out_shape, grid=(nl,),
        in_specs=[pl.BlockSpec((tm, D_MODEL), lambda i: (i, 0)),
                  pl.BlockSpec((D_MODEL, 4 * 512), lambda i: (0, 0)),
                  pl.BlockSpec((512, D_MODEL), lambda i: (0, 0)), anyspec, anyspec],
        out_specs=out_specs, input_output_aliases={3: 4, 4: 5},
        compiler_params=_cparams("parallel"), name="in_proj_prompt",
    )(x2d, w_bf16, wv_t, kbuf, vbuf)


def _sub_ln(a, g, lam_init):
    return a * lax.rsqrt(jnp.mean(a * a, axis=-1, keepdims=True) + RMS_EPS) * g * (1.0 - lam_init)


ALIBI_SPLIT = 64
ATTN_KEY_BLOCK = 2048
ATTN_QUERY_BLOCK = 2048
ATTN_Q_CHUNK = 512
ATTN_SCORE_LOOKAHEAD = 2


def _attn_prompt_kernel(qi_tab, ki_tab, slopes_ref, lam_ref, q_ref, kb_ref, vt_ref, kfeat_ref, vfeat_ref,
                        g_ref, o_ref, qs_scr, m_scr, acc_scr, *, tq, tk, lam_init):
    h, step = pl.program_id(1), pl.program_id(2)
    qi, ki = qi_tab[step], ki_tab[step]
    slope = slopes_ref[h]

    @pl.when(ki == 0)
    def _():
        qs = q_ref[...] * ATTN_SCALE
        lane = lax.broadcasted_iota(jnp.int32, qs.shape, 1)
        qfeat = jnp.where(lane == 0, ALIBI_SPLIT * slope, jnp.where(lane == 1, slope, 0.0)).astype(BF16)
        qs_scr[0:tq, 0:LANES] = jnp.where(lane < HEAD_DIM, qs, 0.0).astype(BF16)
        qs_scr[tq:2 * tq, 0:LANES] = jnp.where(lane >= HEAD_DIM, qs, 0.0).astype(BF16)
        qs_scr[0:tq, LANES:2 * LANES] = qfeat
        qs_scr[tq:2 * tq, LANES:2 * LANES] = qfeat
        m_scr[...] = jnp.full(m_scr.shape, -jnp.inf, F32)
        acc_scr[...] = jnp.zeros(acc_scr.shape, F32)

    def update(koff):
        kp = jnp.concatenate([kb_ref[...], kfeat_ref[...]], axis=1)
        vpt = jnp.concatenate([vt_ref[...], vfeat_ref[...]], axis=0)
        d = slope * (ki * tk - qi * tq).astype(F32)
        qc = ATTN_Q_CHUNK

        def n_keys(c0):
            return tk if koff is None else max(0, min(tk, (c0 % tq) + qc - koff))

        chunks = [c0 for c0 in range(0, 2 * tq, qc) if n_keys(c0) > 0]

        def scores(c0):
            return lax.dot_general(kp[0:n_keys(c0), :], qs_scr[c0:c0 + qc, :], (((1,), (1,)), ((), ())),
                                   preferred_element_type=F32)

        def softmax_part(s, c0):
            q0 = c0 % tq
            if koff is not None and koff + s.shape[0] - 1 > q0:
                key = lax.broadcasted_iota(jnp.int32, s.shape, 0) + koff
                qry = lax.broadcasted_iota(jnp.int32, s.shape, 1) + q0
                s = jnp.where(qry >= key, s, -jnp.inf)
            m_prev = m_scr[:, c0:c0 + qc]
            m_new = jnp.maximum(m_prev, jnp.max(s, axis=0, keepdims=True) + d)
            m_scr[:, c0:c0 + qc] = m_new
            return jnp.exp(s - (m_new - d)).astype(BF16), jnp.exp(m_prev - m_new)

        def accumulate(c0, p, alpha):
            pv = jnp.dot(vpt[:, 0:p.shape[0]], p, preferred_element_type=F32)
            acc_scr[:, c0:c0 + qc] = alpha * acc_scr[:, c0:c0 + qc] + pv

        n = len(chunks)
        s_q = [scores(c0) for c0 in chunks[:ATTN_SCORE_LOOKAHEAD]]
        pending = None
        for i, c0 in enumerate(chunks):
            s = s_q.pop(0)
            if i + ATTN_SCORE_LOOKAHEAD < n:
                s_q.append(scores(chunks[i + ATTN_SCORE_LOOKAHEAD]))
            if pending is not None:
                accumulate(*pending)
            pending = (c0,) + softmax_part(s, c0)
        accumulate(*pending)

    rel = ki - qi * (tq // tk)

    @pl.when(rel < 0)
    def _():
        update(None)

    for j in range(tq // tk):
        @pl.when(rel == j)
        def _(j=j):
            update(j * tk)

    @pl.when(rel == tq // tk - 1)
    def _():
        o1 = acc_scr[0:V_DIM, 0:tq] / acc_scr[V_DIM:V_DIM + 1, 0:tq]
        o2 = acc_scr[0:V_DIM, tq:2 * tq] / acc_scr[V_DIM:V_DIM + 1, tq:2 * tq]
        a = (o1 - lam_ref[0, 0] * o2).T
        o_ref[...] = _sub_ln(a, g_ref[...], lam_init)


def _attn_prompt(q, kb, vt, slopes, lam, subln_g, lam_init):
    b, t, _ = q.shape
    tk = min(ATTN_KEY_BLOCK, t)
    tq = min(ATTN_QUERY_BLOCK, t)
    nq, r = t // tq, tq // tk
    pairs = [(qi, ki) for qi in range(nq) for ki in range(r * (qi + 1))]
    qi_tab = jnp.asarray([p[0] for p in pairs], jnp.int32)
    ki_tab = jnp.asarray([p[1] for p in pairs], jnp.int32)
    c = jnp.arange(tk)[:, None]
    lane = jnp.arange(LANES)[None, :]
    kfeat = jnp.where(lane == 0, c // ALIBI_SPLIT, jnp.where(lane == 1, c % ALIBI_SPLIT, 0)).astype(BF16)
    ones_row = jnp.arange(BF16_SUBLANES)[:, None] == 0
    vfeat = jnp.broadcast_to(jnp.where(ones_row, 1.0, 0.0), (BF16_SUBLANES, tk)).astype(BF16)
    kernel = functools.partial(_attn_prompt_kernel, tq=tq, tk=tk, lam_init=lam_init)
    smem = pl.BlockSpec(memory_space=pltpu.SMEM)
    q_spec = pl.BlockSpec((None, tq, V_DIM), lambda b_, h, s, qt, kt: (b_, qt[s], h))
    k_spec = pl.BlockSpec((None, tk, V_DIM), lambda b_, h, s, qt, kt: (b_, kt[s], h))
    vt_spec = pl.BlockSpec((None, V_DIM, tk), lambda b_, h, s, qt, kt: (b_, h, kt[s]))
    const = lambda shape: pl.BlockSpec(shape, lambda b_, h, s, qt, kt: (0, 0))
    grid_spec = pltpu.PrefetchScalarGridSpec(
        num_scalar_prefetch=2, grid=(b, N_HEADS, len(pairs)),
        in_specs=[smem, smem, q_spec, k_spec, vt_spec, const((tk, LANES)), const((BF16_SUBLANES, tk)),
                  const((1, V_DIM))],
        out_specs=q_spec,
        scratch_shapes=[pltpu.VMEM((2 * tq, 2 * LANES), BF16), pltpu.VMEM((1, 2 * tq), F32),
                        pltpu.VMEM((V_DIM + BF16_SUBLANES, 2 * tq), F32)])
    return pl.pallas_call(
        kernel, out_shape=jax.ShapeDtypeStruct((b, t, ATT_WIDTH), F32), grid_spec=grid_spec,
        compiler_params=_cparams("parallel", "parallel", "arbitrary"), name="attn_prompt",
    )(qi_tab, ki_tab, slopes, lam, q, kb, vt, kfeat, vfeat, subln_g.reshape(1, V_DIM))


def _attn_sample_kernel(pt_ref, lam_ref, q_ref, knew_ref, vnew_ref, g_ref, cslope_ref, *rest,
                        n_pages, dec_seq, lam_init):
    del pt_ref
    k_pages = rest[:n_pages]
    v_pages = rest[n_pages:2 * n_pages]
    o_ref, s_scr = rest[2 * n_pages:]
    past = n_pages * PAGE_SIZE
    cols_per_head = 2 * dec_seq
    head_rows = lambda ref, h: ref[pl.ds(h, PAGE_SIZE, stride=N_HEADS), :]

    q = q_ref[...] * ATTN_SCALE
    lane = lax.broadcasted_iota(jnp.int32, (dec_seq, V_DIM), 1)
    qw = []
    for h in range(N_HEADS):
        qh = q[:, h * V_DIM:(h + 1) * V_DIM]
        blocks = [jnp.where(lane < HEAD_DIM, qh, 0.0), jnp.where(lane >= HEAD_DIM, qh, 0.0)]
        if h:
            blocks.insert(0, jnp.zeros((h * cols_per_head, V_DIM), F32))
        blocks.append(jnp.zeros((LANES - (h + 1) * cols_per_head, V_DIM), F32))
        qw.append(jnp.concatenate(blocks, axis=0))
    dn = (((1,), (1,)), ((), ()))

    s_past = None
    for h0 in range(0, N_HEADS, 2):
        keys = jnp.concatenate(
            [jnp.concatenate([head_rows(k_pages[p], h0), head_rows(k_pages[p], h0 + 1)], axis=1)
             for p in range(n_pages)], axis=0)
        w = jnp.concatenate([qw[h0], qw[h0 + 1]], axis=1)
        part = lax.dot_general(keys, w, dn, preferred_element_type=F32)
        s_past = part if s_past is None else s_past + part
    row = lax.broadcasted_iota(jnp.int32, (past, LANES), 0)
    col = lax.broadcasted_iota(jnp.int32, (past, LANES), 1)
    cslope = cslope_ref[...]
    dist = past + col % dec_seq - row
    s_scr[0:past, :] = s_past - cslope * dist.astype(F32)

    knew = knew_ref[...]
    s_new = None
    for h in range(N_HEADS):
        part = lax.dot_general(knew[:, h * V_DIM:(h + 1) * V_DIM], qw[h], dn, preferred_element_type=F32)
        s_new = part if s_new is None else s_new + part
    row = lax.broadcasted_iota(jnp.int32, (dec_seq, LANES), 0)
    col = lax.broadcasted_iota(jnp.int32, (dec_seq, LANES), 1)
    dist = col % dec_seq - row
    s_scr[past:past + dec_seq, :] = jnp.where(dist >= 0, s_new - cslope * dist.astype(F32), -jnp.inf)
    s_scr[past + dec_seq:past + PAGE_SIZE, :] = jnp.full((PAGE_SIZE - dec_seq, LANES), -jnp.inf, F32)

    s_all = s_scr[...]
    m = jnp.max(s_all, axis=0, keepdims=True)
    p_all = jnp.exp(s_all - m)
    l = jnp.sum(p_all, axis=0, keepdims=True)
    c_i = lax.broadcasted_iota(jnp.int32, (1, LANES), 1)
    sign = jnp.where((c_i // dec_seq) % 2 == 0, 1.0, -lam_ref[0, 0])
    s_scr[...] = p_all * (sign / l)

    vnew = vnew_ref[...]
    vzero = jnp.zeros((PAGE_SIZE - dec_seq, V_DIM), F32)
    pts = [s_scr[p * PAGE_SIZE:(p + 1) * PAGE_SIZE, :].T for p in range(n_pages + 1)]
    for h in range(N_HEADS):
        w_h = jnp.concatenate([pt[h * cols_per_head:(h + 1) * cols_per_head, :] for pt in pts], axis=1)
        v_h = jnp.concatenate([head_rows(v_pages[p], h) for p in range(n_pages)]
                              + [vnew[:, h * V_DIM:(h + 1) * V_DIM], vzero], axis=0)
        out = jnp.dot(w_h, v_h, preferred_element_type=F32)
        a = out[0:dec_seq, :] + out[dec_seq:2 * dec_seq, :]
        o_ref[:, h * V_DIM:(h + 1) * V_DIM] = _sub_ln(a, g_ref[...], lam_init)


def _attn_sample(q, k, v, k_pages, v_pages, page_base, page_table, cslope, lam, subln_g, lam_init):
    n, dec_seq, _ = q.shape
    n_pages = page_table.shape[1]
    assert 2 * N_HEADS * dec_seq <= LANES and dec_seq % SUBLANES == 0 and N_HEADS % 2 == 0
    kernel = functools.partial(_attn_sample_kernel, n_pages=n_pages, dec_seq=dec_seq, lam_init=lam_init)
    tok = pl.BlockSpec((None, dec_seq, ATT_WIDTH), lambda i, pt: (i, 0, 0))

    def page_spec(p):
        return pl.BlockSpec((None, PAGE_SIZE * N_HEADS, V_DIM),
                            lambda i, pt: (page_base + pt[i * n_pages + p], 0, 0))

    grid_spec = pltpu.PrefetchScalarGridSpec(
        num_scalar_prefetch=1, grid=(n,),
        in_specs=[pl.BlockSpec(memory_space=pltpu.SMEM), tok, tok, tok,
                  pl.BlockSpec((1, V_DIM), lambda i, pt: (0, 0)),
                  pl.BlockSpec((1, LANES), lambda i, pt: (0, 0))]
                 + [page_spec(p) for p in range(n_pages)] * 2,
        out_specs=tok,
        scratch_shapes=[pltpu.VMEM(((n_pages + 1) * PAGE_SIZE, LANES), F32)])
    return pl.pallas_call(
        kernel, out_shape=jax.ShapeDtypeStruct((n, dec_seq, ATT_WIDTH), F32), grid_spec=grid_spec,
        compiler_params=_cparams("arbitrary"), name="attn_sample",
    )(page_table.reshape(-1), lam, q, k, v, subln_g.reshape(1, V_DIM), cslope,
      *([k_pages] * n_pages), *([v_pages] * n_pages))


def _s5_scan_chunk(bu_scr, hs_scr, hre_scr, him_scr, are_ref, aim_ref, steps, lane_chunk):
    for c in range(N_STATE // lane_chunk):
        lo = c * lane_chunk
        are = jnp.broadcast_to(are_ref[:, lo:lo + lane_chunk], (SCAN_ROWS, lane_chunk))
        aim = jnp.broadcast_to(aim_ref[:, lo:lo + lane_chunk], (SCAN_ROWS, lane_chunk))

        def body(j, carry, lo=lo, are=are, aim=aim):
            hre, him = carry
            r0 = pl.multiple_of(j * SCAN_ROWS, SCAN_ROWS)
            bre = bu_scr[pl.ds(r0, SCAN_ROWS), lo:lo + lane_chunk]
            bim = bu_scr[pl.ds(r0, SCAN_ROWS), N_STATE + lo:N_STATE + lo + lane_chunk]
            nre = are * hre - aim * him + bre
            nim = are * him + aim * hre + bim
            if hs_scr is not None:
                hs_scr[pl.ds(r0, SCAN_ROWS), lo:lo + lane_chunk] = nre
                hs_scr[pl.ds(r0, SCAN_ROWS), N_STATE + lo:N_STATE + lo + lane_chunk] = nim
            return nre, nim

        hre, him = lax.fori_loop(0, steps, body,
                                 (hre_scr[:, lo:lo + lane_chunk], him_scr[:, lo:lo + lane_chunk]),
                                 unroll=min(4, steps))
        hre_scr[:, lo:lo + lane_chunk] = hre
        him_scr[:, lo:lo + lane_chunk] = him


def _s5_kernel(u_ref, h0re_ref, h0im_ref, are_ref, aim_ref, bbd_ref, *rest, steps, with_y):
    perm_scr, rest = rest[-1], rest[:-1]
    if with_y:
        cbd_ref, d_ref, wglu_ref, bglu_ref, s_ref, hre_ref, him_ref, hre_scr, him_scr, bu_scr, hs_scr = rest
    else:
        hre_ref, him_ref, hre_scr, him_scr, bu_scr = rest
        hs_scr = None
    c = pl.program_id(1)

    @pl.when(c == 0)
    def _():
        hre_scr[...] = h0re_ref[...]
        him_scr[...] = h0im_ref[...]

    n_slabs = SSM_WIDTH // LANES
    for r in range(SCAN_ROWS):
        for k in range(n_slabs):
            perm_scr[k, pl.ds(r, steps, stride=SCAN_ROWS), :] = u_ref[r, :, k * LANES:(k + 1) * LANES]
    u = jnp.concatenate([perm_scr[k] for k in range(n_slabs)], axis=1)
    ub = u.astype(BF16)
    hu, hst = SSM_WIDTH // 2, N_STATE // 2
    for half in range(2):
        for part in range(2):
            cols = slice(part * N_STATE + half * hst, part * N_STATE + (half + 1) * hst)
            bu_scr[:, cols] = jnp.dot(ub[:, half * hu:(half + 1) * hu], bbd_ref[half * hu:(half + 1) * hu, cols],
                                      preferred_element_type=F32)
    _s5_scan_chunk(bu_scr, hs_scr, hre_scr, him_scr, are_ref, aim_ref, steps, lane_chunk=1024)
    if with_y:
        y_halves = []
        for half in range(2):
            ycols = slice(half * hu, (half + 1) * hu)
            acc = None
            for part in range(2):
                rows = slice(part * N_STATE + half * hst, part * N_STATE + (half + 1) * hst)
                term = jnp.dot(hs_scr[:, rows].astype(BF16), cbd_ref[rows, ycols], preferred_element_type=F32)
                acc = term if acc is None else acc + term
            y_halves.append(acc)
        y = jnp.concatenate(y_halves, axis=1) + d_ref[...] * u
        g = jax.nn.gelu(y)
        gate = jnp.dot(g.astype(BF16), wglu_ref[...], preferred_element_type=F32) + bglu_ref[...]
        s = g * jax.nn.sigmoid(gate)
        for k in range(n_slabs):
            perm_scr[k] = s[:, k * LANES:(k + 1) * LANES]
        for r in range(SCAN_ROWS):
            rows = [perm_scr[k, pl.ds(r, steps, stride=SCAN_ROWS), :] for k in range(n_slabs)]
            s_ref[r] = jnp.concatenate(rows, axis=1).astype(s_ref.dtype)

    @pl.when(c == pl.num_programs(1) - 1)
    def _():
        hre_ref[...] = hre_scr[...]
        him_ref[...] = him_scr[...]


def _s5_call(u4, h0re, h0im, a_re, a_im, bbd, tail, steps_per_chunk, with_y):
    ng, _, j_total, _ = u4.shape
    steps = min(steps_per_chunk, j_total)
    nc = j_total // steps
    blk = steps * SCAN_ROWS
    const = lambda shape: pl.BlockSpec(shape, lambda g, c: (0,) * len(shape))
    st_spec = pl.BlockSpec((None, SCAN_ROWS, N_STATE), lambda g, c: (g, 0, 0))
    st_shape = jax.ShapeDtypeStruct((ng, SCAN_ROWS, N_STATE), F32)
    u_spec = pl.BlockSpec((None, SCAN_ROWS, steps, SSM_WIDTH), lambda g, c: (g, 0, c, 0))
    in_specs = [u_spec, st_spec, st_spec, const((1, N_STATE)), const((1, N_STATE)),
                const((SSM_WIDTH, 2 * N_STATE))]
    scratch = [pltpu.VMEM((SCAN_ROWS, N_STATE), F32), pltpu.VMEM((SCAN_ROWS, N_STATE), F32),
               pltpu.VMEM((blk, 2 * N_STATE), F32)]
    if with_y:
        in_specs += [const((2 * N_STATE, SSM_WIDTH)), const((1, SSM_WIDTH)),
                     const((SSM_WIDTH, SSM_WIDTH)), const((1, SSM_WIDTH))]
        out_shape = (jax.ShapeDtypeStruct(u4.shape, BF16), st_shape, st_shape)
        out_specs = (u_spec, st_spec, st_spec)
        scratch.append(pltpu.VMEM((blk, 2 * N_STATE), F32))
    else:
        out_shape = (st_shape, st_shape)
        out_specs = (st_spec, st_spec)
    scratch.append(pltpu.VMEM((SSM_WIDTH // LANES, blk, LANES), F32))
    return pl.pallas_call(
        functools.partial(_s5_kernel, steps=steps, with_y=with_y),
        out_shape=out_shape, grid=(ng, nc), in_specs=in_specs, out_specs=out_specs,
        scratch_shapes=scratch, compiler_params=_cparams("parallel", "arbitrary"),
        name="s5_scan_glu" if with_y else "s5_scan_state",
    )(u4, h0re, h0im, a_re, a_im, bbd, *tail)


def _s5_seg_init_kernel(fre_ref, fim_ref, are_ref, aim_ref, ire_ref, iim_ref, *, seg_len, n_seq, n_seg):
    pr, pi = are_ref[...], aim_ref[...]
    ar, ai = None, None
    e = seg_len
    while e:
        if e & 1:
            ar, ai = (pr, pi) if ar is None else (ar * pr - ai * pi, ar * pi + ai * pr)
        e >>= 1
        if e:
            pr, pi = pr * pr - pi * pi, 2.0 * pr * pi
    for n in range(n_seq):
        hr = jnp.zeros((1, N_STATE), F32)
        hi = jnp.zeros((1, N_STATE), F32)
        for s in range(n_seg):
            r = n * n_seg + s
            ire_ref[r:r + 1, :] = hr
            iim_ref[r:r + 1, :] = hi
            fr, fi = fre_ref[r:r + 1, :], fim_ref[r:r + 1, :]
            hr, hi = ar * hr - ai * hi + fr, ar * hi + ai * hr + fi


def _s5_seg_init(f_re, f_im, a_re, a_im, seg_len, n_seq, n_seg):
    shape = jax.ShapeDtypeStruct(f_re.shape, F32)
    return pl.pallas_call(
        functools.partial(_s5_seg_init_kernel, seg_len=seg_len, n_seq=n_seq, n_seg=n_seg),
        out_shape=(shape, shape), name="s5_seg_init",
    )(f_re, f_im, a_re, a_im)


def _s5_prompt(u, a_re, a_im, bbd, tail):
    b, t, _ = u.shape
    n_seg = SCAN_ROWS // b
    assert b * n_seg == SCAN_ROWS and t % n_seg == 0
    j = t // n_seg
    u4 = u.reshape(1, SCAN_ROWS, j, SSM_WIDTH)
    zeros = jnp.zeros((1, SCAN_ROWS, N_STATE), F32)
    f_re, f_im = _s5_call(u4, zeros, zeros, a_re, a_im, bbd, (), S5_STATE_STEPS, with_y=False)
    i_re, i_im = _s5_seg_init(f_re[0], f_im[0], a_re, a_im, j, b, n_seg)
    s4, h_re, h_im = _s5_call(u4, i_re[None], i_im[None], a_re, a_im, bbd, tail, S5_OUTPUT_STEPS,
                              with_y=True)
    last = h_re.reshape(b, n_seg, N_STATE)[:, -1], h_im.reshape(b, n_seg, N_STATE)[:, -1]
    return s4.reshape(b, t, SSM_WIDTH), last[0], last[1]


def _s5_sample(u, h0_re, h0_im, a_re, a_im, bbd, tail):
    n, j, _ = u.shape
    ng = n // SCAN_ROWS
    s4, h_re, h_im = _s5_call(u.reshape(ng, SCAN_ROWS, j, SSM_WIDTH), h0_re.reshape(ng, SCAN_ROWS, N_STATE),
                              h0_im.reshape(ng, SCAN_ROWS, N_STATE), a_re, a_im, bbd, tail, j, with_y=True)
    return s4.reshape(n, j, SSM_WIDTH), h_re.reshape(n, N_STATE), h_im.reshape(n, N_STATE)


def _top2_sum(a, b, c, d):
    m1, n1 = jnp.maximum(a, b), jnp.minimum(a, b)
    m2, n2 = jnp.maximum(c, d), jnp.minimum(c, d)
    return jnp.maximum(m1, m2) + jnp.maximum(jnp.minimum(m1, m2), jnp.maximum(n1, n2))


def _route_rows(logit_rows, rbias_ref):
    scores = [jax.nn.sigmoid(r) for r in logit_rows]
    sel = [scores[e] + rbias_ref[e] for e in range(N_EXPERTS)]
    gs = [_top2_sum(*sel[EXPERTS_PER_GROUP * g:EXPERTS_PER_GROUP * (g + 1)]) for g in range(N_EXPERT_GROUPS)]
    best_v, best_g = gs[0], jnp.zeros(gs[0].shape, jnp.int32)
    for g in range(1, N_EXPERT_GROUPS):
        upd = gs[g] > best_v
        best_g = jnp.where(upd, g, best_g)
        best_v = jnp.where(upd, gs[g], best_v)
    picked = []
    for e in range(N_EXPERTS):
        g = e // EXPERTS_PER_GROUP
        rank = jnp.zeros(sel[e].shape, jnp.int32)
        for o in range(EXPERTS_PER_GROUP * g, EXPERTS_PER_GROUP * (g + 1)):
            if o != e:
                ahead = (sel[o] >= sel[e]) if o < e else (sel[o] > sel[e])
                rank = rank + ahead.astype(jnp.int32)
        on = jnp.logical_and(best_g == g, rank < 2)
        picked.append(jnp.where(on, scores[e], 0.0))
    den = picked[0]
    for e in range(1, N_EXPERTS):
        den = den + picked[e]
    return [w / den for w in picked]


def _outproj_kernel(rbias_ref, a_ref, s_ref, x_ref, wo_ref, g_ref, b_ref, rw_ref, x1_ref, gates_ref):
    mix = (jnp.dot(a_ref[...].astype(BF16), wo_ref[0:ATT_WIDTH, :], preferred_element_type=F32)
           + jnp.dot(s_ref[...].astype(BF16), wo_ref[ATT_WIDTH:, :], preferred_element_type=F32))
    x1 = _layer_norm(DN_ALPHA * x_ref[...] + mix, g_ref[...], b_ref[...])
    x1_ref[...] = x1
    xh = x1.astype(BF16)
    xl = (x1 - xh.astype(F32)).astype(BF16)
    rwh, rwl = rw_ref[0:D_MODEL, :], rw_ref[D_MODEL:, :]
    logits = (jnp.dot(xh, rwh, preferred_element_type=F32)
              + jnp.dot(xl, rwh, preferred_element_type=F32)
              + jnp.dot(xh, rwl, preferred_element_type=F32))
    lt = logits.T
    gate_rows = _route_rows([lt[e:e + 1, :] for e in range(N_EXPERTS)], rbias_ref)
    tm = lt.shape[1]
    r_i = lax.broadcasted_iota(jnp.int32, (N_EXPERTS, tm), 0)
    gt = jnp.zeros((N_EXPERTS, tm), F32)
    for e in range(N_EXPERTS):
        gt = jnp.where(r_i == e, jnp.broadcast_to(gate_rows[e], (N_EXPERTS, tm)), gt)
    gt = jnp.concatenate([gt, jnp.zeros((LANES - N_EXPERTS, tm), F32)], axis=0)
    gates_ref[...] = gt.T


def _out_proj(a2d, s, x2d, wo_bf16, ln_g, ln_b, rw_cat, router_bias):
    t = x2d.shape[0]
    tm = min(PROJ_TOKEN_TILE, t)
    row = lambda w: pl.BlockSpec((tm, w), lambda i: (i, 0))
    const = lambda shape: pl.BlockSpec(shape, lambda i: (0, 0))
    s_spec = row(SSM_WIDTH)
    return pl.pallas_call(
        _outproj_kernel,
        out_shape=(jax.ShapeDtypeStruct((t, D_MODEL), F32), jax.ShapeDtypeStruct((t, LANES), F32)),
        grid=(t // tm,),
        in_specs=[pl.BlockSpec(memory_space=pltpu.SMEM), row(ATT_WIDTH), s_spec, row(D_MODEL),
                  const((D_MODEL, D_MODEL)), const((1, D_MODEL)), const((1, D_MODEL)),
                  const((2 * D_MODEL, LANES))],
        out_specs=(row(D_MODEL), row(LANES)), compiler_params=_cparams("parallel"), name="out_proj_router",
    )(router_bias, a2d, s, x2d, wo_bf16, ln_g.reshape(1, -1), ln_b.reshape(1, -1), rw_cat)


def _moe_kernel(x_ref, gates_ref, wg_ref, wu_ref, wd_ref, g_ref, b_ref, o_ref, xb_scr, acc_scr):
    e = pl.program_id(1)

    @pl.when(e == 0)
    def _():
        xb_scr[...] = x_ref[...].astype(BF16)
        acc_scr[...] = jnp.zeros(acc_scr.shape, F32)

    xb = xb_scr[...]
    width = MOE_HIDDEN_SLICE
    slices = [slice(c0, c0 + width) for c0 in range(0, D_EXPERT, width)]

    def gate_up(cols):
        return (jnp.dot(xb, wg_ref[:, cols].astype(BF16), preferred_element_type=F32),
                jnp.dot(xb, wu_ref[:, cols].astype(BF16), preferred_element_type=F32))

    y = None
    ahead = gate_up(slices[0])
    for i, cols in enumerate(slices):
        hg, hu = ahead
        if i + 1 < len(slices):
            ahead = gate_up(slices[i + 1])
        h = (jax.nn.silu(hg) * hu).astype(BF16)
        part = jnp.dot(h, wd_ref[cols, :].astype(BF16), preferred_element_type=F32)
        y = part if y is None else y + part
    gates = gates_ref[...]
    lane = lax.broadcasted_iota(jnp.int32, gates.shape, 1)
    gate = jnp.sum(jnp.where(lane == e, gates, 0.0), axis=-1, keepdims=True)
    acc_scr[...] = acc_scr[...] + gate * y

    @pl.when(e == pl.num_programs(1) - 1)
    def _():
        o_ref[...] = _layer_norm(DN_ALPHA * x_ref[...] + acc_scr[...], g_ref[...], b_ref[...])


def _moe(x2d, gates, wg, wu, wd, layer, ln_g, ln_b):
    t = x2d.shape[0]
    tm = min(MOE_TOKEN_TILE, t)
    row = lambda w: pl.BlockSpec((tm, w), lambda i, e: (i, 0))
    const = lambda shape: pl.BlockSpec(shape, lambda i, e: (0, 0))
    return pl.pallas_call(
        _moe_kernel, out_shape=jax.ShapeDtypeStruct((t, D_MODEL), F32), grid=(t // tm, N_EXPERTS),
        in_specs=[row(D_MODEL), row(LANES),
                  pl.BlockSpec((None, None, D_MODEL, D_EXPERT), lambda i, e: (layer, e, 0, 0)),
                  pl.BlockSpec((None, None, D_MODEL, D_EXPERT), lambda i, e: (layer, e, 0, 0)),
                  pl.BlockSpec((None, None, D_EXPERT, D_MODEL), lambda i, e: (layer, e, 0, 0)),
                  const((1, D_MODEL)), const((1, D_MODEL))],
        out_specs=row(D_MODEL),
        scratch_shapes=[pltpu.VMEM((tm, D_MODEL), BF16), pltpu.VMEM((tm, D_MODEL), F32)],
        compiler_params=_cparams("parallel", "arbitrary"), name="moe_ln",
    )(x2d, gates, wg, wu, wd, ln_g.reshape(1, -1), ln_b.reshape(1, -1))


def kernel(x_prompt, x_sample, cache_k, cache_v, state_ssm_re, state_ssm_im, page_table, w_in, w_out, ln1_g, ln1_b, ln2_g, ln2_b, lambda_q1, lambda_k1, lambda_q2, lambda_k2, subln_g, ssm_lambda_re, ssm_lambda_im, ssm_log_dt, ssm_b_re, ssm_b_im, ssm_c_re, ssm_c_im, ssm_d, ssm_w_glu, ssm_b_glu, router_w, router_bias, moe_w_gate, moe_w_up, moe_w_down):
    b, t, _ = x_prompt.shape
    n, dec_seq, _ = x_sample.shape
    depth = w_in.shape[0]
    n_pool = cache_k.shape[1]

    slopes = 2.0 ** (-8.0 * jnp.arange(1, N_HEADS + 1, dtype=F32) / N_HEADS)
    col = jnp.arange(LANES)
    cslope = jnp.where(col < 2 * N_HEADS * dec_seq, slopes[jnp.minimum(col // (2 * dec_seq), N_HEADS - 1)], 0.0)
    cslope = cslope.reshape(1, LANES).astype(F32)
    rw_pad = jnp.zeros((D_MODEL, LANES), F32).at[:, :N_EXPERTS].set(router_w)
    rw_hi = rw_pad.astype(BF16)
    rw_lo = (rw_pad - rw_hi.astype(F32)).astype(BF16)
    rw_cat = jnp.concatenate([rw_hi, rw_lo], axis=0)

    pages_k = cache_k.reshape(depth * n_pool, PAGE_SIZE * N_HEADS, V_DIM)
    pages_v = cache_v.reshape(depth * n_pool, PAGE_SIZE * N_HEADS, V_DIM)

    yp = x_prompt.reshape(b * t, D_MODEL)
    ys = x_sample.reshape(n * dec_seq, D_MODEL)
    outs = [[] for _ in range(8)]
    kbuf = jnp.zeros((depth * b * t * N_HEADS, V_DIM), F32)
    vbuf = jnp.zeros((depth * b * t * N_HEADS, V_DIM), F32)
    for l in range(depth):
        lam_init = 0.8 - 0.6 * math.exp(-0.3 * l)
        a_re, a_im, bb_re, bb_im, lam = _layer_prep(
            ssm_lambda_re[l], ssm_lambda_im[l], ssm_log_dt[l], ssm_b_re[l], ssm_b_im[l],
            lambda_q1[l], lambda_k1[l], lambda_q2[l], lambda_k2[l], lam_init)
        a_re, a_im = a_re.reshape(1, N_STATE), a_im.reshape(1, N_STATE)
        bbd = jnp.concatenate([_block_diag_in(bb_re), _block_diag_in(bb_im)], axis=1).astype(BF16)
        cbd = jnp.concatenate([_block_diag_out(ssm_c_re[l]), -_block_diag_out(ssm_c_im[l])], axis=0).astype(BF16)
        tail = (cbd, ssm_d[l].reshape(1, -1), ssm_w_glu[l].astype(BF16), ssm_b_glu[l].reshape(1, -1))
        w_in_b, w_out_b = w_in[l].astype(BF16), w_out[l].astype(BF16)

        def finish(x2d, a2d, s2d):
            x1, gates = _out_proj(a2d, s2d, x2d, w_out_b, ln1_g[l], ln1_b[l], rw_cat, router_bias)
            return _moe(x1, gates, moe_w_gate, moe_w_up, moe_w_down, l, ln2_g[l], ln2_b[l])

        u, q, kb, vt, kbuf, vbuf = _in_proj_prompt(yp, w_in_b, w_in_b[:, 3 * 512:].T, kbuf, vbuf, l, b)
        a = _attn_prompt(q.reshape(b, t, -1), kb.reshape(b, t, -1), vt, slopes, lam, subln_g[l], lam_init)
        s, hr, hi = _s5_prompt(u.reshape(b, t, -1), a_re, a_im, bbd, tail)
        yp = finish(yp, a.reshape(b * t, -1), s.reshape(b * t, -1))
        outs[2].append(hr.reshape(b, N_SSM_GROUPS, SSM_STATE))
        outs[3].append(hi.reshape(b, N_SSM_GROUPS, SSM_STATE))

        u, q, k, v = _in_proj_sample(ys, w_in_b)
        a = _attn_sample(q.reshape(n, dec_seq, -1), k.reshape(n, dec_seq, -1), v.reshape(n, dec_seq, -1),
                         pages_k, pages_v, l * n_pool, page_table, cslope, lam, subln_g[l], lam_init)
        s, hr, hi = _s5_sample(u.reshape(n, dec_seq, -1), state_ssm_re[l].reshape(n, N_STATE),
                               state_ssm_im[l].reshape(n, N_STATE), a_re, a_im, bbd, tail)
        ys = finish(ys, a.reshape(n * dec_seq, -1), s.reshape(n * dec_seq, -1))
        outs[4].append(k.reshape(n, dec_seq, N_HEADS, V_DIM))
        outs[5].append(v.reshape(n, dec_seq, N_HEADS, V_DIM))
        outs[6].append(hr.reshape(n, N_SSM_GROUPS, SSM_STATE))
        outs[7].append(hi.reshape(n, N_SSM_GROUPS, SSM_STATE))

    outs[0] = kbuf.reshape(depth, b, t, N_HEADS, V_DIM)
    outs[1] = vbuf.reshape(depth, b, t, N_HEADS, V_DIM)
    outs = [o if isinstance(o, jax.Array) else jnp.stack(o) for o in outs]
    return (yp.reshape(b, t, D_MODEL), ys.reshape(n, dec_seq, D_MODEL)) + tuple(outs)
```
